```python
import math
import jax, jax.numpy as jnp
from jax import lax
import numpy as np

D_MODEL = 1024
BATCH = 16
SEQ = 4096
DEPTH = 1
DEC_BATCH = 4
DEC_SEQ = 4096
PAST_LEN = 128

N_META = 16
D_MIX = D_MODEL
RMS_EPS = 1e-6
DA_HEADS = 4
DA_WIDTH = D_MIX // 2
DA_V_DIM = DA_WIDTH // DA_HEADS
DA_QK_DIM = DA_V_DIM // 2
ROPE_DIM = DA_QK_DIM // 4
ROPE_THETA = 500000.0
Q_BLOCK = 128
ML_WIDTH = D_MIX - DA_WIDTH
ML_HEADS = 4
ML_HEAD_DIM = ML_WIDTH // ML_HEADS
CHUNK = 64
CONV_W = 3
NEG_BIG = -1e30
OFF_DA_Q = 0
OFF_DA_K = OFF_DA_Q + DA_WIDTH
OFF_DA_V = OFF_DA_K + DA_WIDTH
OFF_ML_QK = OFF_DA_V + DA_WIDTH
OFF_ML_V = OFF_ML_QK + 2 * ML_WIDTH
OFF_ML_O = OFF_ML_V + ML_WIDTH
OFF_GATES = OFF_ML_O + ML_WIDTH
N_GATES = 4 * ML_HEADS
D_IN = OFF_GATES + N_GATES
N_EXPERTS = 16
CAP_FACTOR = 2
D_FF_EXPERT = 2816

kernel_name = "hymba_diffattn_mlstm_ec_encoder"


def _rmsnorm(x, g):
    xf = x.astype(jnp.float32)
    y = xf * lax.rsqrt(jnp.mean(xf * xf, axis=-1, keepdims=True) + RMS_EPS)
    return (y * g.astype(jnp.float32)).astype(x.dtype)


def _lambda_init(layer_idx):
    return 0.8 - 0.6 * math.exp(-0.3 * layer_idx)


def _rope_tables(L):
    pos = jnp.arange(L, dtype=jnp.float32)
    inv = ROPE_THETA ** (-jnp.arange(0, ROPE_DIM, 2, dtype=jnp.float32) / ROPE_DIM)
    ang = pos[:, None] * inv[None, :]
    return jnp.cos(ang), jnp.sin(ang)


def _partial_rope(x, cos, sin):
    half = ROPE_DIM // 2
    x1 = x[..., :half]
    x2 = x[..., half:ROPE_DIM]
    c = cos.astype(x.dtype)
    s = sin.astype(x.dtype)
    rot = jnp.concatenate([x1 * c - x2 * s, x2 * c + x1 * s], axis=-1)
    return jnp.concatenate([rot, x[..., ROPE_DIM:]], axis=-1)


def _diff_attention(q, k, v, lam):
    B, H, _, L, dq = q.shape
    nb = -(-L // Q_BLOCK)
    Lq = nb * Q_BLOCK
    qp = jnp.pad(q, ((0, 0), (0, 0), (0, 0), (0, Lq - L), (0, 0)))
    qb = jnp.moveaxis(qp.reshape(B, H, 2, nb, Q_BLOCK, dq), 3, 0)

    def block(qblk):
        s = jnp.einsum('bhjqd,bhjkd->bhjqk', qblk, k).astype(jnp.float32)
        p = jax.nn.softmax(s, axis=-1)
        a = p[:, :, 0] - lam * p[:, :, 1]
        return jnp.einsum('bhqk,bhkd->bhqd', a.astype(v.dtype), v)

    out = lax.map(block, qb)
    out = jnp.moveaxis(out, 0, 2).reshape(B, H, Lq, v.shape[-1])
    return out[:, :, :L]


def _mlstm_chunk_scan(q, k, v, log_i, log_f):
    B, H, T, dk = q.shape
    dv = v.shape[-1]
    nc = T // CHUNK

    def to_chunks(a):
        return jnp.moveaxis(a.reshape(a.shape[:2] + (nc, CHUNK) + a.shape[3:]), 2, 0)

    mask = jnp.tril(jnp.ones((CHUNK, CHUNK), dtype=bool))

    def step(carry, inp):
        C, n, m = carry
        qc, kc, vc, li, lf = inp
        b = jnp.cumsum(lf, axis=-1)
        Dm = jnp.where(mask, b[..., :, None] - b[..., None, :] + li[..., None, :], -jnp.inf)
        inter = b + m[..., None]
        m_t = jnp.maximum(jnp.max(Dm, axis=-1), inter)
        W = jnp.exp(Dm - m_t[..., None])
        g = jnp.exp(inter - m_t)
        S = jnp.einsum('bhtd,bhsd->bhts', qc, kc) * W
        num = jnp.einsum('bhts,bhsd->bhtd', S, vc) + g[..., None] * jnp.einsum('bhtd,bhde->bhte', qc, C)
        den = jnp.sum(S, axis=-1) + g * jnp.einsum('bhtd,bhd->bht', qc, n)
        h = num / jnp.maximum(jnp.abs(den), jnp.exp(-m_t))[..., None]
        bL = b[..., -1]
        a = bL[..., None] - b + li
        m_new = jnp.maximum(bL + m, jnp.max(a, axis=-1))
        wa = jnp.exp(a - m_new[..., None])
        gd = jnp.exp(bL + m - m_new)
        C_new = gd[..., None, None] * C + jnp.einsum('bhs,bhsd,bhse->bhde', wa, kc, vc)
        n_new = gd[..., None] * n + jnp.einsum('bhs,bhsd->bhd', wa, kc)
        return (C_new, n_new, m_new), h

    init = (jnp.zeros((B, H, dk, dv), jnp.float32),
            jnp.zeros((B, H, dk), jnp.float32),
            jnp.zeros((B, H), jnp.float32))
    _, hs = lax.scan(step, init, (to_chunks(q), to_chunks(k), to_chunks(v),
                                  to_chunks(log_i), to_chunks(log_f)))
    return jnp.moveaxis(hs, 0, 2).reshape(B, H, T, dv)


def _bidir_mlstm(q, k, v, i_fw, f_fw, i_bw, f_bw):
    pad = (-N_META) % CHUNK
    pw3 = ((0, 0), (0, 0), (pad, 0))
    pw4 = pw3 + ((0, 0),)
    qp, kp, vp = jnp.pad(q, pw4), jnp.pad(k, pw4), jnp.pad(v, pw4)
    li_fw = jnp.pad(i_fw, pw3, constant_values=NEG_BIG)
    lf_fw = jnp.pad(jax.nn.log_sigmoid(f_fw), pw3)
    li_bw = jnp.pad(i_bw, pw3, constant_values=NEG_BIG)
    lf_bw = jnp.pad(jax.nn.log_sigmoid(f_bw), pw3)
    flip = lambda a: jnp.flip(a, axis=2)
    h_fw = _mlstm_chunk_scan(qp, kp, vp, li_fw, lf_fw)
    h_bw = flip(_mlstm_chunk_scan(flip(qp), flip(kp), flip(vp), flip(li_bw), flip(lf_bw)))
    return (h_fw + h_bw)[:, :, pad:]


def _centred_conv(x, w):
    L = x.shape[1]
    r = CONV_W // 2
    xp = jnp.pad(x, ((0, 0), (r, r), (0, 0)))
    y = xp[:, 0:L] * w[0]
    for j in range(1, CONV_W):
        y = y + xp[:, j:j + L] * w[j]
    return y


def _expert_choice_moe(h, w_router, w_gate, w_up, w_down):
    B, L, D = h.shape
    T = B * L
    xt = h.reshape(T, D)
    aff = jax.nn.softmax((xt @ w_router).astype(jnp.float32), axis=-1)
    cap = max(1, CAP_FACTOR * T // N_EXPERTS)
    gate, idx = lax.top_k(aff.T, cap)

    def expert(args):
        ids, wg, wu, wd = args
        xg = xt[ids]
        return (jax.nn.silu(xg @ wg) * (xg @ wu)) @ wd

    ye = lax.map(expert, (idx, w_gate, w_up, w_down))
    ye = ye * gate[..., None].astype(ye.dtype)
    y = jnp.zeros((T, D), h.dtype).at[idx.reshape(-1)].add(ye.reshape(-1, D))
    return y.reshape(B, L, D)


def _layer(x, cos, sin, lam_init, g_mix, w_in, b_gates, conv_w, g_q, g_k, lam_q1, lam_k1, lam_q2, lam_k2,
           g_da_out, g_ml_out, w_out, g_ffn, w_router, w_gate, w_up, w_down):
    B, L, _ = x.shape
    h = _rmsnorm(x, g_mix)
    z = h @ w_in

    dq = z[..., OFF_DA_Q:OFF_DA_K].reshape(B, L, DA_HEADS, 2, DA_QK_DIM)
    dk = z[..., OFF_DA_K:OFF_DA_V].reshape(B, L, DA_HEADS, 2, DA_QK_DIM)
    dv = z[..., OFF_DA_V:OFF_ML_QK].reshape(B, L, DA_HEADS, DA_V_DIM)
    dq = _rmsnorm(dq, g_q).transpose(0, 2, 3, 1, 4)
    dk = _rmsnorm(dk, g_k).transpose(0, 2, 3, 1, 4)
    dq = _partial_rope(dq, cos, sin) * (DA_QK_DIM ** -0.5)
    dk = _partial_rope(dk, cos, sin)
    dv = dv.transpose(0, 2, 1, 3)
    f32 = jnp.float32
    lam = (jnp.exp(jnp.sum(lam_q1.astype(f32) * lam_k1.astype(f32)))
           - jnp.exp(jnp.sum(lam_q2.astype(f32) * lam_k2.astype(f32))) + lam_init)
    oa = _diff_attention(dq, dk, dv, lam)
    oa = _rmsnorm(oa, g_da_out) * (1.0 - lam_init)
    oa = oa.transpose(0, 2, 1, 3).reshape(B, L, DA_WIDTH)

    mqk = jax.nn.silu(_centred_conv(z[..., OFF_ML_QK:OFF_ML_V], conv_w))
    mq = mqk[..., :ML_WIDTH].reshape(B, L, ML_HEADS, ML_HEAD_DIM).transpose(0, 2, 1, 3).astype(f32)
    mk = mqk[..., ML_WIDTH:].reshape(B, L, ML_HEADS, ML_HEAD_DIM).transpose(0, 2, 1, 3).astype(f32)
    mv = z[..., OFF_ML_V:OFF_ML_O].reshape(B, L, ML_HEADS, ML_HEAD_DIM).transpose(0, 2, 1, 3).astype(f32)
    mo = z[..., OFF_ML_O:OFF_GATES]
    gates = (z[..., OFF_GATES:] + b_gates).astype(f32).reshape(B, L, 4, ML_HEADS).transpose(2, 0, 3, 1)
    hm = _bidir_mlstm(mq * (ML_HEAD_DIM ** -0.5), mk, mv, gates[0], gates[1], gates[2], gates[3])
    hm = _rmsnorm(hm, g_ml_out)
    hm = hm.transpose(0, 2, 1, 3).reshape(B, L, ML_WIDTH).astype(x.dtype) * jax.nn.sigmoid(mo)

    x = x + jnp.concatenate([oa, hm], axis=-1) @ w_out
    x = x + _expert_choice_moe(_rmsnorm(x, g_ffn), w_router, w_gate, w_up, w_down)
    return x


def _trunk(x, meta, g_mix, w_in, b_gates, conv_w, g_q, g_k, lam_q1, lam_k1, lam_q2, lam_k2,
           g_da_out, g_ml_out, w_out, g_ffn, w_router, w_gate, w_up, w_down):
    B = x.shape[0]
    m = jnp.broadcast_to(meta.astype(x.dtype)[None], (B, N_META, x.shape[-1]))
    h = jnp.concatenate([m, x], axis=1)
    cos, sin = _rope_tables(h.shape[1])
    for l in range(DEPTH):
        h = _layer(h, cos, sin, _lambda_init(l), g_mix[l], w_in[l], b_gates[l], conv_w[l], g_q[l], g_k[l],
                   lam_q1[l], lam_k1[l], lam_q2[l], lam_k2[l], g_da_out[l], g_ml_out[l], w_out[l],
                   g_ffn[l], w_router[l], w_gate[l], w_up[l], w_down[l])
    return h[:, N_META:]


def setup_inputs(seed: int = 0) -> dict:
    key = jax.random.key(seed)
    ks = jax.random.split(key, 24)
    nrm = lambda k, s: jax.random.normal(k, s, jnp.float32)
    bi = 0.1 * nrm(ks[5], (DEPTH, 2, ML_HEADS))
    bf = jnp.linspace(3.0, 6.0, ML_HEADS, dtype=jnp.float32)[None, None] + 0.1 * nrm(ks[6], (DEPTH, 2, ML_HEADS))
    b_gates = jnp.concatenate([bi[:, 0], bf[:, 0], bi[:, 1], bf[:, 1]], axis=-1)
    return {
        "x_prompt": nrm(ks[0], (BATCH, SEQ, D_MODEL)),
        "x_sample": nrm(ks[1], (DEC_BATCH, DEC_SEQ, D_MODEL)),
        "meta": nrm(ks[2], (N_META, D_MODEL)),
        "g_mix": 1.0 + 0.1 * nrm(ks[3], (DEPTH, D_MODEL)),
        "w_in": nrm(ks[4], (DEPTH, D_MODEL, D_IN)) * D_MODEL ** -0.5,
        "b_gates": b_gates,
        "conv_w": nrm(ks[7], (DEPTH, CONV_W, 2 * ML_WIDTH)) * CONV_W ** -0.5,
        "g_q": 1.0 + 0.1 * nrm(ks[8], (DEPTH, DA_QK_DIM)),
        "g_k": 1.0 + 0.1 * nrm(ks[9], (DEPTH, DA_QK_DIM)),
        "lam_q1": 0.1 * nrm(ks[10], (DEPTH, DA_QK_DIM)),
        "lam_k1": 0.1 * nrm(ks[11], (DEPTH, DA_QK_DIM)),
        "lam_q2": 0.1 * nrm(ks[12], (DEPTH, DA_QK_DIM)),
        "lam_k2": 0.1 * nrm(ks[13], (DEPTH, DA_QK_DIM)),
        "g_da_out": 1.0 + 0.1 * nrm(ks[14], (DEPTH, DA_V_DIM)),
        "g_ml_out": 1.0 + 0.1 * nrm(ks[15], (DEPTH, ML_HEAD_DIM)),
        "w_out": nrm(ks[16], (DEPTH, D_MIX, D_MODEL)) * D_MIX ** -0.5,
        "g_ffn": 1.0 + 0.1 * nrm(ks[17], (DEPTH, D_MODEL)),
        "w_router": nrm(ks[18], (DEPTH, D_MODEL, N_EXPERTS)) * D_MODEL ** -0.5,
        "w_gate": nrm(ks[19], (DEPTH, N_EXPERTS, D_MODEL, D_FF_EXPERT)) * D_MODEL ** -0.5,
        "w_up": nrm(ks[20], (DEPTH, N_EXPERTS, D_MODEL, D_FF_EXPERT)) * D_MODEL ** -0.5,
        "w_down": nrm(ks[21], (DEPTH, N_EXPERTS, D_FF_EXPERT, D_MODEL)) * D_FF_EXPERT ** -0.5,
    }


def reference(x_prompt, x_sample, meta, g_mix, w_in, b_gates, conv_w, g_q, g_k, lam_q1, lam_k1, lam_q2, lam_k2,
              g_da_out, g_ml_out, w_out, g_ffn, w_router, w_gate, w_up, w_down):
    weights = (g_mix, w_in, b_gates, conv_w, g_q, g_k, lam_q1, lam_k1, lam_q2, lam_k2,
               g_da_out, g_ml_out, w_out, g_ffn, w_router, w_gate, w_up, w_down)
    y_prompt = _trunk(x_prompt, meta, *weights)
    y_sample = _trunk(x_sample, meta, *weights)
    return (y_prompt, y_sample)
```

```python
import functools
import math

import jax
import jax.numpy as jnp
from jax import lax
from jax.experimental import pallas as pl
from jax.experimental.pallas import tpu as pltpu

F32 = jnp.float32
BF16 = jnp.bfloat16
I32 = jnp.int32

D_MODEL = 1024
N_META = 16
LANE = 128
FRONT = LANE
PAD_ROWS = FRONT - N_META
RMS_EPS = 1e-6
DA_HEADS = 4
DA_WIDTH = 512
DA_QK_DIM = 64
ROPE_DIM = 16
ROPE_THETA = 500000.0
ML_HEADS = 4
ML_WIDTH = 512
ML_HEAD_DIM = 128
NEG_BIG = -1e30
N_GATES = 16
D_IN = 3600
D_IN_PAD = 3712
N_EXPERTS = 16
CAP_FACTOR = 2
D_FF = 2816
LAM_INIT = 0.8 - 0.6 * math.exp(-0.3 * 0)
MLSTM_CHUNK = LANE
VMEM_LIMIT = 56 * 1024 * 1024


def _cparams(n_axes):
    return pltpu.CompilerParams(dimension_semantics=("arbitrary",) * n_axes,
                                vmem_limit_bytes=VMEM_LIMIT)


def _row_tile(lp):
    for t in (384, 256, 128):
        if lp % t == 0:
            return t
    raise ValueError(lp)


def _inproj_body(x_ref, w_ref, gmix_ref, bg_ref, cos_ref, sa_ref, sb_ref, gq_ref, gk_ref,
                 qk_ref, v_ref, mqk_ref, mv_ref, mo_ref, gt_ref):
    x = x_ref[...]
    h = x * lax.rsqrt(jnp.mean(x * x, axis=-1, keepdims=True) + RMS_EPS) * gmix_ref[...]
    z = jnp.dot(h.astype(BF16), w_ref[...], preferred_element_type=F32)
    lane = lax.broadcasted_iota(I32, (1, LANE), 1)
    lo = lane < DA_QK_DIM
    cos = cos_ref[...]
    sa = sa_ref[...]
    sb = sb_ref[...]

    def normrope(u, g, scale):
        sq = u * u
        s_lo = jnp.sum(jnp.where(lo, sq, 0.0), axis=-1, keepdims=True)
        s_hi = jnp.sum(jnp.where(lo, 0.0, sq), axis=-1, keepdims=True)
        ms = jnp.where(lo, s_lo, s_hi) * (1.0 / DA_QK_DIM)
        y = u * lax.rsqrt(ms + RMS_EPS) * g
        y = y * cos + pltpu.roll(y, 8, 1) * sa + pltpu.roll(y, LANE - 8, 1) * sb
        return y * scale

    for hh in range(DA_HEADS):
        c0 = hh * LANE
        qk_ref[:, c0:c0 + LANE] = normrope(z[:, c0:c0 + LANE], gq_ref[...],
                                           DA_QK_DIM ** -0.5).astype(BF16)
        c1 = DA_WIDTH + hh * LANE
        qk_ref[:, c1:c1 + LANE] = normrope(z[:, c1:c1 + LANE], gk_ref[...], 1.0).astype(BF16)
    v_ref[...] = z[:, 1024:1536].astype(BF16)
    mqk_ref[...] = z[:, 1536:2560]
    mv_ref[...] = z[:, 2560:3072]
    mo_ref[...] = z[:, 3072:3584]
    g = z[:, 3584:3712] + bg_ref[...]
    gt_ref[0] = g.T[:N_GATES, :]


def _inproj(xp, w_in_p, g_mix, bg, cos, sa, sb, gq, gk, nb, lp):
    tm = _row_tile(lp)
    nt = lp // tm
    t_all = nb * lp
    row = lambda i: (i, 0)
    tab = lambda i: (i % nt, 0)
    const = lambda i: (0, 0)
    return pl.pallas_call(
        _inproj_body,
        grid=(nb * nt,),
        in_specs=[
            pl.BlockSpec((tm, D_MODEL), row),
            pl.BlockSpec((D_MODEL, D_IN_PAD), const),
            pl.BlockSpec((1, D_MODEL), const),
            pl.BlockSpec((1, LANE), const),
            pl.BlockSpec((tm, LANE), tab),
            pl.BlockSpec((tm, LANE), tab),
            pl.BlockSpec((tm, LANE), tab),
            pl.BlockSpec((1, LANE), const),
            pl.BlockSpec((1, LANE), const),
        ],
        out_specs=[
            pl.BlockSpec((tm, 1024), row),
            pl.BlockSpec((tm, 512), row),
            pl.BlockSpec((tm, 1024), row),
            pl.BlockSpec((tm, 512), row),
            pl.BlockSpec((tm, 512), row),
            pl.BlockSpec((1, N_GATES, tm), lambda i: (i // nt, 0, i % nt)),
        ],
        out_shape=[
            jax.ShapeDtypeStruct((t_all, 1024), BF16),
            jax.ShapeDtypeStruct((t_all, 512), BF16),
            jax.ShapeDtypeStruct((t_all, 1024), F32),
            jax.ShapeDtypeStruct((t_all, 512), F32),
            jax.ShapeDtypeStruct((t_all, 512), F32),
            jax.ShapeDtypeStruct((nb, N_GATES, lp), F32),
        ],
        compiler_params=_cparams(1),
        name="inproj",
    )(xp, w_in_p, g_mix, bg, cos, sa, sb, gq, gk)


def _attn_body(lam_ref, q_ref, k_ref, v_ref, bias_ref, g_ref, o_ref):
    lv = lam_ref[...]
    lam = (jnp.exp(jnp.sum(lv[0:1] * lv[1:2], axis=-1, keepdims=True))
           - jnp.exp(jnp.sum(lv[2:3] * lv[3:4], axis=-1, keepdims=True)) + LAM_INIT)
    q = q_ref[...]
    k = k_ref[...]
    v = v_ref[...]
    nt_dims = (((1,), (1,)), ((), ()))

    def comp(j):
        s = lax.dot_general(q[:, j * 64:(j + 1) * 64], k[:, j * 64:(j + 1) * 64], nt_dims,
                            preferred_element_type=F32)
        s_a = s[:, :FRONT] + bias_ref[...]
        s_b = s[:, FRONT:]
        m = jnp.maximum(jnp.max(s_a, axis=-1, keepdims=True), jnp.max(s_b, axis=-1, keepdims=True))
        e_a = jnp.exp(s_a - m)
        e_b = jnp.exp(s_b - m)
        l = jnp.sum(e_a, axis=-1, keepdims=True) + jnp.sum(e_b, axis=-1, keepdims=True)
        return e_a, e_b, 1.0 / l

    ea0, eb0, r0 = comp(0)
    ea1, eb1, r1 = comp(1)
    r1 = r1 * lam
    a_a = (ea0 * r0 - ea1 * r1).astype(BF16)
    a_b = (eb0 * r0 - eb1 * r1).astype(BF16)
    o = (jnp.dot(a_a, v[:FRONT], preferred_element_type=F32)
         + jnp.dot(a_b, v[FRONT:], preferred_element_type=F32))
    o = o * lax.rsqrt(jnp.mean(o * o, axis=-1, keepdims=True) + RMS_EPS) * g_ref[...]
    o_ref[...] = (o * (1.0 - LAM_INIT)).astype(BF16)


def _attention(lamv, qk, v, bias, g_out, nb, lp):
    tq = _row_tile(lp)
    nq = lp // tq
    return pl.pallas_call(
        _attn_body,
        grid=(nb, DA_HEADS, nq),
        in_specs=[
            pl.BlockSpec((4, DA_QK_DIM), lambda b, h, i: (0, 0)),
            pl.BlockSpec((tq, LANE), lambda b, h, i: (b * nq + i, h)),
            pl.BlockSpec((lp, LANE), lambda b, h, i: (b, DA_HEADS + h)),
            pl.BlockSpec((lp, LANE), lambda b, h, i: (b, h)),
            pl.BlockSpec((1, FRONT), lambda b, h, i: (0, 0)),
            pl.BlockSpec((1, LANE), lambda b, h, i: (0, 0)),
        ],
        out_specs=pl.BlockSpec((tq, LANE), lambda b, h, i: (b * nq + i, h)),
        out_shape=jax.ShapeDtypeStruct((nb * lp, DA_WIDTH), BF16),
        compiler_params=_cparams(3),
        name="diff_attention",
    )(lamv, qk, qk, v, bias, g_out)


def _log_sigmoid(x):
    return jnp.minimum(x, 0.0) - jnp.log(1.0 + jnp.exp(-jnp.abs(x)))


def _lane_cumsum(x, forward):
    lane = lax.broadcasted_iota(I32, x.shape, 1)
    sh = 1
    while sh < LANE:
        if forward:
            x = x + jnp.where(lane >= sh, pltpu.roll(x, sh, 1), 0.0)
        else:
            x = x + jnp.where(lane < LANE - sh, pltpu.roll(x, LANE - sh, 1), 0.0)
        sh *= 2
    return x


def _mlstm_body(q_ref, k_ref, v_ref, og_ref, gt_ref, cwq_ref, cwk_ref, gml_ref, o_ref,
                qs, ks, va, hf, hb):
    head = pl.program_id(1)
    lp = q_ref.shape[0]
    nc = lp // MLSTM_CHUNK
    c = MLSTM_CHUNK
    row = lax.broadcasted_iota(I32, (lp, 1), 0)

    def conv_silu(x, w):
        x_prev = jnp.where(row == 0, 0.0, pltpu.roll(x, 1, 0))
        x_next = jnp.where(row == lp - 1, 0.0, pltpu.roll(x, lp - 1, 0))
        y = x_prev * w[0:1] + x * w[1:2] + x_next * w[2:3]
        return y * jax.nn.sigmoid(y)

    qs[...] = (conv_silu(q_ref[...], cwq_ref[...]) * (ML_HEAD_DIM ** -0.5)).astype(BF16)
    ks[...] = conv_silu(k_ref[...], cwk_ref[...]).astype(BF16)
    va[:, :LANE] = v_ref[...].astype(BF16)
    va[:, LANE:] = jnp.where(lax.broadcasted_iota(I32, (lp, LANE), 1) == 0, 1.0, 0.0).astype(BF16)

    r_i = lax.broadcasted_iota(I32, (c, c), 0)
    c_i = lax.broadcasted_iota(I32, (c, c), 1)
    sub = lax.broadcasted_iota(I32, (N_GATES, 1), 0)
    lane8 = lax.broadcasted_iota(I32, (8, c), 1)

    def step(ci, state, m_prev, forward):
        start = pl.multiple_of(ci * c, c)
        rows = pl.ds(start, c)
        qc = qs[rows, :]
        kc = ks[rows, :]
        vc = va[rows, :]
        g = gt_ref[0, :, rows]
        gi = 0 if forward else 2
        i_row = jnp.sum(jnp.where(sub == gi * ML_HEADS + head, g, 0.0), axis=0, keepdims=True)
        f_row = jnp.sum(jnp.where(sub == (gi + 1) * ML_HEADS + head, g, 0.0), axis=0, keepdims=True)
        valid = (lane8 + start) >= PAD_ROWS
        li = jnp.where(valid, jnp.broadcast_to(i_row, (8, c)), NEG_BIG)
        lf = jnp.where(valid, _log_sigmoid(jnp.broadcast_to(f_row, (8, c))), 0.0)
        cum = _lane_cumsum(lf, forward)
        b_last = cum[0:1, c - 1:c] if forward else cum[0:1, 0:1]
        cum_r = jnp.broadcast_to(cum[0:1], (c, c))
        li_r = jnp.broadcast_to(li[0:1], (c, c))
        cum_c = cum_r.T
        li_c = li_r.T
        mask = (c_i <= r_i) if forward else (c_i >= r_i)
        dm = jnp.where(mask, cum_c - cum_r + li_r, -jnp.inf)
        inter = cum_c[:, 0:1] + m_prev
        m_t = jnp.maximum(jnp.max(dm, axis=-1, keepdims=True), inter)
        w = jnp.exp(dm - m_t)
        gsc = jnp.exp(inter - m_t)
        s = lax.dot_general(qc, kc, (((1,), (1,)), ((), ())), preferred_element_type=F32) * w
        r1 = jnp.dot(s.astype(BF16), vc, preferred_element_type=F32)
        r2 = jnp.dot(qc, state.astype(BF16), preferred_element_type=F32)
        nd = r1 + gsc * r2
        num = nd[:, :LANE]
        den = nd[:, LANE:LANE + 1]
        h_out = num / jnp.maximum(jnp.abs(den), jnp.exp(-m_t))
        a_row = b_last - cum[0:1] + li[0:1]
        m_new = jnp.maximum(b_last + m_prev, jnp.max(a_row, axis=-1, keepdims=True))
        wa = jnp.exp(b_last - cum_c + li_c - m_new)
        kw = kc.astype(F32) * wa
        kv = jnp.dot(kw.T.astype(BF16), vc, preferred_element_type=F32)
        state_new = jnp.exp(b_last + m_prev - m_new) * state + kv
        return h_out, state_new, m_new

    def body(i, carry):
        sf, mf, sb, mb = carry
        h_f, sf, mf = step(i, sf, mf, True)
        hf[pl.ds(pl.multiple_of(i * c, c), c), :] = h_f
        j = nc - 1 - i
        h_b, sb, mb = step(j, sb, mb, False)
        hb[pl.ds(pl.multiple_of(j * c, c), c), :] = h_b
        return sf, mf, sb, mb

    z_state = jnp.zeros((ML_HEAD_DIM, 2 * LANE), F32)
    z_m = jnp.zeros((1, 1), F32)
    lax.fori_loop(0, nc, body, (z_state, z_m, z_state, z_m))

    hs = hf[...] + hb[...]
    y = hs * lax.rsqrt(jnp.mean(hs * hs, axis=-1, keepdims=True) + RMS_EPS) * gml_ref[...]
    o_ref[...] = (y * jax.nn.sigmoid(og_ref[...])).astype(BF16)


def _mlstm(mqk, mv, mo, gt, conv_w, g_ml, nb, lp):
    blk = lambda col: pl.BlockSpec((lp, LANE), col)
    return pl.pallas_call(
        _mlstm_body,
        grid=(nb, ML_HEADS),
        in_specs=[
            blk(lambda b, h: (b, h)),
            blk(lambda b, h: (b, ML_HEADS + h)),
            blk(lambda b, h: (b, h)),
            blk(lambda b, h: (b, h)),
            pl.BlockSpec((1, N_GATES, lp), lambda b, h: (b, 0, 0)),
            pl.BlockSpec((3, LANE), lambda b, h: (0, h)),
            pl.BlockSpec((3, LANE), lambda b, h: (0, ML_HEADS + h)),
            pl.BlockSpec((1, LANE), lambda b, h: (0, 0)),
        ],
        out_specs=blk(lambda b, h: (b, h)),
        out_shape=jax.ShapeDtypeStruct((nb * lp, ML_WIDTH), BF16),
        scratch_shapes=[
            pltpu.VMEM((lp, LANE), BF16),
            pltpu.VMEM((lp, LANE), BF16),
            pltpu.VMEM((lp, 2 * LANE), BF16),
            pltpu.VMEM((lp, LANE), F32),
            pltpu.VMEM((lp, LANE), F32),
        ],
        compiler_params=_cparams(2),
        name="bidir_mlstm",
    )(mqk, mqk, mv, mo, gt, conv_w, conv_w, g_ml)


def _outproj_body(x_ref, oa_ref, hm_ref, wa_ref, wm_ref, gffn_ref, wr_ref, valid_ref,
                  x2_ref, xn_ref, affc_ref, afft_ref):
    x2 = (x_ref[...]
          + jnp.dot(oa_ref[...], wa_ref[...], preferred_element_type=F32)
          + jnp.dot(hm_ref[...], wm_ref[...], preferred_element_type=F32))
    x2_ref[...] = x2
    xn = x2 * lax.rsqrt(jnp.mean(x2 * x2, axis=-1, keepdims=True) + RMS_EPS) * gffn_ref[...]
    xn_ref[...] = xn.astype(BF16)
    logits = jnp.dot(xn, wr_ref[...], preferred_element_type=F32,
                     precision=lax.Precision.HIGHEST)
    lane = lax.broadcasted_iota(I32, logits.shape, 1)
    real = lane < N_EXPERTS
    logits = jnp.where(real, logits, -jnp.inf)
    e = jnp.exp(logits - jnp.max(logits, axis=-1, keepdims=True))
    aff = e / jnp.sum(e, axis=-1, keepdims=True)
    aff = jnp.where(real & (valid_ref[...] > 0.0), aff, -1.0)
    affc_ref[...] = aff
    afft_ref[...] = aff.T[:N_EXPERTS, :]


def _outproj(xp, oa, hm, w_a, w_m, g_ffn, w_r, valid, nb, lp):
    tm = _row_tile(lp)
    nt = lp // tm
    t_all = nb * lp
    row = lambda i: (i, 0)
    const = lambda i: (0, 0)
    return pl.pallas_call(
        _outproj_body,
        grid=(nb * nt,),
        in_specs=[
            pl.BlockSpec((tm, D_MODEL), row),
            pl.BlockSpec((tm, DA_WIDTH), row),
            pl.BlockSpec((tm, ML_WIDTH), row),
            pl.BlockSpec((DA_WIDTH, D_MODEL), const),
            pl.BlockSpec((ML_WIDTH, D_MODEL), const),
            pl.BlockSpec((1, D_MODEL), const),
            pl.BlockSpec((D_MODEL, LANE), const),
            pl.BlockSpec((tm, 1), lambda i: (i % nt, 0)),
        ],
        out_specs=[
            pl.BlockSpec((tm, D_MODEL), row),
            pl.BlockSpec((tm, D_MODEL), row),
            pl.BlockSpec((tm, LANE), row),
            pl.BlockSpec((N_EXPERTS, tm), lambda i: (0, i)),
        ],
        out_shape=[
            jax.ShapeDtypeStruct((t_all, D_MODEL), F32),
            jax.ShapeDtypeStruct((t_all, D_MODEL), BF16),
            jax.ShapeDtypeStruct((t_all, LANE), F32),
            jax.ShapeDtypeStruct((N_EXPERTS, t_all), F32),
        ],
        compiler_params=_cparams(1),
        name="outproj_router",
    )(xp, oa, hm, w_a, w_m, g_ffn, w_r, valid)


def _rope_tables(lp):
    pos = jnp.arange(lp, dtype=F32) - float(PAD_ROWS)
    inv = ROPE_THETA ** (-jnp.arange(0, ROPE_DIM, 2, dtype=F32) / ROPE_DIM)
    ang = pos[:, None] * inv[None, :]
    cos8, sin8 = jnp.cos(ang), jnp.sin(ang)
    half = ROPE_DIM // 2
    one_block = jnp.ones((lp, DA_QK_DIM - ROPE_DIM), F32)
    zero8 = jnp.zeros((lp, half), F32)
    zero_block = jnp.zeros((lp, DA_QK_DIM - ROPE_DIM), F32)
    cos_c = jnp.concatenate([cos8, cos8, one_block], axis=1)
    sa_c = jnp.concatenate([zero8, sin8, zero_block], axis=1)
    sb_c = jnp.concatenate([-sin8, zero8, zero_block], axis=1)
    tile2 = lambda a: jnp.concatenate([a, a], axis=1)
    return tile2(cos_c), tile2(sa_c), tile2(sb_c)


def _pad_tokens(x, meta):
    b = x.shape[0]
    m = jnp.broadcast_to(meta.astype(x.dtype)[None], (b, N_META, D_MODEL))
    z = jnp.zeros((b, PAD_ROWS, D_MODEL), x.dtype)
    return jnp.concatenate([z, m, x], axis=1)


def _mixer(xp, nb, lp, g_mix, w_in, b_gates, conv_w, g_q, g_k, lamv, g_da_out, g_ml_out, w_out,
           g_ffn, w_router):
    w_in_p = jnp.pad(w_in.astype(BF16), ((0, 0), (0, D_IN_PAD - D_IN)))
    bg = jnp.pad(b_gates, (0, LANE - N_GATES)).reshape(1, LANE)
    cos, sa, sb = _rope_tables(lp)
    gq = jnp.tile(g_q, 2).reshape(1, LANE)
    gk = jnp.tile(g_k, 2).reshape(1, LANE)
    qk, v, mqk, mv, mo, gt = _inproj(xp, w_in_p, g_mix.reshape(1, D_MODEL), bg, cos, sa, sb,
                                     gq, gk, nb, lp)
    bias = jnp.where(jnp.arange(FRONT) < PAD_ROWS, NEG_BIG, 0.0).astype(F32).reshape(1, FRONT)
    oa = _attention(lamv, qk, v, bias, g_da_out.reshape(1, LANE), nb, lp)
    hm = _mlstm(mqk, mv, mo, gt, conv_w, g_ml_out.reshape(1, LANE), nb, lp)
    w_o = w_out.astype(BF16)
    w_r = jnp.pad(w_router, ((0, 0), (0, LANE - N_EXPERTS)))
    valid = (jnp.arange(lp) >= PAD_ROWS).astype(F32).reshape(lp, 1)
    return _outproj(xp, oa, hm, w_o[:DA_WIDTH], w_o[DA_WIDTH:], g_ffn.reshape(1, D_MODEL), w_r,
                    valid, nb, lp)


TOK_TILE = LANE
WIN = 32
SUB = 8
FFN_ROWS = 512
FF_CHUNK = 256


def _route_body(aff_ref, thr_ref, cnt_ref, eqa_ref, seg_ref, *, cap, nt):
    a = aff_ref[...]
    bits = pltpu.bitcast(a, I32)
    capf = float(cap)

    def search(i, v):
        trial = v | lax.shift_left(jnp.int32(1), 30 - i)
        c = jnp.sum((bits >= trial).astype(F32), axis=-1, keepdims=True)
        return jnp.where(c >= capf, trial, v)

    thr_bits = lax.fori_loop(0, 31, search, jnp.zeros((N_EXPERTS, 1), I32))
    gt = bits > thr_bits
    eq = bits == thr_bits
    need = capf - jnp.sum(gt.astype(F32), axis=-1, keepdims=True)
    thr_ref[...] = jnp.broadcast_to(pltpu.bitcast(thr_bits, F32), (N_EXPERTS, LANE))

    slab = LANE * TOK_TILE
    ind = (lax.broadcasted_iota(I32, (slab, LANE), 0) // TOK_TILE
           == lax.broadcasted_iota(I32, (slab, LANE), 1)).astype(BF16)
    upper = (lax.broadcasted_iota(I32, (LANE, LANE), 0)
             < lax.broadcasted_iota(I32, (LANE, LANE), 1)).astype(BF16)
    eq_base = jnp.zeros((N_EXPERTS, 1), F32)
    seg_base = jnp.zeros((N_EXPERTS, 1), F32)
    for s0 in range(0, nt, LANE):
        n_t = min(LANE, nt - s0)
        cols = slice(s0 * TOK_TILE, (s0 + n_t) * TOK_TILE)
        gt_cnt = jnp.dot(gt[:, cols].astype(BF16), ind[:n_t * TOK_TILE], preferred_element_type=F32)
        eq_cnt = jnp.dot(eq[:, cols].astype(BF16), ind[:n_t * TOK_TILE], preferred_element_type=F32)
        eq_before = eq_base + jnp.dot(eq_cnt.astype(BF16), upper, preferred_element_type=F32)
        eq_allow = jnp.clip(need - eq_before, 0.0, eq_cnt)
        cnt = gt_cnt + eq_allow
        seg = seg_base + jnp.dot(cnt.astype(BF16), upper, preferred_element_type=F32)
        eq_base = eq_base + jnp.sum(eq_cnt, axis=-1, keepdims=True)
        seg_base = seg_base + jnp.sum(cnt, axis=-1, keepdims=True)
        cnt_ref[:, s0:s0 + LANE] = cnt.astype(I32)
        eqa_ref[:, s0:s0 + LANE] = eq_allow.astype(I32)
        seg_ref[:, s0:s0 + LANE] = seg.astype(I32)


def _route(aff_t, cap):
    tg = aff_t.shape[1]
    nt = tg // TOK_TILE
    ntp = -(-nt // LANE) * LANE
    tab = jax.ShapeDtypeStruct((N_EXPERTS, ntp), I32)
    return pl.pallas_call(
        functools.partial(_route_body, cap=cap, nt=nt),
        out_shape=[jax.ShapeDtypeStruct((N_EXPERTS, LANE), F32), tab, tab, tab],
        compiler_params=pltpu.CompilerParams(vmem_limit_bytes=VMEM_LIMIT),
        name="route",
    )(aff_t)


def _window_rows(cnt, p):
    return jnp.clip(cnt - p * WIN, 0, WIN)


def _dispatch_body(seg_sm, cnt_sm, xn_ref, aff_ref, thr_ref, eqa_ref, *rest,
                   ntp, base, stride, tail_start):
    xg_ref, pbuf, stage, zbuf, sem = rest[-5:]
    j = pl.program_id(0)

    if tail_start is not None:
        @pl.when(j == 0)
        def _zero_tail():
            zbuf[...] = jnp.zeros_like(zbuf)
            copies = [pltpu.make_async_copy(
                zbuf, xg_ref.at[pl.ds((e * stride + tail_start) * SUB, zbuf.shape[0])], sem.at[e])
                for e in range(N_EXPERTS)]
            for c in copies:
                c.start()
            for c in copies:
                c.wait()

    a = aff_ref[...]
    thr = thr_ref[...]
    lane = lax.broadcasted_iota(I32, (N_EXPERTS, LANE), 1)
    eqa = jnp.sum(jnp.where(lane == j % LANE, eqa_ref[...], 0).astype(F32), axis=-1, keepdims=True)
    upper = (lax.broadcasted_iota(I32, (LANE, LANE), 0)
             < lax.broadcasted_iota(I32, (LANE, LANE), 1)).astype(BF16)
    gt = a > thr
    eq = a == thr
    eq_rank = jnp.dot(eq.astype(BF16), upper, preferred_element_type=F32)
    sel = gt | (eq & (eq_rank < eqa))
    rank = jnp.dot(sel.astype(BF16), upper, preferred_element_type=F32)
    rank = jnp.where(sel, rank, -1.0)

    cnts = [cnt_sm[e * ntp + j] for e in range(N_EXPERTS)]
    segs = [seg_sm[e * ntp + j] for e in range(N_EXPERTS)]
    n_pass = (functools.reduce(jnp.maximum, cnts) + WIN - 1) // WIN
    slot = lax.broadcasted_iota(I32, (WIN, LANE), 0).astype(F32)

    def one_pass(p, carry):
        off = (p * WIN).astype(F32)
        for e in range(N_EXPERTS):
            r_e = jnp.broadcast_to(rank[e:e + 1, :], (WIN, LANE))
            pbuf[e * WIN:(e + 1) * WIN, :] = (r_e == slot + off).astype(BF16)
        g = jnp.dot(pbuf[...], xn_ref[...], preferred_element_type=F32)
        for s in range(SUB):
            stage[pl.ds(s, N_EXPERTS * WIN, stride=SUB), :] = g[:, s * LANE:(s + 1) * LANE]

        def copy(e):
            n = _window_rows(cnts[e], p) * SUB
            dst = pl.multiple_of((e * stride + base + segs[e] + p * WIN) * SUB, SUB)
            return pltpu.make_async_copy(stage.at[pl.ds(e * WIN * SUB, n)],
                                         xg_ref.at[pl.ds(dst, n)], sem.at[e])

        for e in range(N_EXPERTS):
            @pl.when(cnts[e] > p * WIN)
            def _start():
                copy(e).start()
        for e in range(N_EXPERTS):
            @pl.when(cnts[e] > p * WIN)
            def _wait():
                copy(e).wait()
        return carry

    lax.fori_loop(0, n_pass, one_pass, 0)


def _dispatch(seg, cnt, xn, aff_t, thr, eqa, xg, *, tile0, nt, ntp, base, stride, tail_start):
    first = xg is None
    zero_tail = first and stride > tail_start
    tail_rows = (stride - tail_start) * SUB if zero_tail else SUB
    kern = functools.partial(_dispatch_body, ntp=ntp, base=base, stride=stride,
                             tail_start=tail_start if zero_tail else None)
    in_specs = [
        pl.BlockSpec((TOK_TILE, D_MODEL), lambda j, s, c: (tile0 + j, 0)),
        pl.BlockSpec((N_EXPERTS, TOK_TILE), lambda j, s, c: (0, tile0 + j)),
        pl.BlockSpec((N_EXPERTS, LANE), lambda j, s, c: (0, 0)),
        pl.BlockSpec((N_EXPERTS, LANE), lambda j, s, c: (0, j // LANE)),
    ]
    args = [seg, cnt, xn, aff_t, thr, eqa]
    aliases = {}
    if not first:
        in_specs.append(pl.BlockSpec(memory_space=pl.ANY))
        args.append(xg)
        aliases = {len(args) - 1: 0}
    return pl.pallas_call(
        kern,
        grid_spec=pltpu.PrefetchScalarGridSpec(
            num_scalar_prefetch=2,
            grid=(nt,),
            in_specs=in_specs,
            out_specs=pl.BlockSpec(memory_space=pl.ANY),
            scratch_shapes=[
                pltpu.VMEM((N_EXPERTS * WIN, TOK_TILE), BF16),
                pltpu.VMEM((N_EXPERTS * WIN * SUB, LANE), F32),
                pltpu.VMEM((tail_rows, LANE), F32),
                pltpu.SemaphoreType.DMA((N_EXPERTS,)),
            ]),
        out_shape=jax.ShapeDtypeStruct((N_EXPERTS * stride * SUB, LANE), F32),
        input_output_aliases=aliases,
        compiler_params=pltpu.CompilerParams(dimension_semantics=("arbitrary",),
                                             vmem_limit_bytes=VMEM_LIMIT, has_side_effects=True),
        name="dispatch",
    )(*args)


def _ffn_body(xg_ref, wg_ref, wu_ref, wd_ref, wr_ref, o_ref, xb):
    e = pl.program_id(0)
    tm = xb.shape[0]
    for s in range(SUB):
        xb[:, s * LANE:(s + 1) * LANE] = xg_ref[pl.ds(s, tm, stride=SUB), :].astype(BF16)
    x = xb[...]
    logits = jnp.dot(x, wr_ref[...], preferred_element_type=F32)
    lane = lax.broadcasted_iota(I32, logits.shape, 1)
    logits = jnp.where(lane < N_EXPERTS, logits, -jnp.inf)
    ex = jnp.exp(logits - jnp.max(logits, axis=-1, keepdims=True))
    aff = ex / jnp.sum(ex, axis=-1, keepdims=True)
    gate = jnp.sum(jnp.where(lane == e, aff, 0.0), axis=-1, keepdims=True)
    acc = jnp.zeros((tm, D_MODEL), F32)
    for f in range(0, D_FF, FF_CHUNK):
        g = jnp.dot(x, wg_ref[0, :, f:f + FF_CHUNK], preferred_element_type=F32)
        u = jnp.dot(x, wu_ref[0, :, f:f + FF_CHUNK], preferred_element_type=F32)
        h = (g * jax.nn.sigmoid(g) * u).astype(BF16)
        acc = acc + jnp.dot(h, wd_ref[0, f:f + FF_CHUNK, :], preferred_element_type=F32)
    y = (acc * gate).astype(BF16).astype(F32)
    for s in range(SUB):
        o_ref[pl.ds(s, tm, stride=SUB), :] = y[:, s * LANE:(s + 1) * LANE]


def _ffn(xg, wg, wu, wd, w_r, stride):
    nr = stride // FFN_ROWS
    blk = pl.BlockSpec((FFN_ROWS * SUB, LANE), lambda e, i: (e * nr + i, 0))
    return pl.pallas_call(
        _ffn_body,
        grid=(N_EXPERTS, nr),
        in_specs=[
            blk,
            pl.BlockSpec((1, D_MODEL, D_FF), lambda e, i: (e, 0, 0)),
            pl.BlockSpec((1, D_MODEL, D_FF), lambda e, i: (e, 0, 0)),
            pl.BlockSpec((1, D_FF, D_MODEL), lambda e, i: (e, 0, 0)),
            pl.BlockSpec((D_MODEL, LANE), lambda e, i: (0, 0)),
        ],
        out_specs=blk,
        out_shape=jax.ShapeDtypeStruct(xg.shape, F32),
        scratch_shapes=[pltpu.VMEM((FFN_ROWS, D_MODEL), BF16)],
        compiler_params=_cparams(2),
        name="expert_ffn",
    )(xg, wg, wu, wd, w_r)


def _combine_body(seg_sm, cnt_sm, x2_ref, aff_ref, thr_ref, eqa_ref, ye_ref, o_ref,
                  ybuf, ymat, sem, *, ntp, ntl, base, stride):
    j = pl.program_id(0)

    @pl.when(j == 0)
    def _init():
        ybuf[...] = jnp.zeros_like(ybuf)

    @pl.when(j % ntl > 0)
    def _tile():
        a = aff_ref[...]
        thr = thr_ref[...]
        eqa = eqa_ref[0]
        lower = (lax.broadcasted_iota(I32, (TOK_TILE, TOK_TILE), 1)
                 < lax.broadcasted_iota(I32, (TOK_TILE, TOK_TILE), 0)).astype(BF16)
        gt = a > thr
        eq = a == thr
        eq_rank = jnp.dot(lower, eq.astype(BF16), preferred_element_type=F32)
        sel = gt | (eq & (eq_rank < eqa))
        rank = jnp.dot(lower, sel.astype(BF16), preferred_element_type=F32)
        rank = jnp.where(sel, rank, -1.0).astype(BF16)
        ncol = N_EXPERTS * WIN
        rep = (lax.broadcasted_iota(I32, (LANE, ncol), 1) // WIN
               == lax.broadcasted_iota(I32, (LANE, ncol), 0)).astype(BF16)
        rank_rep = jnp.dot(rank, rep, preferred_element_type=F32)
        col_slot = (lax.broadcasted_iota(I32, (1, ncol), 1) % WIN).astype(F32)

        cnts = [cnt_sm[e * ntp + j] for e in range(N_EXPERTS)]
        segs = [seg_sm[e * ntp + j] for e in range(N_EXPERTS)]
        n_pass = (functools.reduce(jnp.maximum, cnts) + WIN - 1) // WIN

        def one_pass(p, acc):
            def copy(e):
                n = _window_rows(cnts[e], p) * SUB
                src = pl.multiple_of((e * stride + base + segs[e] + p * WIN) * SUB, SUB)
                return pltpu.make_async_copy(ye_ref.at[pl.ds(src, n)],
                                             ybuf.at[pl.ds(e * WIN * SUB, n)], sem.at[e])

            for e in range(N_EXPERTS):
                @pl.when(cnts[e] > p * WIN)
                def _start():
                    copy(e).start()
            onehot = (rank_rep == col_slot + (p * WIN).astype(F32)).astype(BF16)
            for e in range(N_EXPERTS):
                @pl.when(cnts[e] > p * WIN)
                def _wait():
                    copy(e).wait()
            for s in range(SUB):
                ymat[:, s * LANE:(s + 1) * LANE] = ybuf[pl.ds(s, ncol, stride=SUB), :].astype(BF16)
            return acc + jnp.dot(onehot, ymat[...], preferred_element_type=F32)

        o_ref[...] = lax.fori_loop(0, n_pass, one_pass, x2_ref[...])


def _combine(seg, cnt, x2, aff_c, thr_row, eqa_rows, ye, *, tile0, nt, ntp, ntl, base, stride):
    ncol = N_EXPERTS * WIN
    n_out = nt // ntl * (ntl - 1)
    out_map = lambda j, s, c: (j // ntl * (ntl - 1) + jnp.maximum(j % ntl - 1, 0), 0)
    return pl.pallas_call(
        functools.partial(_combine_body, ntp=ntp, ntl=ntl, base=base, stride=stride),
        grid_spec=pltpu.PrefetchScalarGridSpec(
            num_scalar_prefetch=2,
            grid=(nt,),
            in_specs=[
                pl.BlockSpec((TOK_TILE, D_MODEL), lambda j, s, c: (tile0 + j, 0)),
                pl.BlockSpec((TOK_TILE, LANE), lambda j, s, c: (tile0 + j, 0)),
                pl.BlockSpec((1, LANE), lambda j, s, c: (0, 0)),
                pl.BlockSpec((1, 1, LANE), lambda j, s, c: (j, 0, 0)),
                pl.BlockSpec(memory_space=pl.ANY),
            ],
            out_specs=pl.BlockSpec((TOK_TILE, D_MODEL), out_map),
            scratch_shapes=[
                pltpu.VMEM((ncol * SUB, LANE), F32),
                pltpu.VMEM((ncol, D_MODEL), BF16),
                pltpu.SemaphoreType.DMA((N_EXPERTS,)),
            ]),
        out_shape=jax.ShapeDtypeStruct((n_out * TOK_TILE, D_MODEL), F32),
        compiler_params=pltpu.CompilerParams(dimension_semantics=("arbitrary",),
                                             vmem_limit_bytes=VMEM_LIMIT),
        name="combine",
    )(seg, cnt, x2, aff_c, thr_row, eqa_rows, ye)


def _round_up(a, b):
    return -(-a // b) * b


def kernel(x_prompt, x_sample, meta, g_mix, w_in, b_gates, conv_w, g_q, g_k, lam_q1, lam_k1, lam_q2,
           lam_k2, g_da_out, g_ml_out, w_out, g_ffn, w_router, w_gate, w_up, w_down):
    assert x_prompt.shape[1:] == x_sample.shape[1:]
    groups = (x_prompt.shape[0], x_sample.shape[0])
    x = jnp.concatenate([x_prompt, x_sample], axis=0)
    nb, s, _ = x.shape
    lp = FRONT + s
    assert lp % LANE == 0
    ntl = lp // TOK_TILE
    xp = _pad_tokens(x, meta).reshape(nb * lp, D_MODEL)
    lamv = jnp.stack([lam_q1[0], lam_k1[0], lam_q2[0], lam_k2[0]])
    x2, xn, aff_c, aff_t = _mixer(xp, nb, lp, g_mix[0], w_in[0], b_gates[0], conv_w[0], g_q[0],
                                  g_k[0], lamv, g_da_out[0], g_ml_out[0], w_out[0], g_ffn[0],
                                  w_router[0])

    caps = [max(1, CAP_FACTOR * b * (N_META + s) // N_EXPERTS) for b in groups]
    bases = [sum(caps[:i]) for i in range(len(caps))]
    tail_start = sum(caps)
    stride = _round_up(tail_start, FFN_ROWS)
    infos = []
    tile0 = 0
    for b, cap in zip(groups, caps):
        nt = b * ntl
        thr, cnt, eqa, seg = _route(aff_t[:, tile0 * TOK_TILE:(tile0 + nt) * TOK_TILE], cap)
        ntp = cnt.shape[1]
        thr_row = jnp.pad(thr[:, 0], (0, LANE - N_EXPERTS), constant_values=2.0).reshape(1, LANE)
        eqa_rows = jnp.pad(eqa.T.astype(F32), ((0, 0), (0, LANE - N_EXPERTS))).reshape(ntp, 1, LANE)
        infos.append(dict(tile0=tile0, nt=nt, ntp=ntp, cap=cap, thr=thr, thr_row=thr_row, eqa=eqa,
                          eqa_rows=eqa_rows, seg=seg.reshape(-1), cnt=cnt.reshape(-1)))
        tile0 += nt

    xg = None
    for gi, info in enumerate(infos):
        xg = _dispatch(info["seg"], info["cnt"], xn, aff_t, info["thr"], info["eqa"], xg,
                       tile0=info["tile0"], nt=info["nt"], ntp=info["ntp"], base=bases[gi],
                       stride=stride, tail_start=tail_start)
    w_r = jnp.pad(w_router[0].astype(BF16), ((0, 0), (0, LANE - N_EXPERTS)))
    ye = _ffn(xg, w_gate[0].astype(BF16), w_up[0].astype(BF16), w_down[0].astype(BF16), w_r, stride)

    outs = []
    for gi, (b, info) in enumerate(zip(groups, infos)):
        y = _combine(info["seg"], info["cnt"], x2, aff_c, info["thr_row"], info["eqa_rows"], ye,
                     tile0=info["tile0"], nt=info["nt"], ntp=info["ntp"], ntl=ntl, base=bases[gi],
                     stride=stride)
        outs.append(y.reshape(b, s, D_MODEL))
    return tuple(outs)
```

```python
import functools
import math

import jax
import jax.numpy as jnp
from jax import lax
from jax.experimental import pallas as pl
from jax.experimental.pallas import tpu as pltpu

F32 = jnp.float32
BF16 = jnp.bfloat16
I32 = jnp.int32

D_MODEL = 1024
N_META = 16
LANE = 128
SUBLANE = 8
FRONT = LANE
PAD_ROWS = FRONT - N_META
RMS_EPS = 1e-6
DA_HEADS = 4
DA_WIDTH = 512
DA_QK_DIM = 64
ROPE_DIM = 16
ROPE_THETA = 500000.0
ML_HEADS = 4
ML_WIDTH = 512
ML_HEAD_DIM = 128
NEG_BIG = -1e30
N_GATES = 16
D_IN = 3600
D_IN_PAD = 3712
N_EXPERTS = 16
CAP_FACTOR = 2
D_FF = 2816
LAM_INIT = 0.8 - 0.6 * math.exp(-0.3 * 0)
LOG2E = math.log2(math.e)
MLSTM_CHUNK = LANE
MLSTM_UNROLL = 3
VMEM_LIMIT = 56 * 1024 * 1024


def _cparams(n_axes):
    return pltpu.CompilerParams(dimension_semantics=("arbitrary",) * n_axes,
                                vmem_limit_bytes=VMEM_LIMIT)


def _row_tile(lp):
    for t in (384, 256, 128):
        if lp % t == 0:
            return t
    raise ValueError(lp)


def _inproj_body(x_ref, w_ref, gmix_ref, bg_ref, cos_ref, sa_ref, sb_ref, gq_ref, gk_ref,
                 qk_ref, v_ref, mqk_ref, mv_ref, mo_ref, gt_ref):
    x = x_ref[...]
    h = x * lax.rsqrt(jnp.mean(x * x, axis=-1, keepdims=True) + RMS_EPS) * gmix_ref[...]
    z = jnp.dot(h.astype(BF16), w_ref[...], preferred_element_type=F32)
    lane = lax.broadcasted_iota(I32, (1, LANE), 1)
    lo = lane < DA_QK_DIM
    cos = cos_ref[...]
    sa = sa_ref[...]
    sb = sb_ref[...]

    def normrope(u, g, scale):
        sq = u * u
        s_lo = jnp.sum(jnp.where(lo, sq, 0.0), axis=-1, keepdims=True)
        s_hi = jnp.sum(jnp.where(lo, 0.0, sq), axis=-1, keepdims=True)
        ms = jnp.where(lo, s_lo, s_hi) * (1.0 / DA_QK_DIM)
        y = u * lax.rsqrt(ms + RMS_EPS) * g
        y = y * cos + pltpu.roll(y, 8, 1) * sa + pltpu.roll(y, LANE - 8, 1) * sb
        return y * scale

    for hh in range(DA_HEADS):
        c0 = hh * LANE
        qk_ref[:, c0:c0 + LANE] = normrope(z[:, c0:c0 + LANE], gq_ref[...],
                                           DA_QK_DIM ** -0.5 * LOG2E).astype(BF16)
        c1 = DA_WIDTH + hh * LANE
        qk_ref[:, c1:c1 + LANE] = normrope(z[:, c1:c1 + LANE], gk_ref[...], 1.0).astype(BF16)
    v_ref[...] = z[:, 1024:1536].astype(BF16)
    mqk_ref[...] = z[:, 1536:2560]
    mv_ref[...] = z[:, 2560:3072]
    mo_ref[...] = z[:, 3072:3584]
    g = z[:, 3584:3712] + bg_ref[...]
    gt_ref[0] = g.T[:N_GATES, :]


def _inproj(xp, w_in_p, g_mix, bg, cos, sa, sb, gq, gk, nb, lp):
    tm = _row_tile(lp)
    nt = lp // tm
    t_all = nb * lp
    row = lambda i: (i, 0)
    tab = lambda i: (i % nt, 0)
    const = lambda i: (0, 0)
    return pl.pallas_call(
        _inproj_body,
        grid=(nb * nt,),
        in_specs=[
            pl.BlockSpec((tm, D_MODEL), row),
            pl.BlockSpec((D_MODEL, D_IN_PAD), const),
            pl.BlockSpec((1, D_MODEL), const),
            pl.BlockSpec((1, LANE), const),
            pl.BlockSpec((tm, LANE), tab),
            pl.BlockSpec((tm, LANE), tab),
            pl.BlockSpec((tm, LANE), tab),
            pl.BlockSpec((1, LANE), const),
            pl.BlockSpec((1, LANE), const),
        ],
        out_specs=[
            pl.BlockSpec((tm, 1024), row),
            pl.BlockSpec((tm, 512), row),
            pl.BlockSpec((tm, 1024), row),
            pl.BlockSpec((tm, 512), row),
            pl.BlockSpec((tm, 512), row),
            pl.BlockSpec((1, N_GATES, tm), lambda i: (i // nt, 0, i % nt)),
        ],
        out_shape=[
            jax.ShapeDtypeStruct((t_all, 1024), BF16),
            jax.ShapeDtypeStruct((t_all, 512), BF16),
            jax.ShapeDtypeStruct((t_all, 1024), F32),
            jax.ShapeDtypeStruct((t_all, 512), F32),
            jax.ShapeDtypeStruct((t_all, 512), F32),
            jax.ShapeDtypeStruct((nb, N_GATES, lp), F32),
        ],
        compiler_params=_cparams(1),
        name="inproj",
    )(xp, w_in_p, g_mix, bg, cos, sa, sb, gq, gk)


def _attn_q_rows(lp):
    for t in (192, 128):
        if lp % t == 0:
            return t
    raise ValueError(lp)


def _attn_body(lam_ref, q_ref, k_ref, v_ref, bias_ref, g_ref, o_ref,
               va, s0, s1, m0, m1, e0, e1):
    lp = k_ref.shape[0]
    tq = s0.shape[1]
    nblk = lp // tq
    va[:, :LANE] = v_ref[...]
    va[:, LANE:] = jnp.where(lax.broadcasted_iota(I32, (lp, LANE), 1) == 0, 1.0, 0.0).astype(BF16)
    e1[...] = jnp.zeros_like(e1)

    lv = lam_ref[...]
    lam = (jnp.exp(jnp.sum(lv[0:1] * lv[1:2], axis=-1, keepdims=True))
           - jnp.exp(jnp.sum(lv[2:3] * lv[3:4], axis=-1, keepdims=True)) + LAM_INIT)
    first = lax.broadcasted_iota(I32, (tq, LANE), 1) < DA_QK_DIM
    nt_dims = (((1,), (1,)), ((), ()))

    def rows_of(blk):
        return pl.ds(pl.multiple_of(blk * tq, tq), tq)

    def stage_a(blk, s_ref, m_ref):
        q = q_ref[rows_of(blk), :]
        zero = jnp.zeros_like(q)
        for j, qj in enumerate((jnp.where(first, q, zero), jnp.where(first, zero, q))):
            s = lax.dot_general(qj, k_ref[...], nt_dims, preferred_element_type=F32)
            s_pad = s[:, :FRONT] + bias_ref[...]
            s_ref[j, :, :FRONT] = s_pad
            s_ref[j, :, FRONT:] = s[:, FRONT:]
            m_ref[j] = jnp.maximum(jnp.max(s_pad, axis=-1, keepdims=True),
                                   jnp.max(s[:, FRONT:], axis=-1, keepdims=True))

    def stage_b(s_ref, m_ref, e_ref):
        for j in range(2):
            e_ref[j] = jnp.exp2(s_ref[j] - m_ref[j]).astype(BF16)

    def stage_c(blk, e_ref):
        r0 = jnp.dot(e_ref[0], va[...], preferred_element_type=F32)
        r1 = jnp.dot(e_ref[1], va[...], preferred_element_type=F32)
        o = r0[:, :LANE] / r0[:, LANE:LANE + 1] - lam * (r1[:, :LANE] / r1[:, LANE:LANE + 1])
        o = o * lax.rsqrt(jnp.mean(o * o, axis=-1, keepdims=True) + RMS_EPS) * g_ref[...]
        o_ref[rows_of(blk), :] = (o * (1.0 - LAM_INIT)).astype(BF16)

    stage_a(0, s0, m0)

    def step(t, carry):
        nxt = jnp.minimum(t + 1, nblk - 1)
        prv = jnp.maximum(t - 1, 0)

        @pl.when(t % 2 == 0)
        def _even():
            stage_a(nxt, s1, m1)
            stage_b(s0, m0, e0)
            stage_c(prv, e1)

        @pl.when(t % 2 == 1)
        def _odd():
            stage_a(nxt, s0, m0)
            stage_b(s1, m1, e1)
            stage_c(prv, e0)

        return carry

    lax.fori_loop(0, nblk, step, 0)
    stage_c(nblk - 1, e0 if nblk % 2 == 1 else e1)


def _attention(lamv, qk, v, bias, g_out, nb, lp):
    tq = _attn_q_rows(lp)
    blk = lambda col: pl.BlockSpec((lp, LANE), col)
    return pl.pallas_call(
        _attn_body,
        grid=(nb, DA_HEADS),
        in_specs=[
            pl.BlockSpec((4, DA_QK_DIM), lambda b, h: (0, 0)),
            blk(lambda b, h: (b, h)),
            blk(lambda b, h: (b, DA_HEADS + h)),
            blk(lambda b, h: (b, h)),
            pl.BlockSpec((1, FRONT), lambda b, h: (0, 0)),
            pl.BlockSpec((1, LANE), lambda b, h: (0, 0)),
        ],
        out_specs=blk(lambda b, h: (b, h)),
        out_shape=jax.ShapeDtypeStruct((nb * lp, DA_WIDTH), BF16),
        scratch_shapes=[
            pltpu.VMEM((lp, 2 * LANE), BF16),
            pltpu.VMEM((2, tq, lp), F32), pltpu.VMEM((2, tq, lp), F32),
            pltpu.VMEM((2, tq, 1), F32), pltpu.VMEM((2, tq, 1), F32),
            pltpu.VMEM((2, tq, lp), BF16), pltpu.VMEM((2, tq, lp), BF16),
        ],
        compiler_params=_cparams(2),
        name="diff_attention",
    )(lamv, qk, qk, v, bias, g_out)


def _log_sigmoid(x):
    return jnp.minimum(x, 0.0) - jnp.log(1.0 + jnp.exp(-jnp.abs(x)))


def _lane_scan(x, forward, op, identity):
    lane = lax.broadcasted_iota(I32, x.shape, 1)
    sh = 1
    while sh < LANE:
        if forward:
            x = op(x, jnp.where(lane >= sh, pltpu.roll(x, sh, 1), identity))
        else:
            x = op(x, jnp.where(lane < LANE - sh, pltpu.roll(x, LANE - sh, 1), identity))
        sh *= 2
    return x


def _mlstm_body(q_ref, k_ref, v_ref, og_ref, gif_ref, gff_ref, gib_ref, gfb_ref,
                cwq_ref, cwk_ref, gml_ref, o_ref,
                qs, ks, va, hf, hb, rowv0, rowv1, uvec0, uvec1, scal0, scal1,
                colv0, colv1, r10, r11, kvs0, kvs1):
    gates = (gif_ref, gff_ref, gib_ref, gfb_ref)
    rowv, uvec, scal = (rowv0, rowv1), (uvec0, uvec1), (scal0, scal1)
    colv, r1, kvs = (colv0, colv1), (r10, r11), (kvs0, kvs1)
    lp = q_ref.shape[0]
    nc = lp // MLSTM_CHUNK
    c = MLSTM_CHUNK
    row = lax.broadcasted_iota(I32, (lp, 1), 0)

    def conv_silu(x, w):
        x_prev = jnp.where(row == 0, 0.0, pltpu.roll(x, 1, 0))
        x_next = jnp.where(row == lp - 1, 0.0, pltpu.roll(x, lp - 1, 0))
        y = x_prev * w[0:1] + x * w[1:2] + x_next * w[2:3]
        return y * jax.nn.sigmoid(y)

    qs[...] = (conv_silu(q_ref[...], cwq_ref[...]) * (ML_HEAD_DIM ** -0.5)).astype(BF16)
    ks[...] = conv_silu(k_ref[...], cwk_ref[...]).astype(BF16)
    va[:, :LANE] = v_ref[...].astype(BF16)
    va[:, LANE:] = jnp.where(lax.broadcasted_iota(I32, (lp, LANE), 1) == 0, 1.0, 0.0).astype(BF16)

    r_i = lax.broadcasted_iota(I32, (c, c), 0)
    c_i = lax.broadcasted_iota(I32, (c, c), 1)
    ncp = gates[0].shape[2]
    lane_c = lax.broadcasted_iota(I32, (ncp, c), 1)
    pos = lax.broadcasted_iota(I32, (ncp, c), 0) * c + lane_c
    valid = pos >= PAD_ROWS

    for d, forward in enumerate((True, False)):
        li = jnp.where(valid, gates[2 * d][0, 0], NEG_BIG)
        lf = jnp.where(valid, _log_sigmoid(gates[2 * d + 1][0, 0]), 0.0)
        cum = _lane_scan(lf, forward, jnp.add, 0.0)
        u = li - cum
        cm = _lane_scan(u, forward, jnp.maximum, -jnp.inf)
        b_last = cum[:, c - 1:c] if forward else cum[:, 0:1]
        a = b_last + u
        a_max = jnp.max(a, axis=-1, keepdims=True)
        rowv[d][...] = jnp.zeros_like(rowv[d])
        for r, vec in enumerate((cm, jnp.exp(a - a_max), cum + cm, cum)):
            rowv[d][pl.ds(r, ncp, stride=SUBLANE), :] = vec
        uvec[d][...] = u
        scal[d][...] = jnp.where(lane_c == 0, b_last, a_max)

    zpad = jnp.zeros((c - SUBLANE, c), F32)

    def intra(ci, carry):
        start = pl.multiple_of(ci * c, c)
        rows = pl.ds(start, c)
        qc = qs[rows, :]
        kc = ks[rows, :]
        vc = va[rows, :]
        for d, forward in enumerate((True, False)):
            tile = rowv[d][pl.ds(pl.multiple_of(ci * SUBLANE, SUBLANE), SUBLANE), :]
            cols = jnp.concatenate([tile, zpad], axis=0).T
            colv[d][rows, :] = cols
            mask = (c_i <= r_i) if forward else (c_i >= r_i)
            w = jnp.exp(jnp.where(mask, uvec[d][pl.ds(ci, 1), :] - cols[:, 0:1], -jnp.inf))
            s = lax.dot_general(qc, kc, (((1,), (1,)), ((), ())), preferred_element_type=F32) * w
            r1[d][rows, :] = jnp.dot(s.astype(BF16), vc, preferred_element_type=F32)
            kw = kc.astype(F32) * cols[:, 1:2]
            kvs[d][rows, :] = jnp.dot(kw.T.astype(BF16), vc, preferred_element_type=F32)
        return carry

    lax.fori_loop(0, nc, intra, 0, unroll=MLSTM_UNROLL)

    def inter_step(ci, d, state, m_prev):
        rows = pl.ds(pl.multiple_of(ci * c, c), c)
        cols = colv[d][rows, :]
        m_loc = cols[:, 2:3]
        inter = cols[:, 3:4] + m_prev
        m_t = jnp.maximum(m_loc, inter)
        r2 = jnp.dot(qs[rows, :], state.astype(BF16), preferred_element_type=F32)
        nd = jnp.exp(m_loc - m_t) * r1[d][rows, :] + jnp.exp(inter - m_t) * r2
        h_out = nd[:, :LANE] / jnp.maximum(jnp.abs(nd[:, LANE:LANE + 1]), jnp.exp(-m_t))
        sc = scal[d][pl.ds(ci, 1), :]
        b_last = sc[:, 0:1]
        a_max = sc[:, 1:2]
        m_new = jnp.maximum(b_last + m_prev, a_max)
        state_new = (jnp.exp(b_last + m_prev - m_new) * state
                     + jnp.exp(a_max - m_new) * kvs[d][rows, :])
        return h_out, state_new, m_new

    def body(i, carry):
        sf, mf, sb, mb = carry
        h_f, sf, mf = inter_step(i, 0, sf, mf)
        hf[pl.ds(pl.multiple_of(i * c, c), c), :] = h_f
        j = nc - 1 - i
        h_b, sb, mb = inter_step(j, 1, sb, mb)
        hb[pl.ds(pl.multiple_of(j * c, c), c), :] = h_b
        return sf, mf, sb, mb

    z_state = jnp.zeros((ML_HEAD_DIM, 2 * LANE), F32)
    z_m = jnp.zeros((1, 1), F32)
    lax.fori_loop(0, nc, body, (z_state, z_m, z_state, z_m))

    hs = hf[...] + hb[...]
    y = hs * lax.rsqrt(jnp.mean(hs * hs, axis=-1, keepdims=True) + RMS_EPS) * gml_ref[...]
    o_ref[...] = (y * jax.nn.sigmoid(og_ref[...])).astype(BF16)


def _mlstm(mqk, mv, mo, gt, conv_w, g_ml, nb, lp):
    blk = lambda col: pl.BlockSpec((lp, LANE), col)
    c = MLSTM_CHUNK
    nc = lp // c
    ncp = _round_up(nc, SUBLANE)
    gt4 = jnp.pad(gt.reshape(nb, N_GATES, nc, c), ((0, 0), (0, 0), (0, ncp - nc), (0, 0)))
    gate = lambda g: pl.BlockSpec((1, 1, ncp, c), lambda b, h: (b, g * ML_HEADS + h, 0, 0))
    vec = lambda rows, cols=LANE: pltpu.VMEM((rows, cols), F32)
    return pl.pallas_call(
        _mlstm_body,
        grid=(nb, ML_HEADS),
        in_specs=[
            blk(lambda b, h: (b, h)),
            blk(lambda b, h: (b, ML_HEADS + h)),
            blk(lambda b, h: (b, h)),
            blk(lambda b, h: (b, h)),
            gate(0), gate(1), gate(2), gate(3),
            pl.BlockSpec((3, LANE), lambda b, h: (0, h)),
            pl.BlockSpec((3, LANE), lambda b, h: (0, ML_HEADS + h)),
            pl.BlockSpec((1, LANE), lambda b, h: (0, 0)),
        ],
        out_specs=blk(lambda b, h: (b, h)),
        out_shape=jax.ShapeDtypeStruct((nb * lp, ML_WIDTH), BF16),
        scratch_shapes=[
            pltpu.VMEM((lp, LANE), BF16),
            pltpu.VMEM((lp, LANE), BF16),
            pltpu.VMEM((lp, 2 * LANE), BF16),
            vec(lp), vec(lp),
            vec(ncp * SUBLANE), vec(ncp * SUBLANE),
            vec(ncp), vec(ncp),
            vec(ncp), vec(ncp),
            vec(lp), vec(lp),
            vec(lp, 2 * LANE), vec(lp, 2 * LANE),
            vec(lp, 2 * LANE), vec(lp, 2 * LANE),
        ],
        compiler_params=_cparams(2),
        name="bidir_mlstm",
    )(mqk, mqk, mv, mo, gt4, gt4, gt4, gt4, conv_w, conv_w, g_ml)


def _outproj_body(x_ref, oa_ref, hm_ref, wa_ref, wm_ref, gffn_ref, wr_ref, valid_ref,
                  x2_ref, xn_ref, affc_ref, afft_ref):
    x2 = (x_ref[...]
          + jnp.dot(oa_ref[...], wa_ref[...], preferred_element_type=F32)
          + jnp.dot(hm_ref[...], wm_ref[...], preferred_element_type=F32))
    x2_ref[...] = x2
    xn = x2 * lax.rsqrt(jnp.mean(x2 * x2, axis=-1, keepdims=True) + RMS_EPS) * gffn_ref[...]
    xn_ref[...] = xn.astype(BF16)
    xh = xn.astype(BF16)
    xl = (xn - xh.astype(F32)).astype(BF16)
    r_h = jnp.dot(xh, wr_ref[...], preferred_element_type=F32)
    logits = (r_h[:, :LANE] + r_h[:, LANE:]
              + jnp.dot(xl, wr_ref[:, :LANE], preferred_element_type=F32))
    lane = lax.broadcasted_iota(I32, logits.shape, 1)
    real = lane < N_EXPERTS
    logits = jnp.where(real, logits, -jnp.inf)
    e = jnp.exp(logits - jnp.max(logits, axis=-1, keepdims=True))
    aff = e / jnp.sum(e, axis=-1, keepdims=True)
    aff = jnp.where(real & (valid_ref[...] > 0.0), aff, -1.0)
    affc_ref[...] = aff
    afft_ref[...] = aff.T[:N_EXPERTS, :]


def _outproj(xp, oa, hm, w_a, w_m, g_ffn, w_r, valid, nb, lp):
    tm = _row_tile(lp)
    nt = lp // tm
    t_all = nb * lp
    row = lambda i: (i, 0)
    const = lambda i: (0, 0)
    return pl.pallas_call(
        _outproj_body,
        grid=(nb * nt,),
        in_specs=[
            pl.BlockSpec((tm, D_MODEL), row),
            pl.BlockSpec((tm, DA_WIDTH), row),
            pl.BlockSpec((tm, ML_WIDTH), row),
            pl.BlockSpec((DA_WIDTH, D_MODEL), const),
            pl.BlockSpec((ML_WIDTH, D_MODEL), const),
            pl.BlockSpec((1, D_MODEL), const),
            pl.BlockSpec((D_MODEL, 2 * LANE), const),
            pl.BlockSpec((tm, 1), lambda i: (i % nt, 0)),
        ],
        out_specs=[
            pl.BlockSpec((tm, D_MODEL), row),
            pl.BlockSpec((tm, D_MODEL), row),
            pl.BlockSpec((tm, LANE), row),
            pl.BlockSpec((N_EXPERTS, tm), lambda i: (0, i)),
        ],
        out_shape=[
            jax.ShapeDtypeStruct((t_all, D_MODEL), F32),
            jax.ShapeDtypeStruct((t_all, D_MODEL), BF16),
            jax.ShapeDtypeStruct((t_all, LANE), F32),
            jax.ShapeDtypeStruct((N_EXPERTS, t_all), F32),
        ],
        compiler_params=_cparams(1),
        name="outproj_router",
    )(xp, oa, hm, w_a, w_m, g_ffn, w_r, valid)


def _rope_tables(lp):
    pos = jnp.arange(lp, dtype=F32) - float(PAD_ROWS)
    inv = ROPE_THETA ** (-jnp.arange(0, ROPE_DIM, 2, dtype=F32) / ROPE_DIM)
    ang = pos[:, None] * inv[None, :]
    cos8, sin8 = jnp.cos(ang), jnp.sin(ang)
    half = ROPE_DIM // 2
    one_block = jnp.ones((lp, DA_QK_DIM - ROPE_DIM), F32)
    zero8 = jnp.zeros((lp, half), F32)
    zero_block = jnp.zeros((lp, DA_QK_DIM - ROPE_DIM), F32)
    cos_c = jnp.concatenate([cos8, cos8, one_block], axis=1)
    sa_c = jnp.concatenate([zero8, sin8, zero_block], axis=1)
    sb_c = jnp.concatenate([-sin8, zero8, zero_block], axis=1)
    tile2 = lambda a: jnp.concatenate([a, a], axis=1)
    return tile2(cos_c), tile2(sa_c), tile2(sb_c)


def _pad_tokens(x, meta):
    b = x.shape[0]
    m = jnp.broadcast_to(meta.astype(x.dtype)[None], (b, N_META, D_MODEL))
    z = jnp.zeros((b, PAD_ROWS, D_MODEL), x.dtype)
    return jnp.concatenate([z, m, x], axis=1)


def _mixer(xp, nb, lp, g_mix, w_in, b_gates, conv_w, g_q, g_k, lamv, g_da_out, g_ml_out, w_out,
           g_ffn, w_router):
    w_in_p = jnp.pad(w_in.astype(BF16), ((0, 0), (0, D_IN_PAD - D_IN)))
    bg = jnp.pad(b_gates, (0, LANE - N_GATES)).reshape(1, LANE)
    cos, sa, sb = _rope_tables(lp)
    gq = jnp.tile(g_q, 2).reshape(1, LANE)
    gk = jnp.tile(g_k, 2).reshape(1, LANE)
    qk, v, mqk, mv, mo, gt = _inproj(xp, w_in_p, g_mix.reshape(1, D_MODEL), bg, cos, sa, sb,
                                     gq, gk, nb, lp)
    bias = jnp.where(jnp.arange(FRONT) < PAD_ROWS, NEG_BIG, 0.0).astype(F32).reshape(1, FRONT)
    oa = _attention(lamv, qk, v, bias, g_da_out.reshape(1, LANE), nb, lp)
    hm = _mlstm(mqk, mv, mo, gt, conv_w, g_ml_out.reshape(1, LANE), nb, lp)
    w_o = w_out.astype(BF16)
    w_r = jnp.pad(w_router, ((0, 0), (0, LANE - N_EXPERTS)))
    w_r_hi = w_r.astype(BF16)
    w_r = jnp.concatenate([w_r_hi, (w_r - w_r_hi.astype(F32)).astype(BF16)], axis=1)
    valid = (jnp.arange(lp) >= PAD_ROWS).astype(F32).reshape(lp, 1)
    return _outproj(xp, oa, hm, w_o[:DA_WIDTH], w_o[DA_WIDTH:], g_ffn.reshape(1, D_MODEL), w_r,
                    valid, nb, lp)


TOK_TILE = LANE
WIN = 32
SUB = SUBLANE
FFN_ROWS = 512
FF_CHUNK = 256


def _route_body(aff_ref, thr_ref, cnt_ref, eqa_ref, seg_ref, *, cap, nt):
    a = aff_ref[...]
    bits = pltpu.bitcast(a, I32)
    capf = float(cap)

    def search(i, v):
        trial = v | lax.shift_left(jnp.int32(1), 30 - i)
        c = jnp.sum((bits >= trial).astype(F32), axis=-1, keepdims=True)
        return jnp.where(c >= capf, trial, v)

    thr_bits = lax.fori_loop(0, 31, search, jnp.zeros((N_EXPERTS, 1), I32))
    gt = bits > thr_bits
    eq = bits == thr_bits
    need = capf - jnp.sum(gt.astype(F32), axis=-1, keepdims=True)
    thr_ref[...] = jnp.broadcast_to(pltpu.bitcast(thr_bits, F32), (N_EXPERTS, LANE))

    slab = LANE * TOK_TILE
    ind = (lax.broadcasted_iota(I32, (slab, LANE), 0) // TOK_TILE
           == lax.broadcasted_iota(I32, (slab, LANE), 1)).astype(BF16)
    upper = (lax.broadcasted_iota(I32, (LANE, LANE), 0)
             < lax.broadcasted_iota(I32, (LANE, LANE), 1)).astype(BF16)
    eq_base = jnp.zeros((N_EXPERTS, 1), F32)
    seg_base = jnp.zeros((N_EXPERTS, 1), F32)
    for s0 in range(0, nt, LANE):
        n_t = min(LANE, nt - s0)
        cols = slice(s0 * TOK_TILE, (s0 + n_t) * TOK_TILE)
        gt_cnt = jnp.dot(gt[:, cols].astype(BF16), ind[:n_t * TOK_TILE], preferred_element_type=F32)
        eq_cnt = jnp.dot(eq[:, cols].astype(BF16), ind[:n_t * TOK_TILE], preferred_element_type=F32)
        eq_before = eq_base + jnp.dot(eq_cnt.astype(BF16), upper, preferred_element_type=F32)
        eq_allow = jnp.clip(need - eq_before, 0.0, eq_cnt)
        cnt = gt_cnt + eq_allow
        seg = seg_base + jnp.dot(cnt.astype(BF16), upper, preferred_element_type=F32)
        eq_base = eq_base + jnp.sum(eq_cnt, axis=-1, keepdims=True)
        seg_base = seg_base + jnp.sum(cnt, axis=-1, keepdims=True)
        cnt_ref[:, s0:s0 + LANE] = cnt.astype(I32)
        eqa_ref[:, s0:s0 + LANE] = eq_allow.astype(I32)
        seg_ref[:, s0:s0 + LANE] = seg.astype(I32)


def _route(aff_t, cap):
    tg = aff_t.shape[1]
    nt = tg // TOK_TILE
    ntp = -(-nt // LANE) * LANE
    tab = jax.ShapeDtypeStruct((N_EXPERTS, ntp), I32)
    return pl.pallas_call(
        functools.partial(_route_body, cap=cap, nt=nt),
        out_shape=[jax.ShapeDtypeStruct((N_EXPERTS, LANE), F32), tab, tab, tab],
        compiler_params=pltpu.CompilerParams(vmem_limit_bytes=VMEM_LIMIT),
        name="route",
    )(aff_t)


def _window_rows(cnt, p):
    return jnp.clip(cnt - p * WIN, 0, WIN)


def _dispatch_body(seg_sm, cnt_sm, xn_ref, aff_ref, thr_ref, eqa_ref, *rest,
                   ntp, base, stride, tail_start):
    xg_ref, pbuf, stage, zbuf, sem = rest[-5:]
    j = pl.program_id(0)

    if tail_start is not None:
        @pl.when(j == 0)
        def _zero_tail():
            zbuf[...] = jnp.zeros_like(zbuf)
            copies = [pltpu.make_async_copy(
                zbuf, xg_ref.at[pl.ds((e * stride + tail_start) * SUB, zbuf.shape[0])], sem.at[e])
                for e in range(N_EXPERTS)]
            for c in copies:
                c.start()
            for c in copies:
                c.wait()

    a = aff_ref[...]
    thr = thr_ref[...]
    lane = lax.broadcasted_iota(I32, (N_EXPERTS, LANE), 1)
    eqa = jnp.sum(jnp.where(lane == j % LANE, eqa_ref[...], 0).astype(F32), axis=-1, keepdims=True)
    upper = (lax.broadcasted_iota(I32, (LANE, LANE), 0)
             < lax.broadcasted_iota(I32, (LANE, LANE), 1)).astype(BF16)
    gt = a > thr
    eq = a == thr
    eq_rank = jnp.dot(eq.astype(BF16), upper, preferred_element_type=F32)
    sel = gt | (eq & (eq_rank < eqa))
    rank = jnp.dot(sel.astype(BF16), upper, preferred_element_type=F32)
    rank = jnp.where(sel, rank, -1.0)

    cnts = [cnt_sm[e * ntp + j] for e in range(N_EXPERTS)]
    segs = [seg_sm[e * ntp + j] for e in range(N_EXPERTS)]
    n_pass = (functools.reduce(jnp.maximum, cnts) + WIN - 1) // WIN
    slot = lax.broadcasted_iota(I32, (WIN, LANE), 0).astype(F32)

    def one_pass(p, carry):
        off = (p * WIN).astype(F32)
        for e in range(N_EXPERTS):
            r_e = jnp.broadcast_to(rank[e:e + 1, :], (WIN, LANE))
            pbuf[e * WIN:(e + 1) * WIN, :] = (r_e == slot + off).astype(BF16)
        g = jnp.dot(pbuf[...], xn_ref[...], preferred_element_type=F32)
        for s in range(SUB):
            stage[pl.ds(s, N_EXPERTS * WIN, stride=SUB), :] = g[:, s * LANE:(s + 1) * LANE]

        def copy(e):
            n = _window_rows(cnts[e], p) * SUB
            dst = pl.multiple_of((e * stride + base + segs[e] + p * WIN) * SUB, SUB)
            return pltpu.make_async_copy(stage.at[pl.ds(e * WIN * SUB, n)],
                                         xg_ref.at[pl.ds(dst, n)], sem.at[e])

        for e in range(N_EXPERTS):
            @pl.when(cnts[e] > p * WIN)
            def _start():
                copy(e).start()
        for e in range(N_EXPERTS):
            @pl.when(cnts[e] > p * WIN)
            def _wait():
                copy(e).wait()
        return carry

    lax.fori_loop(0, n_pass, one_pass, 0)


def _dispatch(seg, cnt, xn, aff_t, thr, eqa, xg, *, tile0, nt, ntp, base, stride, tail_start):
    first = xg is None
    zero_tail = first and stride > tail_start
    tail_rows = (stride - tail_start) * SUB if zero_tail else SUB
    kern = functools.partial(_dispatch_body, ntp=ntp, base=base, stride=stride,
                             tail_start=tail_start if zero_tail else None)
    in_specs = [
        pl.BlockSpec((TOK_TILE, D_MODEL), lambda j, s, c: (tile0 + j, 0)),
        pl.BlockSpec((N_EXPERTS, TOK_TILE), lambda j, s, c: (0, tile0 + j)),
        pl.BlockSpec((N_EXPERTS, LANE), lambda j, s, c: (0, 0)),
        pl.BlockSpec((N_EXPERTS, LANE), lambda j, s, c: (0, j // LANE)),
    ]
    args = [seg, cnt, xn, aff_t, thr, eqa]
    aliases = {}
    if not first:
        in_specs.append(pl.BlockSpec(memory_space=pl.ANY))
        args.append(xg)
        aliases = {len(args) - 1: 0}
    return pl.pallas_call(
        kern,
        grid_spec=pltpu.PrefetchScalarGridSpec(
            num_scalar_prefetch=2,
            grid=(nt,),
            in_specs=in_specs,
            out_specs=pl.BlockSpec(memory_space=pl.ANY),
            scratch_shapes=[
                pltpu.VMEM((N_EXPERTS * WIN, TOK_TILE), BF16),
                pltpu.VMEM((N_EXPERTS * WIN * SUB, LANE), F32),
                pltpu.VMEM((tail_rows, LANE), F32),
                pltpu.SemaphoreType.DMA((N_EXPERTS,)),
            ]),
        out_shape=jax.ShapeDtypeStruct((N_EXPERTS * stride * SUB, LANE), F32),
        input_output_aliases=aliases,
        compiler_params=pltpu.CompilerParams(dimension_semantics=("arbitrary",),
                                             vmem_limit_bytes=VMEM_LIMIT, has_side_effects=True),
        name="dispatch",
    )(*args)


def _ffn_body(xg_ref, wg_ref, wu_ref, wd_ref, wr_ref, o_ref, xb):
    e = pl.program_id(0)
    tm = xb.shape[0]
    for s in range(SUB):
        xb[:, s * LANE:(s + 1) * LANE] = xg_ref[pl.ds(s, tm, stride=SUB), :].astype(BF16)
    x = xb[...]
    logits = jnp.dot(x, wr_ref[...], preferred_element_type=F32)
    lane = lax.broadcasted_iota(I32, logits.shape, 1)
    logits = jnp.where(lane < N_EXPERTS, logits, -jnp.inf)
    ex = jnp.exp(logits - jnp.max(logits, axis=-1, keepdims=True))
    aff = ex / jnp.sum(ex, axis=-1, keepdims=True)
    gate = jnp.sum(jnp.where(lane == e, aff, 0.0), axis=-1, keepdims=True)
    acc = jnp.zeros((tm, D_MODEL), F32)
    for f in range(0, D_FF, FF_CHUNK):
        g = jnp.dot(x, wg_ref[0, :, f:f + FF_CHUNK], preferred_element_type=F32)
        u = jnp.dot(x, wu_ref[0, :, f:f + FF_CHUNK], preferred_element_type=F32)
        h = (g * jax.nn.sigmoid(g) * u).astype(BF16)
        acc = acc + jnp.dot(h, wd_ref[0, f:f + FF_CHUNK, :], preferred_element_type=F32)
    y = (acc * gate).astype(BF16).astype(F32)
    for s in range(SUB):
        o_ref[pl.ds(s, tm, stride=SUB), :] = y[:, s * LANE:(s + 1) * LANE]


def _ffn(xg, wg, wu, wd, w_r, stride):
    nr = stride // FFN_ROWS
    blk = pl.BlockSpec((FFN_ROWS * SUB, LANE), lambda e, i: (e * nr + i, 0))
    return pl.pallas_call(
        _ffn_body,
        grid=(N_EXPERTS, nr),
        in_specs=[
            blk,
            pl.BlockSpec((1, D_MODEL, D_FF), lambda e, i: (e, 0, 0)),
            pl.BlockSpec((1, D_MODEL, D_FF), lambda e, i: (e, 0, 0)),
            pl.BlockSpec((1, D_FF, D_MODEL), lambda e, i: (e, 0, 0)),
            pl.BlockSpec((D_MODEL, LANE), lambda e, i: (0, 0)),
        ],
        out_specs=blk,
        out_shape=jax.ShapeDtypeStruct(xg.shape, F32),
        scratch_shapes=[pltpu.VMEM((FFN_ROWS, D_MODEL), BF16)],
        compiler_params=_cparams(2),
        name="expert_ffn",
    )(xg, wg, wu, wd, w_r)


def _combine_body(seg_sm, cnt_sm, x2_ref, aff_ref, thr_ref, eqa_ref, ye_ref, o_ref,
                  ybuf, ymat, sem, *, ntp, ntl, base, stride):
    j = pl.program_id(0)

    @pl.when(j == 0)
    def _init():
        ybuf[...] = jnp.zeros_like(ybuf)

    @pl.when(j % ntl > 0)
    def _tile():
        a = aff_ref[...]
        thr = thr_ref[...]
        eqa = eqa_ref[0]
        lower = (lax.broadcasted_iota(I32, (TOK_TILE, TOK_TILE), 1)
                 < lax.broadcasted_iota(I32, (TOK_TILE, TOK_TILE), 0)).astype(BF16)
        gt = a > thr
        eq = a == thr
        eq_rank = jnp.dot(lower, eq.astype(BF16), preferred_element_type=F32)
        sel = gt | (eq & (eq_rank < eqa))
        rank = jnp.dot(lower, sel.astype(BF16), preferred_element_type=F32)
        rank = jnp.where(sel, rank, -1.0).astype(BF16)
        ncol = N_EXPERTS * WIN
        rep = (lax.broadcasted_iota(I32, (LANE, ncol), 1) // WIN
               == lax.broadcasted_iota(I32, (LANE, ncol), 0)).astype(BF16)
        rank_rep = jnp.dot(rank, rep, preferred_element_type=F32)
        col_slot = (lax.broadcasted_iota(I32, (1, ncol), 1) % WIN).astype(F32)

        cnts = [cnt_sm[e * ntp + j] for e in range(N_EXPERTS)]
        segs = [seg_sm[e * ntp + j] for e in range(N_EXPERTS)]
        n_pass = (functools.reduce(jnp.maximum, cnts) + WIN - 1) // WIN

        def one_pass(p, acc):
            def copy(e):
                n = _window_rows(cnts[e], p) * SUB
                src = pl.multiple_of((e * stride + base + segs[e] + p * WIN) * SUB, SUB)
                return pltpu.make_async_copy(ye_ref.at[pl.ds(src, n)],
                                             ybuf.at[pl.ds(e * WIN * SUB, n)], sem.at[e])

            for e in range(N_EXPERTS):
                @pl.when(cnts[e] > p * WIN)
                def _start():
                    copy(e).start()
            onehot = (rank_rep == col_slot + (p * WIN).astype(F32)).astype(BF16)
            for e in range(N_EXPERTS):
                @pl.when(cnts[e] > p * WIN)
                def _wait():
                    copy(e).wait()
            for s in range(SUB):
                ymat[:, s * LANE:(s + 1) * LANE] = ybuf[pl.ds(s, ncol, stride=SUB), :].astype(BF16)
            return acc + jnp.dot(onehot, ymat[...], preferred_element_type=F32)

        o_ref[...] = lax.fori_loop(0, n_pass, one_pass, x2_ref[...])


def _combine(seg, cnt, x2, aff_c, thr_row, eqa_rows, ye, *, tile0, nt, ntp, ntl, base, stride):
    ncol = N_EXPERTS * WIN
    n_out = nt // ntl * (ntl - 1)
    out_map = lambda j, s, c: (j // ntl * (ntl - 1) + jnp.maximum(j % ntl - 1, 0), 0)
    return pl.pallas_call(
        functools.partial(_combine_body, ntp=ntp, ntl=ntl, base=base, stride=stride),
        grid_spec=pltpu.PrefetchScalarGridSpec(
            num_scalar_prefetch=2,
            grid=(nt,),
            in_specs=[
                pl.BlockSpec((TOK_TILE, D_MODEL), lambda j, s, c: (tile0 + j, 0)),
                pl.BlockSpec((TOK_TILE, LANE), lambda j, s, c: (tile0 + j, 0)),
                pl.BlockSpec((1, LANE), lambda j, s, c: (0, 0)),
                pl.BlockSpec((1, 1, LANE), lambda j, s, c: (j, 0, 0)),
                pl.BlockSpec(memory_space=pl.ANY),
            ],
            out_specs=pl.BlockSpec((TOK_TILE, D_MODEL), out_map),
            scratch_shapes=[
                pltpu.VMEM((ncol * SUB, LANE), F32),
                pltpu.VMEM((ncol, D_MODEL), BF16),
                pltpu.SemaphoreType.DMA((N_EXPERTS,)),
            ]),
        out_shape=jax.ShapeDtypeStruct((n_out * TOK_TILE, D_MODEL), F32),
        compiler_params=pltpu.CompilerParams(dimension_semantics=("arbitrary",),
                                             vmem_limit_bytes=VMEM_LIMIT),
        name="combine",
    )(seg, cnt, x2, aff_c, thr_row, eqa_rows, ye)


def _round_up(a, b):
    return -(-a // b) * b


def kernel(x_prompt, x_sample, meta, g_mix, w_in, b_gates, conv_w, g_q, g_k, lam_q1, lam_k1, lam_q2,
           lam_k2, g_da_out, g_ml_out, w_out, g_ffn, w_router, w_gate, w_up, w_down):
    assert x_prompt.shape[1:] == x_sample.shape[1:]
    groups = (x_prompt.shape[0], x_sample.shape[0])
    x = jnp.concatenate([x_prompt, x_sample], axis=0)
    nb, s, _ = x.shape
    lp = FRONT + s
    assert lp % LANE == 0
    ntl = lp // TOK_TILE
    xp = _pad_tokens(x, meta).reshape(nb * lp, D_MODEL)
    lamv = jnp.stack([lam_q1[0], lam_k1[0], lam_q2[0], lam_k2[0]])
    x2, xn, aff_c, aff_t = _mixer(xp, nb, lp, g_mix[0], w_in[0], b_gates[0], conv_w[0], g_q[0],
                                  g_k[0], lamv, g_da_out[0], g_ml_out[0], w_out[0], g_ffn[0],
                                  w_router[0])

    caps = [max(1, CAP_FACTOR * b * (N_META + s) // N_EXPERTS) for b in groups]
    bases = [sum(caps[:i]) for i in range(len(caps))]
    tail_start = sum(caps)
    stride = _round_up(tail_start, FFN_ROWS)
    infos = []
    tile0 = 0
    for b, cap in zip(groups, caps):
        nt = b * ntl
        thr, cnt, eqa, seg = _route(aff_t[:, tile0 * TOK_TILE:(tile0 + nt) * TOK_TILE], cap)
        ntp = cnt.shape[1]
        thr_row = jnp.pad(thr[:, 0], (0, LANE - N_EXPERTS), constant_values=2.0).reshape(1, LANE)
        eqa_rows = jnp.pad(eqa.T.astype(F32), ((0, 0), (0, LANE - N_EXPERTS))).reshape(ntp, 1, LANE)
        infos.append(dict(tile0=tile0, nt=nt, ntp=ntp, cap=cap, thr=thr, thr_row=thr_row, eqa=eqa,
                          eqa_rows=eqa_rows, seg=seg.reshape(-1), cnt=cnt.reshape(-1)))
        tile0 += nt

    xg = None
    for gi, info in enumerate(infos):
        xg = _dispatch(info["seg"], info["cnt"], xn, aff_t, info["thr"], info["eqa"], xg,
                       tile0=info["tile0"], nt=info["nt"], ntp=info["ntp"], base=bases[gi],
                       stride=stride, tail_start=tail_start)
    w_r = jnp.pad(w_router[0].astype(BF16), ((0, 0), (0, LANE - N_EXPERTS)))
    ye = _ffn(xg, w_gate[0].astype(BF16), w_up[0].astype(BF16), w_down[0].astype(BF16), w_r, stride)

    outs = []
    for gi, (b, info) in enumerate(zip(groups, infos)):
        y = _combine(info["seg"], info["cnt"], x2, aff_c, info["thr_row"], info["eqa_rows"], ye,
                     tile0=info["tile0"], nt=info["nt"], ntp=info["ntp"], ntl=ntl, base=bases[gi],
                     stride=stride)
        outs.append(y.reshape(b, s, D_MODEL))
    return tuple(outs)
```

```python
import functools
import math

import jax
import jax.numpy as jnp
from jax import lax
from jax.experimental import pallas as pl
from jax.experimental.pallas import tpu as pltpu

F32 = jnp.float32
BF16 = jnp.bfloat16
I32 = jnp.int32

D_MODEL = 1024
N_META = 16
LANE = 128
SUBLANE = 8
FRONT = LANE
PAD_ROWS = FRONT - N_META
RMS_EPS = 1e-6
DA_HEADS = 4
DA_WIDTH = 512
DA_QK_DIM = 64
ROPE_DIM = 16
ROPE_THETA = 500000.0
ML_HEADS = 4
ML_WIDTH = 512
ML_HEAD_DIM = 128
NEG_BIG = -1e30
N_GATES = 16
D_IN = 3600
D_IN_PAD = 3712
N_EXPERTS = 16
CAP_FACTOR = 2
D_FF = 2816
LAM_INIT = 0.8 - 0.6 * math.exp(-0.3 * 0)
LOG2E = math.log2(math.e)
MLSTM_CHUNK = LANE
MLSTM_UNROLL = 3
VMEM_LIMIT = 56 * 1024 * 1024


def _cparams(n_axes):
    return pltpu.CompilerParams(dimension_semantics=("arbitrary",) * n_axes,
                                vmem_limit_bytes=VMEM_LIMIT)


def _row_tile(lp):
    for t in (384, 256, 128):
        if lp % t == 0:
            return t
    raise ValueError(lp)


def _inproj_body(x_ref, w_ref, gmix_ref, bg_ref, cos_ref, sa_ref, sb_ref, gq_ref, gk_ref,
                 qk_ref, v_ref, mqk_ref, mv_ref, mo_ref, gt_ref):
    x = x_ref[...]
    h = x * lax.rsqrt(jnp.mean(x * x, axis=-1, keepdims=True) + RMS_EPS) * gmix_ref[...]
    z = jnp.dot(h.astype(BF16), w_ref[...], preferred_element_type=F32)
    lane = lax.broadcasted_iota(I32, (1, LANE), 1)
    lo = lane < DA_QK_DIM
    cos = cos_ref[...]
    sa = sa_ref[...]
    sb = sb_ref[...]

    def normrope(u, g, scale):
        sq = u * u
        s_lo = jnp.sum(jnp.where(lo, sq, 0.0), axis=-1, keepdims=True)
        s_hi = jnp.sum(jnp.where(lo, 0.0, sq), axis=-1, keepdims=True)
        ms = jnp.where(lo, s_lo, s_hi) * (1.0 / DA_QK_DIM)
        y = u * lax.rsqrt(ms + RMS_EPS) * g
        y = y * cos + pltpu.roll(y, 8, 1) * sa + pltpu.roll(y, LANE - 8, 1) * sb
        return y * scale

    for hh in range(DA_HEADS):
        c0 = hh * LANE
        qk_ref[:, c0:c0 + LANE] = normrope(z[:, c0:c0 + LANE], gq_ref[...],
                                           DA_QK_DIM ** -0.5 * LOG2E).astype(BF16)
        c1 = DA_WIDTH + hh * LANE
        qk_ref[:, c1:c1 + LANE] = normrope(z[:, c1:c1 + LANE], gk_ref[...], 1.0).astype(BF16)
    v_ref[...] = z[:, 1024:1536].astype(BF16)
    mqk_ref[...] = z[:, 1536:2560]
    mv_ref[...] = z[:, 2560:3072]
    mo_ref[...] = z[:, 3072:3584]
    g = z[:, 3584:3712] + bg_ref[...]
    gt_ref[0] = g.T[:N_GATES, :]


def _inproj(xp, w_in_p, g_mix, bg, cos, sa, sb, gq, gk, nb, lp):
    tm = _row_tile(lp)
    nt = lp // tm
    t_all = nb * lp
    row = lambda i: (i, 0)
    tab = lambda i: (i % nt, 0)
    const = lambda i: (0, 0)
    return pl.pallas_call(
        _inproj_body,
        grid=(nb * nt,),
        in_specs=[
            pl.BlockSpec((tm, D_MODEL), row),
            pl.BlockSpec((D_MODEL, D_IN_PAD), const),
            pl.BlockSpec((1, D_MODEL), const),
            pl.BlockSpec((1, LANE), const),
            pl.BlockSpec((tm, LANE), tab),
            pl.BlockSpec((tm, LANE), tab),
            pl.BlockSpec((tm, LANE), tab),
            pl.BlockSpec((1, LANE), const),
            pl.BlockSpec((1, LANE), const),
        ],
        out_specs=[
            pl.BlockSpec((tm, 1024), row),
            pl.BlockSpec((tm, 512), row),
            pl.BlockSpec((tm, 1024), row),
            pl.BlockSpec((tm, 512), row),
            pl.BlockSpec((tm, 512), row),
            pl.BlockSpec((1, N_GATES, tm), lambda i: (i // nt, 0, i % nt)),
        ],
        out_shape=[
            jax.ShapeDtypeStruct((t_all, 1024), BF16),
            jax.ShapeDtypeStruct((t_all, 512), BF16),
            jax.ShapeDtypeStruct((t_all, 1024), F32),
            jax.ShapeDtypeStruct((t_all, 512), F32),
            jax.ShapeDtypeStruct((t_all, 512), F32),
            jax.ShapeDtypeStruct((nb, N_GATES, lp), F32),
        ],
        compiler_params=_cparams(1),
        name="inproj",
    )(xp, w_in_p, g_mix, bg, cos, sa, sb, gq, gk)


def _attn_q_rows(lp):
    for t in (192, 128):
        if lp % t == 0:
            return t
    raise ValueError(lp)


def _attn_body(lam_ref, q_ref, k_ref, v_ref, bias_ref, g_ref, o_ref,
               va, s0, s1, m0, m1, e0, e1):
    lp = k_ref.shape[0]
    tq = s0.shape[1]
    nblk = lp // tq
    va[:, :LANE] = v_ref[...]
    va[:, LANE:] = jnp.where(lax.broadcasted_iota(I32, (lp, LANE), 1) == 0, 1.0, 0.0).astype(BF16)
    e1[...] = jnp.zeros_like(e1)

    lv = lam_ref[...]
    lam = (jnp.exp(jnp.sum(lv[0:1] * lv[1:2], axis=-1, keepdims=True))
           - jnp.exp(jnp.sum(lv[2:3] * lv[3:4], axis=-1, keepdims=True)) + LAM_INIT)
    first = lax.broadcasted_iota(I32, (tq, LANE), 1) < DA_QK_DIM
    nt_dims = (((1,), (1,)), ((), ()))

    def rows_of(blk):
        return pl.ds(pl.multiple_of(blk * tq, tq), tq)

    def stage_a(blk, s_ref, m_ref):
        q = q_ref[rows_of(blk), :]
        zero = jnp.zeros_like(q)
        for j, qj in enumerate((jnp.where(first, q, zero), jnp.where(first, zero, q))):
            s = lax.dot_general(qj, k_ref[...], nt_dims, preferred_element_type=F32)
            s_pad = s[:, :FRONT] + bias_ref[...]
            s_ref[j, :, :FRONT] = s_pad
            s_ref[j, :, FRONT:] = s[:, FRONT:]
            m_ref[j] = jnp.maximum(jnp.max(s_pad, axis=-1, keepdims=True),
                                   jnp.max(s[:, FRONT:], axis=-1, keepdims=True))

    def stage_b(s_ref, m_ref, e_ref):
        for j in range(2):
            e_ref[j] = jnp.exp2(s_ref[j] - m_ref[j]).astype(BF16)

    def stage_c(blk, e_ref):
        r0 = jnp.dot(e_ref[0], va[...], preferred_element_type=F32)
        r1 = jnp.dot(e_ref[1], va[...], preferred_element_type=F32)
        o = r0[:, :LANE] / r0[:, LANE:LANE + 1] - lam * (r1[:, :LANE] / r1[:, LANE:LANE + 1])
        o = o * lax.rsqrt(jnp.mean(o * o, axis=-1, keepdims=True) + RMS_EPS) * g_ref[...]
        o_ref[rows_of(blk), :] = (o * (1.0 - LAM_INIT)).astype(BF16)

    stage_a(0, s0, m0)

    def step(t, carry):
        nxt = jnp.minimum(t + 1, nblk - 1)
        prv = jnp.maximum(t - 1, 0)

        @pl.when(t % 2 == 0)
        def _even():
            stage_a(nxt, s1, m1)
            stage_b(s0, m0, e0)
            stage_c(prv, e1)

        @pl.when(t % 2 == 1)
        def _odd():
            stage_a(nxt, s0, m0)
            stage_b(s1, m1, e1)
            stage_c(prv, e0)

        return carry

    lax.fori_loop(0, nblk, step, 0)
    stage_c(nblk - 1, e0 if nblk % 2 == 1 else e1)


def _attention(lamv, qk, v, bias, g_out, nb, lp):
    tq = _attn_q_rows(lp)
    blk = lambda col: pl.BlockSpec((lp, LANE), col)
    return pl.pallas_call(
        _attn_body,
        grid=(nb, DA_HEADS),
        in_specs=[
            pl.BlockSpec((4, DA_QK_DIM), lambda b, h: (0, 0)),
            blk(lambda b, h: (b, h)),
            blk(lambda b, h: (b, DA_HEADS + h)),
            blk(lambda b, h: (b, h)),
            pl.BlockSpec((1, FRONT), lambda b, h: (0, 0)),
            pl.BlockSpec((1, LANE), lambda b, h: (0, 0)),
        ],
        out_specs=blk(lambda b, h: (b, h)),
        out_shape=jax.ShapeDtypeStruct((nb * lp, DA_WIDTH), BF16),
        scratch_shapes=[
            pltpu.VMEM((lp, 2 * LANE), BF16),
            pltpu.VMEM((2, tq, lp), F32), pltpu.VMEM((2, tq, lp), F32),
            pltpu.VMEM((2, tq, 1), F32), pltpu.VMEM((2, tq, 1), F32),
            pltpu.VMEM((2, tq, lp), BF16), pltpu.VMEM((2, tq, lp), BF16),
        ],
        compiler_params=_cparams(2),
        name="diff_attention",
    )(lamv, qk, qk, v, bias, g_out)


def _log_sigmoid(x):
    return jnp.minimum(x, 0.0) - jnp.log(1.0 + jnp.exp(-jnp.abs(x)))


def _lane_scan(x, forward, op, identity):
    lane = lax.broadcasted_iota(I32, x.shape, 1)
    sh = 1
    while sh < LANE:
        if forward:
            x = op(x, jnp.where(lane >= sh, pltpu.roll(x, sh, 1), identity))
        else:
            x = op(x, jnp.where(lane < LANE - sh, pltpu.roll(x, LANE - sh, 1), identity))
        sh *= 2
    return x


def _mlstm_body(q_ref, k_ref, v_ref, og_ref, gif_ref, gff_ref, gib_ref, gfb_ref,
                cwq_ref, cwk_ref, gml_ref, o_ref,
                qs, ks, va, hf, hb, rowv0, rowv1, uvec0, uvec1, scal0, scal1, colv, r1, kvs):
    gates = (gif_ref, gff_ref, gib_ref, gfb_ref)
    rowv, uvec, scal = (rowv0, rowv1), (uvec0, uvec1), (scal0, scal1)
    lp = q_ref.shape[0]
    nc = lp // MLSTM_CHUNK
    c = MLSTM_CHUNK
    row = lax.broadcasted_iota(I32, (lp, 1), 0)

    def conv_silu(x, w):
        x_prev = jnp.where(row == 0, 0.0, pltpu.roll(x, 1, 0))
        x_next = jnp.where(row == lp - 1, 0.0, pltpu.roll(x, lp - 1, 0))
        y = x_prev * w[0:1] + x * w[1:2] + x_next * w[2:3]
        return y * jax.nn.sigmoid(y)

    qs[...] = (conv_silu(q_ref[...], cwq_ref[...]) * (ML_HEAD_DIM ** -0.5)).astype(BF16)
    ks[...] = conv_silu(k_ref[...], cwk_ref[...]).astype(BF16)
    va[:, :LANE] = v_ref[...].astype(BF16)
    va[:, LANE:] = jnp.where(lax.broadcasted_iota(I32, (lp, LANE), 1) == 0, 1.0, 0.0).astype(BF16)

    r_i = lax.broadcasted_iota(I32, (c, c), 0)
    c_i = lax.broadcasted_iota(I32, (c, c), 1)
    ncp = gates[0].shape[2]
    lane_c = lax.broadcasted_iota(I32, (ncp, c), 1)
    pos = lax.broadcasted_iota(I32, (ncp, c), 0) * c + lane_c
    valid = pos >= PAD_ROWS

    for d, forward in enumerate((True, False)):
        li = jnp.where(valid, gates[2 * d][0, 0], NEG_BIG)
        lf = jnp.where(valid, _log_sigmoid(gates[2 * d + 1][0, 0]), 0.0)
        cum = _lane_scan(lf, forward, jnp.add, 0.0)
        u = li - cum
        cm = _lane_scan(u, forward, jnp.maximum, -jnp.inf)
        b_last = cum[:, c - 1:c] if forward else cum[:, 0:1]
        a = b_last + u
        a_max = jnp.max(a, axis=-1, keepdims=True)
        rowv[d][...] = jnp.zeros_like(rowv[d])
        for r, vec in enumerate((cm, jnp.exp(a - a_max), cum + cm, cum)):
            rowv[d][pl.ds(r, ncp, stride=SUBLANE), :] = vec
        uvec[d][...] = u
        scal[d][...] = jnp.where(lane_c == 0, b_last, a_max)

    zpad = jnp.zeros((c - SUBLANE, c), F32)

    def rows_of(ci):
        return pl.ds(pl.multiple_of(ci * c, c), c)

    def intra(ci, d, slot):
        forward = d == 0
        rows = rows_of(ci)
        qc = qs[rows, :]
        kc = ks[rows, :]
        vc = va[rows, :]
        tile = rowv[d][pl.ds(pl.multiple_of(ci * SUBLANE, SUBLANE), SUBLANE), :]
        cols = jnp.concatenate([tile, zpad], axis=0).T
        colv[d, slot] = cols
        mask = (c_i <= r_i) if forward else (c_i >= r_i)
        w = jnp.exp(jnp.where(mask, uvec[d][pl.ds(ci, 1), :] - cols[:, 0:1], -jnp.inf))
        s = lax.dot_general(qc, kc, (((1,), (1,)), ((), ())), preferred_element_type=F32) * w
        r1[d, slot] = jnp.dot(s.astype(BF16), vc, preferred_element_type=F32)
        kw = kc.astype(F32) * cols[:, 1:2]
        kvs[d, slot] = jnp.dot(kw.T.astype(BF16), vc, preferred_element_type=F32)

    def inter_step(ci, d, slot, state, m_prev):
        cols = colv[d, slot]
        m_loc = cols[:, 2:3]
        inter = cols[:, 3:4] + m_prev
        m_t = jnp.maximum(m_loc, inter)
        r2 = jnp.dot(qs[rows_of(ci), :], state.astype(BF16), preferred_element_type=F32)
        nd = jnp.exp(m_loc - m_t) * r1[d, slot] + jnp.exp(inter - m_t) * r2
        h_out = nd[:, :LANE] / jnp.maximum(jnp.abs(nd[:, LANE:LANE + 1]), jnp.exp(-m_t))
        sc = scal[d][pl.ds(ci, 1), :]
        b_last = sc[:, 0:1]
        a_max = sc[:, 1:2]
        m_new = jnp.maximum(b_last + m_prev, a_max)
        state_new = (jnp.exp(b_last + m_prev - m_new) * state
                     + jnp.exp(a_max - m_new) * kvs[d, slot])
        return h_out, state_new, m_new

    def intra_both(ci, carry):
        intra(ci, 0, ci)
        intra(ci, 1, ci)
        return carry

    lax.fori_loop(0, nc, intra_both, 0, unroll=MLSTM_UNROLL)

    def body(i, carry):
        sf, mf, sb, mb = carry
        h_f, sf, mf = inter_step(i, 0, i, sf, mf)
        hf[rows_of(i), :] = h_f
        j = nc - 1 - i
        h_b, sb, mb = inter_step(j, 1, j, sb, mb)
        hb[rows_of(j), :] = h_b
        return sf, mf, sb, mb

    z_state = jnp.zeros((ML_HEAD_DIM, 2 * LANE), F32)
    z_m = jnp.zeros((1, 1), F32)
    lax.fori_loop(0, nc, body, (z_state, z_m, z_state, z_m))

    hs = hf[...] + hb[...]
    y = hs * lax.rsqrt(jnp.mean(hs * hs, axis=-1, keepdims=True) + RMS_EPS) * gml_ref[...]
    o_ref[...] = (y * jax.nn.sigmoid(og_ref[...])).astype(BF16)


def _mlstm(mqk, mv, mo, gt, conv_w, g_ml, nb, lp):
    blk = lambda col: pl.BlockSpec((lp, LANE), col)
    c = MLSTM_CHUNK
    nc = lp // c
    ncp = _round_up(nc, SUBLANE)
    gt4 = jnp.pad(gt.reshape(nb, N_GATES, nc, c), ((0, 0), (0, 0), (0, ncp - nc), (0, 0)))
    gate = lambda g: pl.BlockSpec((1, 1, ncp, c), lambda b, h: (b, g * ML_HEADS + h, 0, 0))
    vec = lambda rows, cols=LANE: pltpu.VMEM((rows, cols), F32)
    return pl.pallas_call(
        _mlstm_body,
        grid=(nb, ML_HEADS),
        in_specs=[
            blk(lambda b, h: (b, h)),
            blk(lambda b, h: (b, ML_HEADS + h)),
            blk(lambda b, h: (b, h)),
            blk(lambda b, h: (b, h)),
            gate(0), gate(1), gate(2), gate(3),
            pl.BlockSpec((3, LANE), lambda b, h: (0, h)),
            pl.BlockSpec((3, LANE), lambda b, h: (0, ML_HEADS + h)),
            pl.BlockSpec((1, LANE), lambda b, h: (0, 0)),
        ],
        out_specs=blk(lambda b, h: (b, h)),
        out_shape=jax.ShapeDtypeStruct((nb * lp, ML_WIDTH), BF16),
        scratch_shapes=[
            pltpu.VMEM((lp, LANE), BF16),
            pltpu.VMEM((lp, LANE), BF16),
            pltpu.VMEM((lp, 2 * LANE), BF16),
            vec(lp), vec(lp),
            vec(ncp * SUBLANE), vec(ncp * SUBLANE),
            vec(ncp), vec(ncp),
            vec(ncp), vec(ncp),
            pltpu.VMEM((2, nc, c, LANE), F32),
            pltpu.VMEM((2, nc, c, 2 * LANE), F32),
            pltpu.VMEM((2, nc, ML_HEAD_DIM, 2 * LANE), F32),
        ],
        compiler_params=_cparams(2),
        name="bidir_mlstm",
    )(mqk, mqk, mv, mo, gt4, gt4, gt4, gt4, conv_w, conv_w, g_ml)


def _outproj_body(x_ref, oa_ref, hm_ref, wa_ref, wm_ref, gffn_ref, wr_ref, valid_ref,
                  x2_ref, xn_ref, affc_ref, afft_ref):
    x2 = (x_ref[...]
          + jnp.dot(oa_ref[...], wa_ref[...], preferred_element_type=F32)
          + jnp.dot(hm_ref[...], wm_ref[...], preferred_element_type=F32))
    x2_ref[...] = x2
    xn = x2 * lax.rsqrt(jnp.mean(x2 * x2, axis=-1, keepdims=True) + RMS_EPS) * gffn_ref[...]
    xn_ref[...] = xn.astype(BF16)
    xh = xn.astype(BF16)
    xl = (xn - xh.astype(F32)).astype(BF16)
    r_h = jnp.dot(xh, wr_ref[...], preferred_element_type=F32)
    logits = (r_h[:, :LANE] + r_h[:, LANE:]
              + jnp.dot(xl, wr_ref[:, :LANE], preferred_element_type=F32))
    lane = lax.broadcasted_iota(I32, logits.shape, 1)
    real = lane < N_EXPERTS
    logits = jnp.where(real, logits, -jnp.inf)
    e = jnp.exp(logits - jnp.max(logits, axis=-1, keepdims=True))
    aff = e / jnp.sum(e, axis=-1, keepdims=True)
    aff = jnp.where(real & (valid_ref[...] > 0.0), aff, -1.0)
    affc_ref[...] = aff
    afft_ref[...] = aff.T[:N_EXPERTS, :]


def _outproj(xp, oa, hm, w_a, w_m, g_ffn, w_r, valid, nb, lp):
    tm = _row_tile(lp)
    nt = lp // tm
    t_all = nb * lp
    row = lambda i: (i, 0)
    const = lambda i: (0, 0)
    return pl.pallas_call(
        _outproj_body,
        grid=(nb * nt,),
        in_specs=[
            pl.BlockSpec((tm, D_MODEL), row),
            pl.BlockSpec((tm, DA_WIDTH), row),
            pl.BlockSpec((tm, ML_WIDTH), row),
            pl.BlockSpec((DA_WIDTH, D_MODEL), const),
            pl.BlockSpec((ML_WIDTH, D_MODEL), const),
            pl.BlockSpec((1, D_MODEL), const),
            pl.BlockSpec((D_MODEL, 2 * LANE), const),
            pl.BlockSpec((tm, 1), lambda i: (i % nt, 0)),
        ],
        out_specs=[
            pl.BlockSpec((tm, D_MODEL), row),
            pl.BlockSpec((tm, D_MODEL), row),
            pl.BlockSpec((tm, LANE), row),
            pl.BlockSpec((N_EXPERTS, tm), lambda i: (0, i)),
        ],
        out_shape=[
            jax.ShapeDtypeStruct((t_all, D_MODEL), F32),
            jax.ShapeDtypeStruct((t_all, D_MODEL), BF16),
            jax.ShapeDtypeStruct((t_all, LANE), F32),
            jax.ShapeDtypeStruct((N_EXPERTS, t_all), F32),
        ],
        compiler_params=_cparams(1),
        name="outproj_router",
    )(xp, oa, hm, w_a, w_m, g_ffn, w_r, valid)


def _rope_tables(lp):
    pos = jnp.arange(lp, dtype=F32) - float(PAD_ROWS)
    inv = ROPE_THETA ** (-jnp.arange(0, ROPE_DIM, 2, dtype=F32) / ROPE_DIM)
    ang = pos[:, None] * inv[None, :]
    cos8, sin8 = jnp.cos(ang), jnp.sin(ang)
    half = ROPE_DIM // 2
    one_block = jnp.ones((lp, DA_QK_DIM - ROPE_DIM), F32)
    zero8 = jnp.zeros((lp, half), F32)
    zero_block = jnp.zeros((lp, DA_QK_DIM - ROPE_DIM), F32)
    cos_c = jnp.concatenate([cos8, cos8, one_block], axis=1)
    sa_c = jnp.concatenate([zero8, sin8, zero_block], axis=1)
    sb_c = jnp.concatenate([-sin8, zero8, zero_block], axis=1)
    tile2 = lambda a: jnp.concatenate([a, a], axis=1)
    return tile2(cos_c), tile2(sa_c), tile2(sb_c)


def _pad_tokens(x, meta):
    b = x.shape[0]
    m = jnp.broadcast_to(meta.astype(x.dtype)[None], (b, N_META, D_MODEL))
    z = jnp.zeros((b, PAD_ROWS, D_MODEL), x.dtype)
    return jnp.concatenate([z, m, x], axis=1)


def _mixer(xp, nb, lp, g_mix, w_in, b_gates, conv_w, g_q, g_k, lamv, g_da_out, g_ml_out, w_out,
           g_ffn, w_router):
    w_in_p = jnp.pad(w_in.astype(BF16), ((0, 0), (0, D_IN_PAD - D_IN)))
    bg = jnp.pad(b_gates, (0, LANE - N_GATES)).reshape(1, LANE)
    cos, sa, sb = _rope_tables(lp)
    gq = jnp.tile(g_q, 2).reshape(1, LANE)
    gk = jnp.tile(g_k, 2).reshape(1, LANE)
    qk, v, mqk, mv, mo, gt = _inproj(xp, w_in_p, g_mix.reshape(1, D_MODEL), bg, cos, sa, sb,
                                     gq, gk, nb, lp)
    bias = jnp.where(jnp.arange(FRONT) < PAD_ROWS, NEG_BIG, 0.0).astype(F32).reshape(1, FRONT)
    oa = _attention(lamv, qk, v, bias, g_da_out.reshape(1, LANE), nb, lp)
    hm = _mlstm(mqk, mv, mo, gt, conv_w, g_ml_out.reshape(1, LANE), nb, lp)
    w_o = w_out.astype(BF16)
    w_r = jnp.pad(w_router, ((0, 0), (0, LANE - N_EXPERTS)))
    w_r_hi = w_r.astype(BF16)
    w_r = jnp.concatenate([w_r_hi, (w_r - w_r_hi.astype(F32)).astype(BF16)], axis=1)
    valid = (jnp.arange(lp) >= PAD_ROWS).astype(F32).reshape(lp, 1)
    return _outproj(xp, oa, hm, w_o[:DA_WIDTH], w_o[DA_WIDTH:], g_ffn.reshape(1, D_MODEL), w_r,
                    valid, nb, lp)


TOK_TILE = LANE
WIN = 32
SUB = SUBLANE
FFN_ROWS = 512
FF_CHUNK = 256


def _route_body(aff_ref, thr_ref, cnt_ref, eqa_ref, seg_ref, *, cap, nt):
    a = aff_ref[...]
    capf = float(cap)

    def search(i, v):
        trial = v | lax.shift_left(jnp.int32(1), 30 - i)
        c = jnp.sum((a >= pltpu.bitcast(trial, F32)).astype(F32), axis=-1, keepdims=True)
        return jnp.where(c >= capf, trial, v)

    thr_bits = lax.fori_loop(0, 31, search, jnp.zeros((N_EXPERTS, 1), I32))
    thr = pltpu.bitcast(thr_bits, F32)
    gt = a > thr
    eq = a == thr
    need = capf - jnp.sum(gt.astype(F32), axis=-1, keepdims=True)
    thr_ref[...] = jnp.broadcast_to(thr, (N_EXPERTS, LANE))

    slab = LANE * TOK_TILE
    ind = (lax.broadcasted_iota(I32, (slab, LANE), 0) // TOK_TILE
           == lax.broadcasted_iota(I32, (slab, LANE), 1)).astype(BF16)
    upper = (lax.broadcasted_iota(I32, (LANE, LANE), 0)
             < lax.broadcasted_iota(I32, (LANE, LANE), 1)).astype(BF16)
    eq_base = jnp.zeros((N_EXPERTS, 1), F32)
    seg_base = jnp.zeros((N_EXPERTS, 1), F32)
    for s0 in range(0, nt, LANE):
        n_t = min(LANE, nt - s0)
        cols = slice(s0 * TOK_TILE, (s0 + n_t) * TOK_TILE)
        gt_cnt = jnp.dot(gt[:, cols].astype(BF16), ind[:n_t * TOK_TILE], preferred_element_type=F32)
        eq_cnt = jnp.dot(eq[:, cols].astype(BF16), ind[:n_t * TOK_TILE], preferred_element_type=F32)
        eq_before = eq_base + jnp.dot(eq_cnt.astype(BF16), upper, preferred_element_type=F32)
        eq_allow = jnp.clip(need - eq_before, 0.0, eq_cnt)
        cnt = gt_cnt + eq_allow
        seg = seg_base + jnp.dot(cnt.astype(BF16), upper, preferred_element_type=F32)
        eq_base = eq_base + jnp.sum(eq_cnt, axis=-1, keepdims=True)
        seg_base = seg_base + jnp.sum(cnt, axis=-1, keepdims=True)
        cnt_ref[:, s0:s0 + LANE] = cnt.astype(I32)
        eqa_ref[:, s0:s0 + LANE] = eq_allow.astype(I32)
        seg_ref[:, s0:s0 + LANE] = seg.astype(I32)


def _route(aff_t, cap):
    tg = aff_t.shape[1]
    nt = tg // TOK_TILE
    ntp = -(-nt // LANE) * LANE
    tab = jax.ShapeDtypeStruct((N_EXPERTS, ntp), I32)
    return pl.pallas_call(
        functools.partial(_route_body, cap=cap, nt=nt),
        out_shape=[jax.ShapeDtypeStruct((N_EXPERTS, LANE), F32), tab, tab, tab],
        compiler_params=pltpu.CompilerParams(vmem_limit_bytes=VMEM_LIMIT),
        name="route",
    )(aff_t)


def _window_rows(cnt, p):
    return jnp.clip(cnt - p * WIN, 0, WIN)


def _dispatch_body(seg_sm, cnt_sm, xn_ref, aff_ref, thr_ref, eqa_ref, *rest,
                   nt, ntp, base, stride, tail_start):
    xg_ref, pbuf, stage, stage_x, zbuf, sem = rest[-6:]
    j = pl.program_id(0)

    if tail_start is not None:
        @pl.when(j == 0)
        def _zero_tail():
            zbuf[...] = jnp.zeros_like(zbuf)
            copies = [pltpu.make_async_copy(
                zbuf, xg_ref.at[pl.ds((e * stride + tail_start) * SUB, zbuf.shape[0])],
                sem.at[2].at[e])
                for e in range(N_EXPERTS)]
            for c in copies:
                c.start()
            for c in copies:
                c.wait()

    a = aff_ref[...]
    thr = thr_ref[...]
    lane = lax.broadcasted_iota(I32, (N_EXPERTS, LANE), 1)
    eqa = jnp.sum(jnp.where(lane == j % LANE, eqa_ref[...], 0).astype(F32), axis=-1, keepdims=True)
    upper = (lax.broadcasted_iota(I32, (LANE, LANE), 0)
             < lax.broadcasted_iota(I32, (LANE, LANE), 1)).astype(BF16)
    gt = a > thr
    eq = a == thr
    eq_rank = jnp.dot(eq.astype(BF16), upper, preferred_element_type=F32)
    sel = gt | (eq & (eq_rank < eqa))
    rank = jnp.dot(sel.astype(BF16), upper, preferred_element_type=F32)
    rank = jnp.where(sel, rank, -1.0)

    n_pass = (functools.reduce(jnp.maximum, [cnt_sm[e * ntp + j] for e in range(N_EXPERTS)])
              + WIN - 1) // WIN
    win_row = lax.broadcasted_iota(I32, (WIN, LANE), 0).astype(F32)

    def gather_rows(p, dst):
        off = (p * WIN).astype(F32) if not isinstance(p, int) else float(p * WIN)
        for e in range(N_EXPERTS):
            r_e = jnp.broadcast_to(rank[e:e + 1, :], (WIN, LANE))
            pbuf[e * WIN:(e + 1) * WIN, :] = (r_e == win_row + off).astype(BF16)
        g = jnp.dot(pbuf[...], xn_ref[...], preferred_element_type=F32)
        for s in range(SUB):
            dst[pl.ds(s, N_EXPERTS * WIN, stride=SUB), :] = g[:, s * LANE:(s + 1) * LANE]

    def copy(t, e, p, src, sem_row):
        n = _window_rows(cnt_sm[e * ntp + t], p) * SUB
        row = pl.multiple_of((e * stride + base + seg_sm[e * ntp + t] + p * WIN) * SUB, SUB)
        return pltpu.make_async_copy(src.at[pl.ds(e * WIN * SUB, n)], xg_ref.at[pl.ds(row, n)],
                                     sem_row.at[e])

    def for_experts(t, p, fn):
        for e in range(N_EXPERTS):
            @pl.when(cnt_sm[e * ntp + t] > p * WIN)
            def _():
                fn(e)

    slot = j % 2
    mine = stage.at[slot]

    @pl.when(j >= 2)
    def _reuse():
        for_experts(j - 2, 0, lambda e: copy(j - 2, e, 0, mine, sem.at[slot]).wait())

    gather_rows(0, mine)
    for_experts(j, 0, lambda e: copy(j, e, 0, mine, sem.at[slot]).start())

    def extra_pass(p, carry):
        gather_rows(p, stage_x)
        for_experts(j, p, lambda e: copy(j, e, p, stage_x, sem.at[2]).start())
        for_experts(j, p, lambda e: copy(j, e, p, stage_x, sem.at[2]).wait())
        return carry

    lax.fori_loop(1, n_pass, extra_pass, 0)

    @pl.when(j == nt - 1)
    def _drain():
        if nt >= 2:
            for_experts(j - 1, 0, lambda e: copy(j - 1, e, 0, stage.at[1 - slot], sem.at[1 - slot]).wait())
        for_experts(j, 0, lambda e: copy(j, e, 0, mine, sem.at[slot]).wait())


def _dispatch(seg, cnt, xn, aff_t, thr, eqa, xg, *, tile0, nt, ntp, base, stride, tail_start):
    first = xg is None
    zero_tail = first and stride > tail_start
    tail_rows = (stride - tail_start) * SUB if zero_tail else SUB
    kern = functools.partial(_dispatch_body, nt=nt, ntp=ntp, base=base, stride=stride,
                             tail_start=tail_start if zero_tail else None)
    in_specs = [
        pl.BlockSpec((TOK_TILE, D_MODEL), lambda j, s, c: (tile0 + j, 0)),
        pl.BlockSpec((N_EXPERTS, TOK_TILE), lambda j, s, c: (0, tile0 + j)),
        pl.BlockSpec((N_EXPERTS, LANE), lambda j, s, c: (0, 0)),
        pl.BlockSpec((N_EXPERTS, LANE), lambda j, s, c: (0, j // LANE)),
    ]
    args = [seg, cnt, xn, aff_t, thr, eqa]
    aliases = {}
    if not first:
        in_specs.append(pl.BlockSpec(memory_space=pl.ANY))
        args.append(xg)
        aliases = {len(args) - 1: 0}
    return pl.pallas_call(
        kern,
        grid_spec=pltpu.PrefetchScalarGridSpec(
            num_scalar_prefetch=2,
            grid=(nt,),
            in_specs=in_specs,
            out_specs=pl.BlockSpec(memory_space=pl.ANY),
            scratch_shapes=[
                pltpu.VMEM((N_EXPERTS * WIN, TOK_TILE), BF16),
                pltpu.VMEM((2, N_EXPERTS * WIN * SUB, LANE), F32),
                pltpu.VMEM((N_EXPERTS * WIN * SUB, LANE), F32),
                pltpu.VMEM((tail_rows, LANE), F32),
                pltpu.SemaphoreType.DMA((3, N_EXPERTS)),
            ]),
        out_shape=jax.ShapeDtypeStruct((N_EXPERTS * stride * SUB, LANE), F32),
        input_output_aliases=aliases,
        compiler_params=pltpu.CompilerParams(dimension_semantics=("arbitrary",),
                                             vmem_limit_bytes=VMEM_LIMIT, has_side_effects=True),
        name="dispatch",
    )(*args)


def _ffn_body(xg_ref, wg_ref, wu_ref, wd_ref, wr_ref, o_ref, xb):
    e = pl.program_id(0)
    tm = xb.shape[0]
    for s in range(SUB):
        xb[:, s * LANE:(s + 1) * LANE] = xg_ref[pl.ds(s, tm, stride=SUB), :].astype(BF16)
    x = xb[...]
    logits = jnp.dot(x, wr_ref[...], preferred_element_type=F32)
    lane = lax.broadcasted_iota(I32, logits.shape, 1)
    logits = jnp.where(lane < N_EXPERTS, logits, -jnp.inf)
    ex = jnp.exp(logits - jnp.max(logits, axis=-1, keepdims=True))
    aff = ex / jnp.sum(ex, axis=-1, keepdims=True)
    gate = jnp.sum(jnp.where(lane == e, aff, 0.0), axis=-1, keepdims=True)
    acc = jnp.zeros((tm, D_MODEL), F32)
    for f in range(0, D_FF, FF_CHUNK):
        g = jnp.dot(x, wg_ref[0, :, f:f + FF_CHUNK], preferred_element_type=F32)
        u = jnp.dot(x, wu_ref[0, :, f:f + FF_CHUNK], preferred_element_type=F32)
        h = (g * jax.nn.sigmoid(g) * u).astype(BF16)
        acc = acc + jnp.dot(h, wd_ref[0, f:f + FF_CHUNK, :], preferred_element_type=F32)
    y = (acc * gate).astype(BF16).astype(F32)
    for s in range(SUB):
        o_ref[pl.ds(s, tm, stride=SUB), :] = y[:, s * LANE:(s + 1) * LANE]


def _ffn(xg, wg, wu, wd, w_r, stride):
    nr = stride // FFN_ROWS
    blk = pl.BlockSpec((FFN_ROWS * SUB, LANE), lambda e, i: (e * nr + i, 0))
    return pl.pallas_call(
        _ffn_body,
        grid=(N_EXPERTS, nr),
        in_specs=[
            blk,
            pl.BlockSpec((1, D_MODEL, D_FF), lambda e, i: (e, 0, 0)),
            pl.BlockSpec((1, D_MODEL, D_FF), lambda e, i: (e, 0, 0)),
            pl.BlockSpec((1, D_FF, D_MODEL), lambda e, i: (e, 0, 0)),
            pl.BlockSpec((D_MODEL, LANE), lambda e, i: (0, 0)),
        ],
        out_specs=blk,
        out_shape=jax.ShapeDtypeStruct(xg.shape, F32),
        scratch_shapes=[pltpu.VMEM((FFN_ROWS, D_MODEL), BF16)],
        compiler_params=_cparams(2),
        name="expert_ffn",
    )(xg, wg, wu, wd, w_r)


def _combine_body(seg_sm, cnt_sm, x2_ref, aff_ref, thr_ref, eqa_ref, ye_ref, o_ref,
                  ybuf, ybuf_x, ymat, sem, *, nt, ntp, ntl, base, stride):
    j = pl.program_id(0)
    ncol = N_EXPERTS * WIN

    def copy(t, e, p, dst, sem_row):
        n = _window_rows(cnt_sm[e * ntp + t], p) * SUB
        row = pl.multiple_of((e * stride + base + seg_sm[e * ntp + t] + p * WIN) * SUB, SUB)
        return pltpu.make_async_copy(ye_ref.at[pl.ds(row, n)], dst.at[pl.ds(e * WIN * SUB, n)],
                                     sem_row.at[e])

    def for_experts(t, p, fn):
        for e in range(N_EXPERTS):
            @pl.when(cnt_sm[e * ntp + t] > p * WIN)
            def _():
                fn(e)

    @pl.when(j == 0)
    def _init():
        ybuf[...] = jnp.zeros_like(ybuf)
        ybuf_x[...] = jnp.zeros_like(ybuf_x)

    nxt = jnp.minimum(j + 1, nt - 1)

    @pl.when((j + 1 < nt) & (nxt % ntl > 0))
    def _prefetch():
        for_experts(nxt, 0, lambda e: copy(nxt, e, 0, ybuf.at[nxt % 2], sem.at[nxt % 2]).start())

    def scatter_rows(onehot, src):
        for s in range(SUB):
            ymat[:, s * LANE:(s + 1) * LANE] = src[pl.ds(s, ncol, stride=SUB), :].astype(BF16)
        return jnp.dot(onehot, ymat[...], preferred_element_type=F32)

    @pl.when(j % ntl > 0)
    def _tile():
        a = aff_ref[...]
        thr = thr_ref[...]
        eqa = eqa_ref[0]
        lower = (lax.broadcasted_iota(I32, (TOK_TILE, TOK_TILE), 1)
                 < lax.broadcasted_iota(I32, (TOK_TILE, TOK_TILE), 0)).astype(BF16)
        gt = a > thr
        eq = a == thr
        eq_rank = jnp.dot(lower, eq.astype(BF16), preferred_element_type=F32)
        sel = gt | (eq & (eq_rank < eqa))
        rank = jnp.dot(lower, sel.astype(BF16), preferred_element_type=F32)
        rank = jnp.where(sel, rank, -1.0).astype(BF16)
        rep = (lax.broadcasted_iota(I32, (LANE, ncol), 1) // WIN
               == lax.broadcasted_iota(I32, (LANE, ncol), 0)).astype(BF16)
        rank_rep = jnp.dot(rank, rep, preferred_element_type=F32)
        col_slot = (lax.broadcasted_iota(I32, (1, ncol), 1) % WIN).astype(F32)
        n_pass = (functools.reduce(jnp.maximum, [cnt_sm[e * ntp + j] for e in range(N_EXPERTS)])
                  + WIN - 1) // WIN

        mine = ybuf.at[j % 2]
        for_experts(j, 0, lambda e: copy(j, e, 0, mine, sem.at[j % 2]).wait())
        acc = x2_ref[...] + scatter_rows((rank_rep == col_slot).astype(BF16), mine)

        def extra_pass(p, acc):
            for_experts(j, p, lambda e: copy(j, e, p, ybuf_x, sem.at[2]).start())
            onehot = (rank_rep == col_slot + (p * WIN).astype(F32)).astype(BF16)
            for_experts(j, p, lambda e: copy(j, e, p, ybuf_x, sem.at[2]).wait())
            return acc + scatter_rows(onehot, ybuf_x)

        o_ref[...] = lax.fori_loop(1, n_pass, extra_pass, acc)


def _combine(seg, cnt, x2, aff_c, thr_row, eqa_rows, ye, *, tile0, nt, ntp, ntl, base, stride):
    ncol = N_EXPERTS * WIN
    n_out = nt // ntl * (ntl - 1)
    out_map = lambda j, s, c: (j // ntl * (ntl - 1) + jnp.maximum(j % ntl - 1, 0), 0)
    return pl.pallas_call(
        functools.partial(_combine_body, nt=nt, ntp=ntp, ntl=ntl, base=base, stride=stride),
        grid_spec=pltpu.PrefetchScalarGridSpec(
            num_scalar_prefetch=2,
            grid=(nt,),
            in_specs=[
                pl.BlockSpec((TOK_TILE, D_MODEL), lambda j, s, c: (tile0 + j, 0)),
                pl.BlockSpec((TOK_TILE, LANE), lambda j, s, c: (tile0 + j, 0)),
                pl.BlockSpec((1, LANE), lambda j, s, c: (0, 0)),
                pl.BlockSpec((1, 1, LANE), lambda j, s, c: (j, 0, 0)),
                pl.BlockSpec(memory_space=pl.ANY),
            ],
            out_specs=pl.BlockSpec((TOK_TILE, D_MODEL), out_map),
            scratch_shapes=[
                pltpu.VMEM((2, ncol * SUB, LANE), F32),
                pltpu.VMEM((ncol * SUB, LANE), F32),
                pltpu.VMEM((ncol, D_MODEL), BF16),
                pltpu.SemaphoreType.DMA((3, N_EXPERTS)),
            ]),
        out_shape=jax.ShapeDtypeStruct((n_out * TOK_TILE, D_MODEL), F32),
        compiler_params=pltpu.CompilerParams(dimension_semantics=("arbitrary",),
                                             vmem_limit_bytes=VMEM_LIMIT),
        name="combine",
    )(seg, cnt, x2, aff_c, thr_row, eqa_rows, ye)


def _round_up(a, b):
    return -(-a // b) * b


def kernel(x_prompt, x_sample, meta, g_mix, w_in, b_gates, conv_w, g_q, g_k, lam_q1, lam_k1, lam_q2,
           lam_k2, g_da_out, g_ml_out, w_out, g_ffn, w_router, w_gate, w_up, w_down):
    assert x_prompt.shape[1:] == x_sample.shape[1:]
    groups = (x_prompt.shape[0], x_sample.shape[0])
    x = jnp.concatenate([x_prompt, x_sample], axis=0)
    nb, s, _ = x.shape
    lp = FRONT + s
    assert lp % LANE == 0
    ntl = lp // TOK_TILE
    xp = _pad_tokens(x, meta).reshape(nb * lp, D_MODEL)
    lamv = jnp.stack([lam_q1[0], lam_k1[0], lam_q2[0], lam_k2[0]])
    x2, xn, aff_c, aff_t = _mixer(xp, nb, lp, g_mix[0], w_in[0], b_gates[0], conv_w[0], g_q[0],
                                  g_k[0], lamv, g_da_out[0], g_ml_out[0], w_out[0], g_ffn[0],
                                  w_router[0])

    caps = [max(1, CAP_FACTOR * b * (N_META + s) // N_EXPERTS) for b in groups]
    bases = [sum(caps[:i]) for i in range(len(caps))]
    tail_start = sum(caps)
    stride = _round_up(tail_start, FFN_ROWS)
    infos = []
    tile0 = 0
    for b, cap in zip(groups, caps):
        nt = b * ntl
        thr, cnt, eqa, seg = _route(aff_t[:, tile0 * TOK_TILE:(tile0 + nt) * TOK_TILE], cap)
        ntp = cnt.shape[1]
        thr_row = jnp.pad(thr[:, 0], (0, LANE - N_EXPERTS), constant_values=2.0).reshape(1, LANE)
        eqa_rows = jnp.pad(eqa.T.astype(F32), ((0, 0), (0, LANE - N_EXPERTS))).reshape(ntp, 1, LANE)
        infos.append(dict(tile0=tile0, nt=nt, ntp=ntp, cap=cap, thr=thr, thr_row=thr_row, eqa=eqa,
                          eqa_rows=eqa_rows, seg=seg.reshape(-1), cnt=cnt.reshape(-1)))
        tile0 += nt

    xg = None
    for gi, info in enumerate(infos):
        xg = _dispatch(info["seg"], info["cnt"], xn, aff_t, info["thr"], info["eqa"], xg,
                       tile0=info["tile0"], nt=info["nt"], ntp=info["ntp"], base=bases[gi],
                       stride=stride, tail_start=tail_start)
    w_r = jnp.pad(w_router[0].astype(BF16), ((0, 0), (0, LANE - N_EXPERTS)))
    ye = _ffn(xg, w_gate[0].astype(BF16), w_up[0].astype(BF16), w_down[0].astype(BF16), w_r, stride)

    outs = []
    for gi, (b, info) in enumerate(zip(groups, infos)):
        y = _combine(info["seg"], info["cnt"], x2, aff_c, info["thr_row"], info["eqa_rows"], ye,
                     tile0=info["tile0"], nt=info["nt"], ntp=info["ntp"], ntl=ntl, base=bases[gi],
                     stride=stride)
        outs.append(y.reshape(b, s, D_MODEL))
    return tuple(outs)
```

```python
import functools
import math

import jax
import jax.numpy as jnp
from jax import lax
from jax.experimental import pallas as pl
from jax.experimental.pallas import tpu as pltpu

F32 = jnp.float32
BF16 = jnp.bfloat16
I32 = jnp.int32

D_MODEL = 1024
N_META = 16
LANE = 128
SUBLANE = 8
FRONT = LANE
PAD_ROWS = FRONT - N_META
RMS_EPS = 1e-6
DA_HEADS = 4
DA_WIDTH = 512
DA_QK_DIM = 64
ROPE_DIM = 16
ROPE_THETA = 500000.0
ML_HEADS = 4
ML_WIDTH = 512
ML_HEAD_DIM = 128
NEG_BIG = -1e30
N_GATES = 16
D_IN = 3600
D_IN_PAD = 3712
N_EXPERTS = 16
CAP_FACTOR = 2
D_FF = 2816
LAM_INIT = 0.8 - 0.6 * math.exp(-0.3 * 0)
LOG2E = math.log2(math.e)
MLSTM_CHUNK = LANE
MLSTM_UNROLL = 3
VMEM_LIMIT = 56 * 1024 * 1024


def _cparams(n_axes):
    return pltpu.CompilerParams(dimension_semantics=("arbitrary",) * n_axes,
                                vmem_limit_bytes=VMEM_LIMIT)


def _row_tile(lp):
    for t in (384, 256, 128):
        if lp % t == 0:
            return t
    raise ValueError(lp)


def _inproj_body(x_ref, w_ref, gmix_ref, bg_ref, cos_ref, sa_ref, sb_ref, gq_ref, gk_ref,
                 qk_ref, v_ref, mqk_ref, mv_ref, mo_ref, gt_ref):
    x = x_ref[...]
    h = x * lax.rsqrt(jnp.mean(x * x, axis=-1, keepdims=True) + RMS_EPS) * gmix_ref[...]
    z = jnp.dot(h.astype(BF16), w_ref[...], preferred_element_type=F32)
    lane = lax.broadcasted_iota(I32, (1, LANE), 1)
    lo = lane < DA_QK_DIM
    cos = cos_ref[...]
    sa = sa_ref[...]
    sb = sb_ref[...]

    def normrope(u, g, scale):
        sq = u * u
        s_lo = jnp.sum(jnp.where(lo, sq, 0.0), axis=-1, keepdims=True)
        s_hi = jnp.sum(jnp.where(lo, 0.0, sq), axis=-1, keepdims=True)
        ms = jnp.where(lo, s_lo, s_hi) * (1.0 / DA_QK_DIM)
        y = u * lax.rsqrt(ms + RMS_EPS) * g
        y = y * cos + pltpu.roll(y, 8, 1) * sa + pltpu.roll(y, LANE - 8, 1) * sb
        return y * scale

    for hh in range(DA_HEADS):
        c0 = hh * LANE
        qk_ref[:, c0:c0 + LANE] = normrope(z[:, c0:c0 + LANE], gq_ref[...],
                                           DA_QK_DIM ** -0.5 * LOG2E).astype(BF16)
        c1 = DA_WIDTH + hh * LANE
        qk_ref[:, c1:c1 + LANE] = normrope(z[:, c1:c1 + LANE], gk_ref[...], 1.0).astype(BF16)
    v_ref[...] = z[:, 1024:1536].astype(BF16)
    mqk_ref[...] = z[:, 1536:2560]
    mv_ref[...] = z[:, 2560:3072].astype(BF16)
    mo_ref[...] = z[:, 3072:3584]
    g = z[:, 3584:3712] + bg_ref[...]
    gt_ref[0] = g.T[:N_GATES, :]


def _inproj(xp, w_in_p, g_mix, bg, cos, sa, sb, gq, gk, nb, lp):
    tm = _row_tile(lp)
    nt = lp // tm
    t_all = nb * lp
    row = lambda i: (i, 0)
    tab = lambda i: (i % nt, 0)
    const = lambda i: (0, 0)
    return pl.pallas_call(
        _inproj_body,
        grid=(nb * nt,),
        in_specs=[
            pl.BlockSpec((tm, D_MODEL), row),
            pl.BlockSpec((D_MODEL, D_IN_PAD), const),
            pl.BlockSpec((1, D_MODEL), const),
            pl.BlockSpec((1, LANE), const),
            pl.BlockSpec((tm, LANE), tab),
            pl.BlockSpec((tm, LANE), tab),
            pl.BlockSpec((tm, LANE), tab),
            pl.BlockSpec((1, LANE), const),
            pl.BlockSpec((1, LANE), const),
        ],
        out_specs=[
            pl.BlockSpec((tm, 1024), row),
            pl.BlockSpec((tm, 512), row),
            pl.BlockSpec((tm, 1024), row),
            pl.BlockSpec((tm, 512), row),
            pl.BlockSpec((tm, 512), row),
            pl.BlockSpec((1, N_GATES, tm), lambda i: (i // nt, 0, i % nt)),
        ],
        out_shape=[
            jax.ShapeDtypeStruct((t_all, 1024), BF16),
            jax.ShapeDtypeStruct((t_all, 512), BF16),
            jax.ShapeDtypeStruct((t_all, 1024), F32),
            jax.ShapeDtypeStruct((t_all, 512), BF16),
            jax.ShapeDtypeStruct((t_all, 512), F32),
            jax.ShapeDtypeStruct((nb, N_GATES, lp), F32),
        ],
        compiler_params=_cparams(1),
        name="inproj",
    )(xp, w_in_p, g_mix, bg, cos, sa, sb, gq, gk)


def _attn_q_rows(lp):
    for t in (192, 128):
        if lp % t == 0:
            return t
    raise ValueError(lp)


def _attn_body(lam_ref, q_ref, k_ref, v_ref, bias_ref, g_ref, o_ref,
               va, s0, s1, m0, m1, e0, e1):
    lp = k_ref.shape[0]
    tq = s0.shape[1]
    nblk = lp // tq
    va[:, :LANE] = v_ref[...]
    va[:, LANE:] = jnp.where(lax.broadcasted_iota(I32, (lp, LANE), 1) == 0, 1.0, 0.0).astype(BF16)
    e1[...] = jnp.zeros_like(e1)

    lv = lam_ref[...]
    lam = (jnp.exp(jnp.sum(lv[0:1] * lv[1:2], axis=-1, keepdims=True))
           - jnp.exp(jnp.sum(lv[2:3] * lv[3:4], axis=-1, keepdims=True)) + LAM_INIT)
    first = lax.broadcasted_iota(I32, (tq, LANE), 1) < DA_QK_DIM
    nt_dims = (((1,), (1,)), ((), ()))

    def rows_of(blk):
        return pl.ds(pl.multiple_of(blk * tq, tq), tq)

    def stage_a(blk, s_ref, m_ref):
        q = q_ref[rows_of(blk), :]
        zero = jnp.zeros_like(q)
        for j, qj in enumerate((jnp.where(first, q, zero), jnp.where(first, zero, q))):
            s = lax.dot_general(qj, k_ref[...], nt_dims, preferred_element_type=F32)
            s_pad = s[:, :FRONT] + bias_ref[...]
            s_ref[j, :, :FRONT] = s_pad
            s_ref[j, :, FRONT:] = s[:, FRONT:]
            m_ref[j] = jnp.maximum(jnp.max(s_pad, axis=-1, keepdims=True),
                                   jnp.max(s[:, FRONT:], axis=-1, keepdims=True))

    def stage_b(s_ref, m_ref, e_ref):
        for j in range(2):
            e_ref[j] = jnp.exp2(s_ref[j] - m_ref[j]).astype(BF16)

    def stage_c(blk, e_ref):
        r0 = jnp.dot(e_ref[0], va[...], preferred_element_type=F32)
        r1 = jnp.dot(e_ref[1], va[...], preferred_element_type=F32)
        o = r0[:, :LANE] / r0[:, LANE:LANE + 1] - lam * (r1[:, :LANE] / r1[:, LANE:LANE + 1])
        o = o * lax.rsqrt(jnp.mean(o * o, axis=-1, keepdims=True) + RMS_EPS) * g_ref[...]
        o_ref[rows_of(blk), :] = (o * (1.0 - LAM_INIT)).astype(BF16)

    stage_a(0, s0, m0)

    def step(t, carry):
        nxt = jnp.minimum(t + 1, nblk - 1)
        prv = jnp.maximum(t - 1, 0)

        @pl.when(t % 2 == 0)
        def _even():
            stage_a(nxt, s1, m1)
            stage_b(s0, m0, e0)
            stage_c(prv, e1)

        @pl.when(t % 2 == 1)
        def _odd():
            stage_a(nxt, s0, m0)
            stage_b(s1, m1, e1)
            stage_c(prv, e0)

        return carry

    lax.fori_loop(0, nblk, step, 0)
    stage_c(nblk - 1, e0 if nblk % 2 == 1 else e1)


def _attention(lamv, qk, v, bias, g_out, nb, lp):
    tq = _attn_q_rows(lp)
    blk = lambda col: pl.BlockSpec((lp, LANE), col)
    return pl.pallas_call(
        _attn_body,
        grid=(nb, DA_HEADS),
        in_specs=[
            pl.BlockSpec((4, DA_QK_DIM), lambda b, h: (0, 0)),
            blk(lambda b, h: (b, h)),
            blk(lambda b, h: (b, DA_HEADS + h)),
            blk(lambda b, h: (b, h)),
            pl.BlockSpec((1, FRONT), lambda b, h: (0, 0)),
            pl.BlockSpec((1, LANE), lambda b, h: (0, 0)),
        ],
        out_specs=blk(lambda b, h: (b, h)),
        out_shape=jax.ShapeDtypeStruct((nb * lp, DA_WIDTH), BF16),
        scratch_shapes=[
            pltpu.VMEM((lp, 2 * LANE), BF16),
            pltpu.VMEM((2, tq, lp), F32), pltpu.VMEM((2, tq, lp), F32),
            pltpu.VMEM((2, tq, 1), F32), pltpu.VMEM((2, tq, 1), F32),
            pltpu.VMEM((2, tq, lp), BF16), pltpu.VMEM((2, tq, lp), BF16),
        ],
        compiler_params=_cparams(2),
        name="diff_attention",
    )(lamv, qk, qk, v, bias, g_out)


def _log_sigmoid(x):
    return jnp.minimum(x, 0.0) - jnp.log(1.0 + jnp.exp(-jnp.abs(x)))


def _lane_scan(x, forward, op, identity):
    lane = lax.broadcasted_iota(I32, x.shape, 1)
    sh = 1
    while sh < LANE:
        if forward:
            x = op(x, jnp.where(lane >= sh, pltpu.roll(x, sh, 1), identity))
        else:
            x = op(x, jnp.where(lane < LANE - sh, pltpu.roll(x, LANE - sh, 1), identity))
        sh *= 2
    return x


def _mlstm_body(q_ref, k_ref, v_ref, og_ref, gif_ref, gff_ref, gib_ref, gfb_ref,
                cwq_ref, cwk_ref, gml_ref, o_ref,
                qs, ks, va, hf, hb, rowv0, rowv1, uvec0, uvec1, scal0, scal1, colv, r1, kvs):
    gates = (gif_ref, gff_ref, gib_ref, gfb_ref)
    rowv, uvec, scal = (rowv0, rowv1), (uvec0, uvec1), (scal0, scal1)
    lp = q_ref.shape[0]
    nc = lp // MLSTM_CHUNK
    c = MLSTM_CHUNK
    row = lax.broadcasted_iota(I32, (lp, 1), 0)

    row_c = lax.broadcasted_iota(I32, (c, 1), 0)

    def conv_chunk(ci, carry):
        start = pl.multiple_of(ci * c, c)
        before = pl.ds(pl.multiple_of(jnp.maximum(start - SUBLANE, 0), SUBLANE), SUBLANE)
        after = pl.ds(pl.multiple_of(jnp.minimum(start + c, lp - SUBLANE), SUBLANE), SUBLANE)
        for src, w_ref, dst, scale in ((q_ref, cwq_ref, qs, ML_HEAD_DIM ** -0.5),
                                       (k_ref, cwk_ref, ks, 1.0)):
            x = src[pl.ds(start, c), :]
            w = w_ref[...]
            last_before = jnp.where(ci > 0, src[before, :][SUBLANE - 1:SUBLANE, :], 0.0)
            first_after = jnp.where(ci < nc - 1, src[after, :][0:1, :], 0.0)
            x_prev = jnp.where(row_c == 0, last_before, pltpu.roll(x, 1, 0))
            x_next = jnp.where(row_c == c - 1, first_after, pltpu.roll(x, c - 1, 0))
            y = x_prev * w[0:1] + x * w[1:2] + x_next * w[2:3]
            dst[pl.ds(start, c), :] = (y * jax.nn.sigmoid(y) * scale).astype(BF16)
        return carry

    lax.fori_loop(0, nc, conv_chunk, 0)
    va[:, :LANE] = v_ref[...]
    va[:, LANE:] = jnp.ones((lp, LANE), BF16)

    r_i = lax.broadcasted_iota(I32, (c, c), 0)
    c_i = lax.broadcasted_iota(I32, (c, c), 1)
    ncp = gates[0].shape[2]
    lane_c = lax.broadcasted_iota(I32, (ncp, c), 1)
    pos = lax.broadcasted_iota(I32, (ncp, c), 0) * c + lane_c
    valid = pos >= PAD_ROWS

    for d, forward in enumerate((True, False)):
        li = jnp.where(valid, gates[2 * d][0, 0], NEG_BIG)
        lf = jnp.where(valid, _log_sigmoid(gates[2 * d + 1][0, 0]), 0.0)
        cum = _lane_scan(lf, forward, jnp.add, 0.0)
        u = li - cum
        cm = _lane_scan(u, forward, jnp.maximum, -jnp.inf)
        b_last = cum[:, c - 1:c] if forward else cum[:, 0:1]
        a = b_last + u
        a_max = jnp.max(a, axis=-1, keepdims=True)
        rowv[d][...] = jnp.zeros_like(rowv[d])
        for r, vec in enumerate((cm, jnp.exp(a - a_max), cum)):
            rowv[d][pl.ds(r, ncp, stride=SUBLANE), :] = vec
        uvec[d][...] = u
        scal[d][0] = jnp.broadcast_to(b_last, (ncp, c))
        scal[d][1] = jnp.broadcast_to(a_max, (ncp, c))

    def rows_of(ci):
        return pl.ds(pl.multiple_of(ci * c, c), c)

    def intra(ci, d, slot, qc, kc, k_t, vc):
        forward = d == 0
        tile = rowv[d][pl.ds(pl.multiple_of(ci * SUBLANE, SUBLANE), SUBLANE), :]

        def column(r):
            return jnp.broadcast_to(tile[r:r + 1, :], (c, c)).T

        cm_c = column(0)
        cum_c = column(2)
        colv[d, 0, slot] = cm_c
        colv[d, 1, slot] = cum_c
        mask = (c_i <= r_i) if forward else (c_i >= r_i)
        w = jnp.exp(jnp.where(mask, uvec[d][pl.ds(ci, 1), :] - cm_c, -jnp.inf))
        s = lax.dot_general(qc, kc, (((1,), (1,)), ((), ())), preferred_element_type=F32) * w
        r1[d, slot] = jnp.dot(s.astype(BF16), vc, preferred_element_type=F32)
        kw_t = k_t * tile[1:2, :]
        kvs[d, slot] = jnp.dot(kw_t.astype(BF16), vc, preferred_element_type=F32)

    def inter_step(ci, d, slot, state, m_prev):
        cm_c = colv[d, 0, slot]
        cum_c = colv[d, 1, slot]
        m_loc = cum_c + cm_c
        inter = cum_c + m_prev
        m_t = jnp.maximum(m_loc, inter)
        r2 = jnp.dot(qs[rows_of(ci), :], state.astype(BF16), preferred_element_type=F32)
        w_intra = jnp.exp(m_loc - m_t)
        w_inter = jnp.exp(inter - m_t)
        r1v = r1[d, slot]
        num = w_intra * r1v[:, :LANE] + w_inter * r2[:, :LANE]
        den = w_intra * r1v[:, LANE:] + w_inter * r2[:, LANE:]
        h_out = num / jnp.maximum(jnp.abs(den), jnp.exp(-m_t))
        b_last = scal[d][0, pl.ds(ci, 1), :]
        a_max = scal[d][1, pl.ds(ci, 1), :]
        m_new = jnp.maximum(b_last + m_prev, a_max)
        keep = jnp.exp(b_last + m_prev - m_new)
        add = jnp.exp(a_max - m_new)
        state_new = (jnp.concatenate([keep, keep], axis=1) * state
                     + jnp.concatenate([add, add], axis=1) * kvs[d, slot])
        return h_out, state_new, m_new

    def intra_both(ci, carry):
        rows = rows_of(ci)
        qc = qs[rows, :]
        kc = ks[rows, :]
        vc = va[rows, :]
        k_t = kc.astype(F32).T
        intra(ci, 0, ci, qc, kc, k_t, vc)
        intra(ci, 1, ci, qc, kc, k_t, vc)
        return carry

    lax.fori_loop(0, nc, intra_both, 0, unroll=MLSTM_UNROLL)

    def body(i, carry):
        sf, mf, sb, mb = carry
        h_f, sf, mf = inter_step(i, 0, i, sf, mf)
        hf[rows_of(i), :] = h_f
        j = nc - 1 - i
        h_b, sb, mb = inter_step(j, 1, j, sb, mb)
        hb[rows_of(j), :] = h_b
        return sf, mf, sb, mb

    z_state = jnp.zeros((ML_HEAD_DIM, 2 * LANE), F32)
    z_m = jnp.zeros((1, LANE), F32)
    lax.fori_loop(0, nc, body, (z_state, z_m, z_state, z_m))

    hs = hf[...] + hb[...]
    y = hs * lax.rsqrt(jnp.mean(hs * hs, axis=-1, keepdims=True) + RMS_EPS) * gml_ref[...]
    o_ref[...] = (y * jax.nn.sigmoid(og_ref[...])).astype(BF16)


def _mlstm(mqk, mv, mo, gt, conv_w, g_ml, nb, lp):
    blk = lambda col: pl.BlockSpec((lp, LANE), col)
    c = MLSTM_CHUNK
    nc = lp // c
    ncp = _round_up(nc, SUBLANE)
    gt4 = jnp.pad(gt.reshape(nb, N_GATES, nc, c), ((0, 0), (0, 0), (0, ncp - nc), (0, 0)))
    gate = lambda g: pl.BlockSpec((1, 1, ncp, c), lambda b, h: (b, g * ML_HEADS + h, 0, 0))
    vec = lambda rows, cols=LANE: pltpu.VMEM((rows, cols), F32)
    return pl.pallas_call(
        _mlstm_body,
        grid=(nb, ML_HEADS),
        in_specs=[
            blk(lambda b, h: (b, h)),
            blk(lambda b, h: (b, ML_HEADS + h)),
            blk(lambda b, h: (b, h)),
            pl.BlockSpec((lp, LANE), lambda b, h: (b, h), pipeline_mode=pl.Buffered(1)),
            gate(0), gate(1), gate(2), gate(3),
            pl.BlockSpec((3, LANE), lambda b, h: (0, h)),
            pl.BlockSpec((3, LANE), lambda b, h: (0, ML_HEADS + h)),
            pl.BlockSpec((1, LANE), lambda b, h: (0, 0)),
        ],
        out_specs=blk(lambda b, h: (b, h)),
        out_shape=jax.ShapeDtypeStruct((nb * lp, ML_WIDTH), BF16),
        scratch_shapes=[
            pltpu.VMEM((lp, LANE), BF16),
            pltpu.VMEM((lp, LANE), BF16),
            pltpu.VMEM((lp, 2 * LANE), BF16),
            vec(lp), vec(lp),
            vec(ncp * SUBLANE), vec(ncp * SUBLANE),
            vec(ncp), vec(ncp),
            pltpu.VMEM((2, ncp, c), F32), pltpu.VMEM((2, ncp, c), F32),
            pltpu.VMEM((2, 2, nc, c, LANE), F32),
            pltpu.VMEM((2, nc, c, 2 * LANE), F32),
            pltpu.VMEM((2, nc, ML_HEAD_DIM, 2 * LANE), F32),
        ],
        compiler_params=_cparams(2),
        name="bidir_mlstm",
    )(mqk, mqk, mv, mo, gt4, gt4, gt4, gt4, conv_w, conv_w, g_ml)


def _outproj_body(x_ref, oa_ref, hm_ref, wa_ref, wm_ref, gffn_ref, wr_ref, valid_ref,
                  x2_ref, xn_ref, affc_ref, afft_ref):
    x2 = (x_ref[...]
          + jnp.dot(oa_ref[...], wa_ref[...], preferred_element_type=F32)
          + jnp.dot(hm_ref[...], wm_ref[...], preferred_element_type=F32))
    x2_ref[...] = x2
    xn = x2 * lax.rsqrt(jnp.mean(x2 * x2, axis=-1, keepdims=True) + RMS_EPS) * gffn_ref[...]
    xn_ref[...] = xn.astype(BF16)
    xh = xn.astype(BF16)
    xl = (xn - xh.astype(F32)).astype(BF16)
    r_h = jnp.dot(xh, wr_ref[...], preferred_element_type=F32)
    logits = (r_h[:, :LANE] + r_h[:, LANE:]
              + jnp.dot(xl, wr_ref[:, :LANE], preferred_element_type=F32))
    lane = lax.broadcasted_iota(I32, logits.shape, 1)
    real = lane < N_EXPERTS
    logits = jnp.where(real, logits, -jnp.inf)
    e = jnp.exp(logits - jnp.max(logits, axis=-1, keepdims=True))
    aff = e / jnp.sum(e, axis=-1, keepdims=True)
    aff = jnp.where(real & (valid_ref[...] > 0.0), aff, -1.0)
    affc_ref[...] = aff
    afft_ref[...] = aff.T[:N_EXPERTS, :]


def _outproj(xp, oa, hm, w_a, w_m, g_ffn, w_r, valid, nb, lp):
    tm = _row_tile(lp)
    nt = lp // tm
    t_all = nb * lp
    row = lambda i: (i, 0)
    const = lambda i: (0, 0)
    return pl.pallas_call(
        _outproj_body,
        grid=(nb * nt,),
        in_specs=[
            pl.BlockSpec((tm, D_MODEL), row),
            pl.BlockSpec((tm, DA_WIDTH), row),
            pl.BlockSpec((tm, ML_WIDTH), row),
            pl.BlockSpec((DA_WIDTH, D_MODEL), const),
            pl.BlockSpec((ML_WIDTH, D_MODEL), const),
            pl.BlockSpec((1, D_MODEL), const),
            pl.BlockSpec((D_MODEL, 2 * LANE), const),
            pl.BlockSpec((tm, 1), lambda i: (i % nt, 0)),
        ],
        out_specs=[
            pl.BlockSpec((tm, D_MODEL), row),
            pl.BlockSpec((tm, D_MODEL), row),
            pl.BlockSpec((tm, LANE), row),
            pl.BlockSpec((N_EXPERTS, tm), lambda i: (0, i)),
        ],
        out_shape=[
            jax.ShapeDtypeStruct((t_all, D_MODEL), F32),
            jax.ShapeDtypeStruct((t_all, D_MODEL), BF16),
            jax.ShapeDtypeStruct((t_all, LANE), F32),
            jax.ShapeDtypeStruct((N_EXPERTS, t_all), F32),
        ],
        compiler_params=_cparams(1),
        name="outproj_router",
    )(xp, oa, hm, w_a, w_m, g_ffn, w_r, valid)


def _rope_tables(lp):
    pos = jnp.arange(lp, dtype=F32) - float(PAD_ROWS)
    inv = ROPE_THETA ** (-jnp.arange(0, ROPE_DIM, 2, dtype=F32) / ROPE_DIM)
    ang = pos[:, None] * inv[None, :]
    cos8, sin8 = jnp.cos(ang), jnp.sin(ang)
    half = ROPE_DIM // 2
    one_block = jnp.ones((lp, DA_QK_DIM - ROPE_DIM), F32)
    zero8 = jnp.zeros((lp, half), F32)
    zero_block = jnp.zeros((lp, DA_QK_DIM - ROPE_DIM), F32)
    cos_c = jnp.concatenate([cos8, cos8, one_block], axis=1)
    sa_c = jnp.concatenate([zero8, sin8, zero_block], axis=1)
    sb_c = jnp.concatenate([-sin8, zero8, zero_block], axis=1)
    tile2 = lambda a: jnp.concatenate([a, a], axis=1)
    return tile2(cos_c), tile2(sa_c), tile2(sb_c)


def _pad_tokens(x, meta):
    b = x.shape[0]
    m = jnp.broadcast_to(meta.astype(x.dtype)[None], (b, N_META, D_MODEL))
    z = jnp.zeros((b, PAD_ROWS, D_MODEL), x.dtype)
    return jnp.concatenate([z, m, x], axis=1)


def _mixer(xp, nb, lp, g_mix, w_in, b_gates, conv_w, g_q, g_k, lamv, g_da_out, g_ml_out, w_out,
           g_ffn, w_router):
    w_in_p = jnp.pad(w_in.astype(BF16), ((0, 0), (0, D_IN_PAD - D_IN)))
    bg = jnp.pad(b_gates, (0, LANE - N_GATES)).reshape(1, LANE)
    cos, sa, sb = _rope_tables(lp)
    gq = jnp.tile(g_q, 2).reshape(1, LANE)
    gk = jnp.tile(g_k, 2).reshape(1, LANE)
    qk, v, mqk, mv, mo, gt = _inproj(xp, w_in_p, g_mix.reshape(1, D_MODEL), bg, cos, sa, sb,
                                     gq, gk, nb, lp)
    bias = jnp.where(jnp.arange(FRONT) < PAD_ROWS, NEG_BIG, 0.0).astype(F32).reshape(1, FRONT)
    oa = _attention(lamv, qk, v, bias, g_da_out.reshape(1, LANE), nb, lp)
    hm = _mlstm(mqk, mv, mo, gt, conv_w, g_ml_out.reshape(1, LANE), nb, lp)
    w_o = w_out.astype(BF16)
    w_r = jnp.pad(w_router, ((0, 0), (0, LANE - N_EXPERTS)))
    w_r_hi = w_r.astype(BF16)
    w_r = jnp.concatenate([w_r_hi, (w_r - w_r_hi.astype(F32)).astype(BF16)], axis=1)
    valid = (jnp.arange(lp) >= PAD_ROWS).astype(F32).reshape(lp, 1)
    return _outproj(xp, oa, hm, w_o[:DA_WIDTH], w_o[DA_WIDTH:], g_ffn.reshape(1, D_MODEL), w_r,
                    valid, nb, lp)


TOK_TILE = LANE
WIN = 32
SUB = SUBLANE
FFN_ROWS = 512
FF_CHUNK = 256


def _route_body(aff_ref, thr_ref, cnt_ref, eqa_ref, seg_ref, *, cap, nt):
    a = aff_ref[...]
    capf = float(cap)

    def search(i, v):
        trial = v | lax.shift_left(jnp.int32(1), 30 - i)
        c = jnp.sum((a >= pltpu.bitcast(trial, F32)).astype(F32), axis=-1, keepdims=True)
        return jnp.where(c >= capf, trial, v)

    thr_bits = lax.fori_loop(0, 31, search, jnp.zeros((N_EXPERTS, 1), I32))
    thr = pltpu.bitcast(thr_bits, F32)
    gt = a > thr
    eq = a == thr
    need = capf - jnp.sum(gt.astype(F32), axis=-1, keepdims=True)
    thr_ref[...] = jnp.broadcast_to(thr, (N_EXPERTS, LANE))

    slab = LANE * TOK_TILE
    ind = (lax.broadcasted_iota(I32, (slab, LANE), 0) // TOK_TILE
           == lax.broadcasted_iota(I32, (slab, LANE), 1)).astype(BF16)
    upper = (lax.broadcasted_iota(I32, (LANE, LANE), 0)
             < lax.broadcasted_iota(I32, (LANE, LANE), 1)).astype(BF16)
    eq_base = jnp.zeros((N_EXPERTS, 1), F32)
    seg_base = jnp.zeros((N_EXPERTS, 1), F32)
    for s0 in range(0, nt, LANE):
        n_t = min(LANE, nt - s0)
        cols = slice(s0 * TOK_TILE, (s0 + n_t) * TOK_TILE)
        gt_cnt = jnp.dot(gt[:, cols].astype(BF16), ind[:n_t * TOK_TILE], preferred_element_type=F32)
        eq_cnt = jnp.dot(eq[:, cols].astype(BF16), ind[:n_t * TOK_TILE], preferred_element_type=F32)
        eq_before = eq_base + jnp.dot(eq_cnt.astype(BF16), upper, preferred_element_type=F32)
        eq_allow = jnp.clip(need - eq_before, 0.0, eq_cnt)
        cnt = gt_cnt + eq_allow
        seg = seg_base + jnp.dot(cnt.astype(BF16), upper, preferred_element_type=F32)
        eq_base = eq_base + jnp.sum(eq_cnt, axis=-1, keepdims=True)
        seg_base = seg_base + jnp.sum(cnt, axis=-1, keepdims=True)
        cnt_ref[:, s0:s0 + LANE] = cnt.astype(I32)
        eqa_ref[:, s0:s0 + LANE] = eq_allow.astype(I32)
        seg_ref[:, s0:s0 + LANE] = seg.astype(I32)


def _route(aff_t, cap):
    tg = aff_t.shape[1]
    nt = tg // TOK_TILE
    ntp = -(-nt // LANE) * LANE
    tab = jax.ShapeDtypeStruct((N_EXPERTS, ntp), I32)
    return pl.pallas_call(
        functools.partial(_route_body, cap=cap, nt=nt),
        out_shape=[jax.ShapeDtypeStruct((N_EXPERTS, LANE), F32), tab, tab, tab],
        compiler_params=pltpu.CompilerParams(vmem_limit_bytes=VMEM_LIMIT),
        name="route",
    )(aff_t)


def _window_rows(cnt, p):
    return jnp.clip(cnt - p * WIN, 0, WIN)


def _dispatch_body(seg_sm, cnt_sm, xn_ref, aff_ref, thr_ref, eqa_ref, *rest,
                   nt, ntp, base, stride, tail_start):
    xg_ref, pbuf, stage, stage_x, zbuf, sem = rest[-6:]
    j = pl.program_id(0)

    if tail_start is not None:
        @pl.when(j == 0)
        def _zero_tail():
            zbuf[...] = jnp.zeros_like(zbuf)
            copies = [pltpu.make_async_copy(
                zbuf, xg_ref.at[pl.ds((e * stride + tail_start) * SUB, zbuf.shape[0])],
                sem.at[2].at[e])
                for e in range(N_EXPERTS)]
            for c in copies:
                c.start()
            for c in copies:
                c.wait()

    a = aff_ref[...]
    thr = thr_ref[...]
    lane = lax.broadcasted_iota(I32, (N_EXPERTS, LANE), 1)
    eqa = jnp.sum(jnp.where(lane == j % LANE, eqa_ref[...], 0).astype(F32), axis=-1, keepdims=True)
    upper = (lax.broadcasted_iota(I32, (LANE, LANE), 0)
             < lax.broadcasted_iota(I32, (LANE, LANE), 1)).astype(BF16)
    gt = a > thr
    eq = a == thr
    eq_rank = jnp.dot(eq.astype(BF16), upper, preferred_element_type=F32)
    sel = gt | (eq & (eq_rank < eqa))
    rank = jnp.dot(sel.astype(BF16), upper, preferred_element_type=F32)
    rank = jnp.where(sel, rank, -1.0)

    n_pass = (functools.reduce(jnp.maximum, [cnt_sm[e * ntp + j] for e in range(N_EXPERTS)])
              + WIN - 1) // WIN
    win_row = lax.broadcasted_iota(I32, (WIN, LANE), 0).astype(F32)

    def gather_rows(p, dst):
        off = (p * WIN).astype(F32) if not isinstance(p, int) else float(p * WIN)
        for e in range(N_EXPERTS):
            r_e = jnp.broadcast_to(rank[e:e + 1, :], (WIN, LANE))
            pbuf[e * WIN:(e + 1) * WIN, :] = (r_e == win_row + off).astype(BF16)
        g = jnp.dot(pbuf[...], xn_ref[...], preferred_element_type=F32)
        for s in range(SUB):
            dst[pl.ds(s, N_EXPERTS * WIN, stride=SUB), :] = g[:, s * LANE:(s + 1) * LANE]

    def copy(t, e, p, src, sem_row):
        n = _window_rows(cnt_sm[e * ntp + t], p) * SUB
        row = pl.multiple_of((e * stride + base + seg_sm[e * ntp + t] + p * WIN) * SUB, SUB)
        return pltpu.make_async_copy(src.at[pl.ds(e * WIN * SUB, n)], xg_ref.at[pl.ds(row, n)],
                                     sem_row.at[e])

    def for_experts(t, p, fn):
        for e in range(N_EXPERTS):
            @pl.when(cnt_sm[e * ntp + t] > p * WIN)
            def _():
                fn(e)

    slot = j % 2
    mine = stage.at[slot]

    @pl.when(j >= 2)
    def _reuse():
        for_experts(j - 2, 0, lambda e: copy(j - 2, e, 0, mine, sem.at[slot]).wait())

    gather_rows(0, mine)
    for_experts(j, 0, lambda e: copy(j, e, 0, mine, sem.at[slot]).start())

    def extra_pass(p, carry):
        gather_rows(p, stage_x)
        for_experts(j, p, lambda e: copy(j, e, p, stage_x, sem.at[2]).start())
        for_experts(j, p, lambda e: copy(j, e, p, stage_x, sem.at[2]).wait())
        return carry

    lax.fori_loop(1, n_pass, extra_pass, 0)

    @pl.when(j == nt - 1)
    def _drain():
        if nt >= 2:
            for_experts(j - 1, 0, lambda e: copy(j - 1, e, 0, stage.at[1 - slot], sem.at[1 - slot]).wait())
        for_experts(j, 0, lambda e: copy(j, e, 0, mine, sem.at[slot]).wait())


def _dispatch(seg, cnt, xn, aff_t, thr, eqa, xg, *, tile0, nt, ntp, base, stride, tail_start):
    first = xg is None
    zero_tail = first and stride > tail_start
    tail_rows = (stride - tail_start) * SUB if zero_tail else SUB
    kern = functools.partial(_dispatch_body, nt=nt, ntp=ntp, base=base, stride=stride,
                             tail_start=tail_start if zero_tail else None)
    in_specs = [
        pl.BlockSpec((TOK_TILE, D_MODEL), lambda j, s, c: (tile0 + j, 0)),
        pl.BlockSpec((N_EXPERTS, TOK_TILE), lambda j, s, c: (0, tile0 + j)),
        pl.BlockSpec((N_EXPERTS, LANE), lambda j, s, c: (0, 0)),
        pl.BlockSpec((N_EXPERTS, LANE), lambda j, s, c: (0, j // LANE)),
    ]
    args = [seg, cnt, xn, aff_t, thr, eqa]
    aliases = {}
    if not first:
        in_specs.append(pl.BlockSpec(memory_space=pl.ANY))
        args.append(xg)
        aliases = {len(args) - 1: 0}
    return pl.pallas_call(
        kern,
        grid_spec=pltpu.PrefetchScalarGridSpec(
            num_scalar_prefetch=2,
            grid=(nt,),
            in_specs=in_specs,
            out_specs=pl.BlockSpec(memory_space=pl.ANY),
            scratch_shapes=[
                pltpu.VMEM((N_EXPERTS * WIN, TOK_TILE), BF16),
                pltpu.VMEM((2, N_EXPERTS * WIN * SUB, LANE), F32),
                pltpu.VMEM((N_EXPERTS * WIN * SUB, LANE), F32),
                pltpu.VMEM((tail_rows, LANE), F32),
                pltpu.SemaphoreType.DMA((3, N_EXPERTS)),
            ]),
        out_shape=jax.ShapeDtypeStruct((N_EXPERTS * stride * SUB, LANE), F32),
        input_output_aliases=aliases,
        compiler_params=pltpu.CompilerParams(dimension_semantics=("arbitrary",),
                                             vmem_limit_bytes=VMEM_LIMIT, has_side_effects=True),
        name="dispatch",
    )(*args)


def _ffn_body(xg_ref, wg_ref, wu_ref, wd_ref, wr_ref, o_ref, xb):
    e = pl.program_id(0)
    tm = xb.shape[0]
    for s in range(SUB):
        xb[:, s * LANE:(s + 1) * LANE] = xg_ref[pl.ds(s, tm, stride=SUB), :].astype(BF16)
    x = xb[...]
    logits = jnp.dot(x, wr_ref[...], preferred_element_type=F32)
    lane = lax.broadcasted_iota(I32, logits.shape, 1)
    logits = jnp.where(lane < N_EXPERTS, logits, -jnp.inf)
    ex = jnp.exp(logits - jnp.max(logits, axis=-1, keepdims=True))
    aff = ex / jnp.sum(ex, axis=-1, keepdims=True)
    gate = jnp.sum(jnp.where(lane == e, aff, 0.0), axis=-1, keepdims=True)
    acc = jnp.zeros((tm, D_MODEL), F32)
    for f in range(0, D_FF, FF_CHUNK):
        g = jnp.dot(x, wg_ref[0, :, f:f + FF_CHUNK], preferred_element_type=F32)
        u = jnp.dot(x, wu_ref[0, :, f:f + FF_CHUNK], preferred_element_type=F32)
        h = (g * jax.nn.sigmoid(g) * u).astype(BF16)
        acc = acc + jnp.dot(h, wd_ref[0, f:f + FF_CHUNK, :], preferred_element_type=F32)
    y = (acc * gate).astype(BF16).astype(F32)
    for s in range(SUB):
        o_ref[pl.ds(s, tm, stride=SUB), :] = y[:, s * LANE:(s + 1) * LANE]


def _ffn(xg, wg, wu, wd, w_r, stride):
    nr = stride // FFN_ROWS
    blk = pl.BlockSpec((FFN_ROWS * SUB, LANE), lambda e, i: (e * nr + i, 0))
    return pl.pallas_call(
        _ffn_body,
        grid=(N_EXPERTS, nr),
        in_specs=[
            blk,
            pl.BlockSpec((1, D_MODEL, D_FF), lambda e, i: (e, 0, 0)),
            pl.BlockSpec((1, D_MODEL, D_FF), lambda e, i: (e, 0, 0)),
            pl.BlockSpec((1, D_FF, D_MODEL), lambda e, i: (e, 0, 0)),
            pl.BlockSpec((D_MODEL, LANE), lambda e, i: (0, 0)),
        ],
        out_specs=blk,
        out_shape=jax.ShapeDtypeStruct(xg.shape, F32),
        scratch_shapes=[pltpu.VMEM((FFN_ROWS, D_MODEL), BF16)],
        compiler_params=_cparams(2),
        name="expert_ffn",
    )(xg, wg, wu, wd, w_r)


def _combine_body(seg_sm, cnt_sm, x2_ref, aff_ref, thr_ref, eqa_ref, ye_ref, o_ref,
                  ybuf, ybuf_x, ymat, sem, *, nt, ntp, ntl, base, stride):
    j = pl.program_id(0)
    ncol = N_EXPERTS * WIN

    def copy(t, e, p, dst, sem_row):
        n = _window_rows(cnt_sm[e * ntp + t], p) * SUB
        row = pl.multiple_of((e * stride + base + seg_sm[e * ntp + t] + p * WIN) * SUB, SUB)
        return pltpu.make_async_copy(ye_ref.at[pl.ds(row, n)], dst.at[pl.ds(e * WIN * SUB, n)],
                                     sem_row.at[e])

    def for_experts(t, p, fn):
        for e in range(N_EXPERTS):
            @pl.when(cnt_sm[e * ntp + t] > p * WIN)
            def _():
                fn(e)

    @pl.when(j == 0)
    def _init():
        ybuf[...] = jnp.zeros_like(ybuf)
        ybuf_x[...] = jnp.zeros_like(ybuf_x)

    nxt = jnp.minimum(j + 1, nt - 1)

    @pl.when((j + 1 < nt) & (nxt % ntl > 0))
    def _prefetch():
        for_experts(nxt, 0, lambda e: copy(nxt, e, 0, ybuf.at[nxt % 2], sem.at[nxt % 2]).start())

    def scatter_rows(onehot, src):
        for s in range(SUB):
            ymat[:, s * LANE:(s + 1) * LANE] = src[pl.ds(s, ncol, stride=SUB), :].astype(BF16)
        return jnp.dot(onehot, ymat[...], preferred_element_type=F32)

    @pl.when(j % ntl > 0)
    def _tile():
        a = aff_ref[...]
        thr = thr_ref[...]
        eqa = eqa_ref[0]
        lower = (lax.broadcasted_iota(I32, (TOK_TILE, TOK_TILE), 1)
                 < lax.broadcasted_iota(I32, (TOK_TILE, TOK_TILE), 0)).astype(BF16)
        gt = a > thr
        eq = a == thr
        eq_rank = jnp.dot(lower, eq.astype(BF16), preferred_element_type=F32)
        sel = gt | (eq & (eq_rank < eqa))
        rank = jnp.dot(lower, sel.astype(BF16), preferred_element_type=F32)
        rank = jnp.where(sel, rank, -1.0).astype(BF16)
        rep = (lax.broadcasted_iota(I32, (LANE, ncol), 1) // WIN
               == lax.broadcasted_iota(I32, (LANE, ncol), 0)).astype(BF16)
        rank_rep = jnp.dot(rank, rep, preferred_element_type=F32)
        col_slot = (lax.broadcasted_iota(I32, (1, ncol), 1) % WIN).astype(F32)
        n_pass = (functools.reduce(jnp.maximum, [cnt_sm[e * ntp + j] for e in range(N_EXPERTS)])
                  + WIN - 1) // WIN

        mine = ybuf.at[j % 2]
        for_experts(j, 0, lambda e: copy(j, e, 0, mine, sem.at[j % 2]).wait())
        acc = x2_ref[...] + scatter_rows((rank_rep == col_slot).astype(BF16), mine)

        def extra_pass(p, acc):
            for_experts(j, p, lambda e: copy(j, e, p, ybuf_x, sem.at[2]).start())
            onehot = (rank_rep == col_slot + (p * WIN).astype(F32)).astype(BF16)
            for_experts(j, p, lambda e: copy(j, e, p, ybuf_x, sem.at[2]).wait())
            return acc + scatter_rows(onehot, ybuf_x)

        o_ref[...] = lax.fori_loop(1, n_pass, extra_pass, acc)


def _combine(seg, cnt, x2, aff_c, thr_row, eqa_rows, ye, *, tile0, nt, ntp, ntl, base, stride):
    ncol = N_EXPERTS * WIN
    n_out = nt // ntl * (ntl - 1)
    out_map = lambda j, s, c: (j // ntl * (ntl - 1) + jnp.maximum(j % ntl - 1, 0), 0)
    return pl.pallas_call(
        functools.partial(_combine_body, nt=nt, ntp=ntp, ntl=ntl, base=base, stride=stride),
        grid_spec=pltpu.PrefetchScalarGridSpec(
            num_scalar_prefetch=2,
            grid=(nt,),
            in_specs=[
                pl.BlockSpec((TOK_TILE, D_MODEL), lambda j, s, c: (tile0 + j, 0)),
                pl.BlockSpec((TOK_TILE, LANE), lambda j, s, c: (tile0 + j, 0)),
                pl.BlockSpec((1, LANE), lambda j, s, c: (0, 0)),
                pl.BlockSpec((1, 1, LANE), lambda j, s, c: (j, 0, 0)),
                pl.BlockSpec(memory_space=pl.ANY),
            ],
            out_specs=pl.BlockSpec((TOK_TILE, D_MODEL), out_map),
            scratch_shapes=[
                pltpu.VMEM((2, ncol * SUB, LANE), F32),
                pltpu.VMEM((ncol * SUB, LANE), F32),
                pltpu.VMEM((ncol, D_MODEL), BF16),
                pltpu.SemaphoreType.DMA((3, N_EXPERTS)),
            ]),
        out_shape=jax.ShapeDtypeStruct((n_out * TOK_TILE, D_MODEL), F32),
        compiler_params=pltpu.CompilerParams(dimension_semantics=("arbitrary",),
                                             vmem_limit_bytes=VMEM_LIMIT),
        name="combine",
    )(seg, cnt, x2, aff_c, thr_row, eqa_rows, ye)


def _round_up(a, b):
    return -(-a // b) * b


def kernel(x_prompt, x_sample, meta, g_mix, w_in, b_gates, conv_w, g_q, g_k, lam_q1, lam_k1, lam_q2,
           lam_k2, g_da_out, g_ml_out, w_out, g_ffn, w_router, w_gate, w_up, w_down):
    assert x_prompt.shape[1:] == x_sample.shape[1:]
    groups = (x_prompt.shape[0], x_sample.shape[0])
    x = jnp.concatenate([x_prompt, x_sample], axis=0)
    nb, s, _ = x.shape
    lp = FRONT + s
    assert lp % LANE == 0
    ntl = lp // TOK_TILE
    xp = _pad_tokens(x, meta).reshape(nb * lp, D_MODEL)
    lamv = jnp.stack([lam_q1[0], lam_k1[0], lam_q2[0], lam_k2[0]])
    x2, xn, aff_c, aff_t = _mixer(xp, nb, lp, g_mix[0], w_in[0], b_gates[0], conv_w[0], g_q[0],
                                  g_k[0], lamv, g_da_out[0], g_ml_out[0], w_out[0], g_ffn[0],
                                  w_router[0])

    caps = [max(1, CAP_FACTOR * b * (N_META + s) // N_EXPERTS) for b in groups]
    bases = [sum(caps[:i]) for i in range(len(caps))]
    tail_start = sum(caps)
    stride = _round_up(tail_start, FFN_ROWS)
    infos = []
    tile0 = 0
    for b, cap in zip(groups, caps):
        nt = b * ntl
        thr, cnt, eqa, seg = _route(aff_t[:, tile0 * TOK_TILE:(tile0 + nt) * TOK_TILE], cap)
        ntp = cnt.shape[1]
        thr_row = jnp.pad(thr[:, 0], (0, LANE - N_EXPERTS), constant_values=2.0).reshape(1, LANE)
        eqa_rows = jnp.pad(eqa.T.astype(F32), ((0, 0), (0, LANE - N_EXPERTS))).reshape(ntp, 1, LANE)
        infos.append(dict(tile0=tile0, nt=nt, ntp=ntp, cap=cap, thr=thr, thr_row=thr_row, eqa=eqa,
                          eqa_rows=eqa_rows, seg=seg.reshape(-1), cnt=cnt.reshape(-1)))
        tile0 += nt

    xg = None
    for gi, info in enumerate(infos):
        xg = _dispatch(info["seg"], info["cnt"], xn, aff_t, info["thr"], info["eqa"], xg,
                       tile0=info["tile0"], nt=info["nt"], ntp=info["ntp"], base=bases[gi],
                       stride=stride, tail_start=tail_start)
    w_r = jnp.pad(w_router[0].astype(BF16), ((0, 0), (0, LANE - N_EXPERTS)))
    ye = _ffn(xg, w_gate[0].astype(BF16), w_up[0].astype(BF16), w_down[0].astype(BF16), w_r, stride)

    outs = []
    for gi, (b, info) in enumerate(zip(groups, infos)):
        y = _combine(info["seg"], info["cnt"], x2, aff_c, info["thr_row"], info["eqa_rows"], ye,
                     tile0=info["tile0"], nt=info["nt"], ntp=info["ntp"], ntl=ntl, base=bases[gi],
                     stride=stride)
        outs.append(y.reshape(b, s, D_MODEL))
    return tuple(outs)
```

```python
import functools
import math

import jax
import jax.numpy as jnp
from jax import lax
from jax.experimental import pallas as pl
from jax.experimental.pallas import tpu as pltpu

F32 = jnp.float32
BF16 = jnp.bfloat16
I32 = jnp.int32

D_MODEL = 1024
N_META = 16
LANE = 128
SUBLANE = 8
FRONT = LANE
PAD_ROWS = FRONT - N_META
RMS_EPS = 1e-6
DA_HEADS = 4
DA_WIDTH = 512
DA_QK_DIM = 64
ROPE_DIM = 16
ROPE_THETA = 500000.0
ML_HEADS = 4
ML_WIDTH = 512
ML_HEAD_DIM = 128
NEG_BIG = -1e30
N_GATES = 16
D_IN = 3600
D_IN_PAD = 3712
N_EXPERTS = 16
CAP_FACTOR = 2
D_FF = 2816
LAM_INIT = 0.8 - 0.6 * math.exp(-0.3 * 0)
LOG2E = math.log2(math.e)
MLSTM_CHUNK = LANE
MLSTM_UNROLL = 3
VMEM_LIMIT = 56 * 1024 * 1024


def _cparams(n_axes):
    return pltpu.CompilerParams(dimension_semantics=("arbitrary",) * n_axes,
                                vmem_limit_bytes=VMEM_LIMIT)


def _row_tile(lp):
    for t in (384, 256, 128):
        if lp % t == 0:
            return t
    raise ValueError(lp)


def _inproj_body(x_ref, w_ref, gmix_ref, bg_ref, cos_ref, sa_ref, sb_ref, gq_ref, gk_ref,
                 qk_ref, v_ref, mqk_ref, mv_ref, mo_ref, gt_ref):
    x = x_ref[...]
    h = x * lax.rsqrt(jnp.mean(x * x, axis=-1, keepdims=True) + RMS_EPS) * gmix_ref[...]
    z = jnp.dot(h.astype(BF16), w_ref[...], preferred_element_type=F32)
    lane = lax.broadcasted_iota(I32, (1, LANE), 1)
    lo = lane < DA_QK_DIM
    cos = cos_ref[...]
    sa = sa_ref[...]
    sb = sb_ref[...]

    def normrope(u, g, scale):
        sq = u * u
        s_lo = jnp.sum(jnp.where(lo, sq, 0.0), axis=-1, keepdims=True)
        s_hi = jnp.sum(jnp.where(lo, 0.0, sq), axis=-1, keepdims=True)
        ms = jnp.where(lo, s_lo, s_hi) * (1.0 / DA_QK_DIM)
        y = u * lax.rsqrt(ms + RMS_EPS) * g
        y = y * cos + pltpu.roll(y, 8, 1) * sa + pltpu.roll(y, LANE - 8, 1) * sb
        return y * scale

    for hh in range(DA_HEADS):
        c0 = hh * LANE
        qk_ref[:, c0:c0 + LANE] = normrope(z[:, c0:c0 + LANE], gq_ref[...],
                                           DA_QK_DIM ** -0.5 * LOG2E).astype(BF16)
        c1 = DA_WIDTH + hh * LANE
        qk_ref[:, c1:c1 + LANE] = normrope(z[:, c1:c1 + LANE], gk_ref[...], 1.0).astype(BF16)
    v_ref[...] = z[:, 1024:1536].astype(BF16)
    mqk_ref[...] = z[:, 1536:2560]
    mv_ref[...] = z[:, 2560:3072].astype(BF16)
    mo_ref[...] = z[:, 3072:3584]
    g = z[:, 3584:3712] + bg_ref[...]
    gt_ref[0] = g.T[:N_GATES, :]


def _inproj(xp, w_in_p, g_mix, bg, cos, sa, sb, gq, gk, nb, lp):
    tm = _row_tile(lp)
    nt = lp // tm
    t_all = nb * lp
    row = lambda i: (i, 0)
    tab = lambda i: (i % nt, 0)
    const = lambda i: (0, 0)
    return pl.pallas_call(
        _inproj_body,
        grid=(nb * nt,),
        in_specs=[
            pl.BlockSpec((tm, D_MODEL), row),
            pl.BlockSpec((D_MODEL, D_IN_PAD), const),
            pl.BlockSpec((1, D_MODEL), const),
            pl.BlockSpec((1, LANE), const),
            pl.BlockSpec((tm, LANE), tab),
            pl.BlockSpec((tm, LANE), tab),
            pl.BlockSpec((tm, LANE), tab),
            pl.BlockSpec((1, LANE), const),
            pl.BlockSpec((1, LANE), const),
        ],
        out_specs=[
            pl.BlockSpec((tm, 1024), row),
            pl.BlockSpec((tm, 512), row),
            pl.BlockSpec((tm, 1024), row),
            pl.BlockSpec((tm, 512), row),
            pl.BlockSpec((tm, 512), row),
            pl.BlockSpec((1, N_GATES, tm), lambda i: (i // nt, 0, i % nt)),
        ],
        out_shape=[
            jax.ShapeDtypeStruct((t_all, 1024), BF16),
            jax.ShapeDtypeStruct((t_all, 512), BF16),
            jax.ShapeDtypeStruct((t_all, 1024), F32),
            jax.ShapeDtypeStruct((t_all, 512), BF16),
            jax.ShapeDtypeStruct((t_all, 512), F32),
            jax.ShapeDtypeStruct((nb, N_GATES, lp), F32),
        ],
        compiler_params=_cparams(1),
        name="inproj",
    )(xp, w_in_p, g_mix, bg, cos, sa, sb, gq, gk)


def _attn_q_rows(lp):
    for t in (192, 128):
        if lp % t == 0:
            return t
    raise ValueError(lp)


def _attn_body(lam_ref, q_ref, k_ref, v_ref, bias_ref, g_ref, o_ref,
               va, s0, s1, m0, m1, e0, e1):
    lp = k_ref.shape[0]
    tq = s0.shape[1]
    nblk = lp // tq
    va[:, :LANE] = v_ref[...]
    va[:, LANE:] = jnp.where(lax.broadcasted_iota(I32, (lp, LANE), 1) == 0, 1.0, 0.0).astype(BF16)
    e1[...] = jnp.zeros_like(e1)

    lv = lam_ref[...]
    lam = (jnp.exp(jnp.sum(lv[0:1] * lv[1:2], axis=-1, keepdims=True))
           - jnp.exp(jnp.sum(lv[2:3] * lv[3:4], axis=-1, keepdims=True)) + LAM_INIT)
    first = lax.broadcasted_iota(I32, (tq, LANE), 1) < DA_QK_DIM
    nt_dims = (((1,), (1,)), ((), ()))

    def rows_of(blk):
        return pl.ds(pl.multiple_of(blk * tq, tq), tq)

    def stage_a(blk, s_ref, m_ref):
        q = q_ref[rows_of(blk), :]
        zero = jnp.zeros_like(q)
        for j, qj in enumerate((jnp.where(first, q, zero), jnp.where(first, zero, q))):
            s = lax.dot_general(qj, k_ref[...], nt_dims, preferred_element_type=F32)
            s_pad = s[:, :FRONT] + bias_ref[...]
            s_ref[j, :, :FRONT] = s_pad
            s_ref[j, :, FRONT:] = s[:, FRONT:]
            m_ref[j] = jnp.maximum(jnp.max(s_pad, axis=-1, keepdims=True),
                                   jnp.max(s[:, FRONT:], axis=-1, keepdims=True))

    def stage_b(s_ref, m_ref, e_ref):
        for j in range(2):
            e_ref[j] = jnp.exp2(s_ref[j] - m_ref[j]).astype(BF16)

    def stage_c(blk, e_ref):
        r0 = jnp.dot(e_ref[0], va[...], preferred_element_type=F32)
        r1 = jnp.dot(e_ref[1], va[...], preferred_element_type=F32)
        o = r0[:, :LANE] / r0[:, LANE:LANE + 1] - lam * (r1[:, :LANE] / r1[:, LANE:LANE + 1])
        o = o * lax.rsqrt(jnp.mean(o * o, axis=-1, keepdims=True) + RMS_EPS) * g_ref[...]
        o_ref[rows_of(blk), :] = (o * (1.0 - LAM_INIT)).astype(BF16)

    stage_a(0, s0, m0)

    def step(t, carry):
        nxt = jnp.minimum(t + 1, nblk - 1)
        prv = jnp.maximum(t - 1, 0)

        @pl.when(t % 2 == 0)
        def _even():
            stage_a(nxt, s1, m1)
            stage_b(s0, m0, e0)
            stage_c(prv, e1)

        @pl.when(t % 2 == 1)
        def _odd():
            stage_a(nxt, s0, m0)
            stage_b(s1, m1, e1)
            stage_c(prv, e0)

        return carry

    lax.fori_loop(0, nblk, step, 0)
    stage_c(nblk - 1, e0 if nblk % 2 == 1 else e1)


def _attention(lamv, qk, v, bias, g_out, nb, lp):
    tq = _attn_q_rows(lp)
    blk = lambda col: pl.BlockSpec((lp, LANE), col)
    return pl.pallas_call(
        _attn_body,
        grid=(nb, DA_HEADS),
        in_specs=[
            pl.BlockSpec((4, DA_QK_DIM), lambda b, h: (0, 0)),
            blk(lambda b, h: (b, h)),
            blk(lambda b, h: (b, DA_HEADS + h)),
            blk(lambda b, h: (b, h)),
            pl.BlockSpec((1, FRONT), lambda b, h: (0, 0)),
            pl.BlockSpec((1, LANE), lambda b, h: (0, 0)),
        ],
        out_specs=blk(lambda b, h: (b, h)),
        out_shape=jax.ShapeDtypeStruct((nb * lp, DA_WIDTH), BF16),
        scratch_shapes=[
            pltpu.VMEM((lp, 2 * LANE), BF16),
            pltpu.VMEM((2, tq, lp), F32), pltpu.VMEM((2, tq, lp), F32),
            pltpu.VMEM((2, tq, 1), F32), pltpu.VMEM((2, tq, 1), F32),
            pltpu.VMEM((2, tq, lp), BF16), pltpu.VMEM((2, tq, lp), BF16),
        ],
        compiler_params=_cparams(2),
        name="diff_attention",
    )(lamv, qk, qk, v, bias, g_out)


def _log_sigmoid(x):
    return jnp.minimum(x, 0.0) - jnp.log(1.0 + jnp.exp(-jnp.abs(x)))


def _lane_scan(x, forward, op, identity):
    lane = lax.broadcasted_iota(I32, x.shape, 1)
    sh = 1
    while sh < LANE:
        if forward:
            x = op(x, jnp.where(lane >= sh, pltpu.roll(x, sh, 1), identity))
        else:
            x = op(x, jnp.where(lane < LANE - sh, pltpu.roll(x, LANE - sh, 1), identity))
        sh *= 2
    return x


def _mlstm_body(q_ref, k_ref, v_ref, og_ref, gif_ref, gff_ref, gib_ref, gfb_ref,
                cwq_ref, cwk_ref, gml_ref, o_ref,
                qs, ks, va, hf, hb, rowv0, rowv1, uvec0, uvec1, scal0, scal1, colv, r1, kvs):
    gates = (gif_ref, gff_ref, gib_ref, gfb_ref)
    rowv, uvec, scal = (rowv0, rowv1), (uvec0, uvec1), (scal0, scal1)
    lp = q_ref.shape[0]
    nc = lp // MLSTM_CHUNK
    c = MLSTM_CHUNK
    row = lax.broadcasted_iota(I32, (lp, 1), 0)

    row_c = lax.broadcasted_iota(I32, (c, 1), 0)

    def conv_chunk(ci, carry):
        start = pl.multiple_of(ci * c, c)
        before = pl.ds(pl.multiple_of(jnp.maximum(start - SUBLANE, 0), SUBLANE), SUBLANE)
        after = pl.ds(pl.multiple_of(jnp.minimum(start + c, lp - SUBLANE), SUBLANE), SUBLANE)
        for src, w_ref, dst, scale in ((q_ref, cwq_ref, qs, ML_HEAD_DIM ** -0.5),
                                       (k_ref, cwk_ref, ks, 1.0)):
            x = src[pl.ds(start, c), :]
            w = w_ref[...]
            last_before = jnp.where(ci > 0, src[before, :][SUBLANE - 1:SUBLANE, :], 0.0)
            first_after = jnp.where(ci < nc - 1, src[after, :][0:1, :], 0.0)
            x_prev = jnp.where(row_c == 0, last_before, pltpu.roll(x, 1, 0))
            x_next = jnp.where(row_c == c - 1, first_after, pltpu.roll(x, c - 1, 0))
            y = x_prev * w[0:1] + x * w[1:2] + x_next * w[2:3]
            dst[pl.ds(start, c), :] = (y * jax.nn.sigmoid(y) * scale).astype(BF16)
        return carry

    lax.fori_loop(0, nc, conv_chunk, 0)
    va[:, :LANE] = v_ref[...]
    va[:, LANE:] = jnp.ones((lp, LANE), BF16)

    r_i = lax.broadcasted_iota(I32, (c, c), 0)
    c_i = lax.broadcasted_iota(I32, (c, c), 1)
    ncp = gates[0].shape[2]
    lane_c = lax.broadcasted_iota(I32, (ncp, c), 1)
    pos = lax.broadcasted_iota(I32, (ncp, c), 0) * c + lane_c
    valid = pos >= PAD_ROWS

    for d, forward in enumerate((True, False)):
        li = jnp.where(valid, gates[2 * d][0, 0], NEG_BIG)
        lf = jnp.where(valid, _log_sigmoid(gates[2 * d + 1][0, 0]), 0.0)
        cum = _lane_scan(lf, forward, jnp.add, 0.0)
        u = li - cum
        cm = _lane_scan(u, forward, jnp.maximum, -jnp.inf)
        b_last = cum[:, c - 1:c] if forward else cum[:, 0:1]
        a = b_last + u
        a_max = jnp.max(a, axis=-1, keepdims=True)
        rowv[d][...] = jnp.zeros_like(rowv[d])
        for r, vec in enumerate((cm, jnp.exp(a - a_max), cum)):
            rowv[d][pl.ds(r, ncp, stride=SUBLANE), :] = vec
        uvec[d][...] = u
        scal[d][0] = jnp.broadcast_to(b_last, (ncp, c))
        scal[d][1] = jnp.broadcast_to(a_max, (ncp, c))

    def rows_of(ci):
        return pl.ds(pl.multiple_of(ci * c, c), c)

    def intra(ci, d, slot, qc, kc, k_t, vc):
        forward = d == 0
        tile = rowv[d][pl.ds(pl.multiple_of(ci * SUBLANE, SUBLANE), SUBLANE), :]

        def column(r):
            return jnp.broadcast_to(tile[r:r + 1, :], (c, c)).T

        cm_c = column(0)
        cum_c = column(2)
        colv[d, 0, slot] = cm_c
        colv[d, 1, slot] = cum_c
        mask = (c_i <= r_i) if forward else (c_i >= r_i)
        w = jnp.exp(jnp.where(mask, uvec[d][pl.ds(ci, 1), :] - cm_c, -jnp.inf))
        s = lax.dot_general(qc, kc, (((1,), (1,)), ((), ())), preferred_element_type=F32) * w
        r1[d, slot] = jnp.dot(s.astype(BF16), vc, preferred_element_type=F32)
        kw_t = k_t * tile[1:2, :]
        kvs[d, slot] = jnp.dot(kw_t.astype(BF16), vc, preferred_element_type=F32)

    def inter_step(ci, d, slot, state, m_prev):
        cm_c = colv[d, 0, slot]
        cum_c = colv[d, 1, slot]
        m_loc = cum_c + cm_c
        inter = cum_c + m_prev
        m_t = jnp.maximum(m_loc, inter)
        r2 = jnp.dot(qs[rows_of(ci), :], state.astype(BF16), preferred_element_type=F32)
        w_intra = jnp.exp(m_loc - m_t)
        w_inter = jnp.exp(inter - m_t)
        r1v = r1[d, slot]
        num = w_intra * r1v[:, :LANE] + w_inter * r2[:, :LANE]
        den = w_intra * r1v[:, LANE:] + w_inter * r2[:, LANE:]
        h_out = num / jnp.maximum(jnp.abs(den), jnp.exp(-m_t))
        b_last = scal[d][0, pl.ds(ci, 1), :]
        a_max = scal[d][1, pl.ds(ci, 1), :]
        m_new = jnp.maximum(b_last + m_prev, a_max)
        keep = jnp.exp(b_last + m_prev - m_new)
        add = jnp.exp(a_max - m_new)
        state_new = (jnp.concatenate([keep, keep], axis=1) * state
                     + jnp.concatenate([add, add], axis=1) * kvs[d, slot])
        return h_out, state_new, m_new

    def intra_both(ci, carry):
        rows = rows_of(ci)
        qc = qs[rows, :]
        kc = ks[rows, :]
        vc = va[rows, :]
        k_t = kc.astype(F32).T
        intra(ci, 0, ci, qc, kc, k_t, vc)
        intra(ci, 1, ci, qc, kc, k_t, vc)
        return carry

    lax.fori_loop(0, nc, intra_both, 0, unroll=MLSTM_UNROLL)

    def body(i, carry):
        sf, mf, sb, mb = carry
        h_f, sf, mf = inter_step(i, 0, i, sf, mf)
        hf[rows_of(i), :] = h_f
        j = nc - 1 - i
        h_b, sb, mb = inter_step(j, 1, j, sb, mb)
        hb[rows_of(j), :] = h_b
        return sf, mf, sb, mb

    z_state = jnp.zeros((ML_HEAD_DIM, 2 * LANE), F32)
    z_m = jnp.zeros((1, LANE), F32)
    lax.fori_loop(0, nc, body, (z_state, z_m, z_state, z_m))

    hs = hf[...] + hb[...]
    y = hs * lax.rsqrt(jnp.mean(hs * hs, axis=-1, keepdims=True) + RMS_EPS) * gml_ref[...]
    o_ref[...] = (y * jax.nn.sigmoid(og_ref[...])).astype(BF16)


def _mlstm(mqk, mv, mo, gt, conv_w, g_ml, nb, lp):
    blk = lambda col: pl.BlockSpec((lp, LANE), col)
    c = MLSTM_CHUNK
    nc = lp // c
    ncp = _round_up(nc, SUBLANE)
    gt4 = jnp.pad(gt.reshape(nb, N_GATES, nc, c), ((0, 0), (0, 0), (0, ncp - nc), (0, 0)))
    gate = lambda g: pl.BlockSpec((1, 1, ncp, c), lambda b, h: (b, g * ML_HEADS + h, 0, 0))
    vec = lambda rows, cols=LANE: pltpu.VMEM((rows, cols), F32)
    return pl.pallas_call(
        _mlstm_body,
        grid=(nb, ML_HEADS),
        in_specs=[
            blk(lambda b, h: (b, h)),
            blk(lambda b, h: (b, ML_HEADS + h)),
            blk(lambda b, h: (b, h)),
            pl.BlockSpec((lp, LANE), lambda b, h: (b, h), pipeline_mode=pl.Buffered(1)),
            gate(0), gate(1), gate(2), gate(3),
            pl.BlockSpec((3, LANE), lambda b, h: (0, h)),
            pl.BlockSpec((3, LANE), lambda b, h: (0, ML_HEADS + h)),
            pl.BlockSpec((1, LANE), lambda b, h: (0, 0)),
        ],
        out_specs=blk(lambda b, h: (b, h)),
        out_shape=jax.ShapeDtypeStruct((nb * lp, ML_WIDTH), BF16),
        scratch_shapes=[
            pltpu.VMEM((lp, LANE), BF16),
            pltpu.VMEM((lp, LANE), BF16),
            pltpu.VMEM((lp, 2 * LANE), BF16),
            vec(lp), vec(lp),
            vec(ncp * SUBLANE), vec(ncp * SUBLANE),
            vec(ncp), vec(ncp),
            pltpu.VMEM((2, ncp, c), F32), pltpu.VMEM((2, ncp, c), F32),
            pltpu.VMEM((2, 2, nc, c, LANE), F32),
            pltpu.VMEM((2, nc, c, 2 * LANE), F32),
            pltpu.VMEM((2, nc, ML_HEAD_DIM, 2 * LANE), F32),
        ],
        compiler_params=_cparams(2),
        name="bidir_mlstm",
    )(mqk, mqk, mv, mo, gt4, gt4, gt4, gt4, conv_w, conv_w, g_ml)


def _outproj_body(x_ref, oa_ref, hm_ref, wa_ref, wm_ref, gffn_ref, wr_ref, valid_ref,
                  x2_ref, xn_ref, affc_ref, afft_ref):
    x2 = (x_ref[...]
          + jnp.dot(oa_ref[...], wa_ref[...], preferred_element_type=F32)
          + jnp.dot(hm_ref[...], wm_ref[...], preferred_element_type=F32))
    x2_ref[...] = x2
    xn = x2 * lax.rsqrt(jnp.mean(x2 * x2, axis=-1, keepdims=True) + RMS_EPS) * gffn_ref[...]
    xn_ref[...] = xn.astype(BF16)
    xh = xn.astype(BF16)
    xl = (xn - xh.astype(F32)).astype(BF16)
    r_h = jnp.dot(xh, wr_ref[...], preferred_element_type=F32)
    logits = (r_h[:, :LANE] + r_h[:, LANE:]
              + jnp.dot(xl, wr_ref[:, :LANE], preferred_element_type=F32))
    lane = lax.broadcasted_iota(I32, logits.shape, 1)
    real = lane < N_EXPERTS
    logits = jnp.where(real, logits, -jnp.inf)
    e = jnp.exp(logits - jnp.max(logits, axis=-1, keepdims=True))
    aff = e / jnp.sum(e, axis=-1, keepdims=True)
    aff = jnp.where(real & (valid_ref[...] > 0.0), aff, -1.0)
    affc_ref[...] = aff
    afft_ref[...] = aff.T[:N_EXPERTS, :]


def _outproj(xp, oa, hm, w_a, w_m, g_ffn, w_r, valid, nb, lp):
    tm = _row_tile(lp)
    nt = lp // tm
    t_all = nb * lp
    row = lambda i: (i, 0)
    const = lambda i: (0, 0)
    return pl.pallas_call(
        _outproj_body,
        grid=(nb * nt,),
        in_specs=[
            pl.BlockSpec((tm, D_MODEL), row),
            pl.BlockSpec((tm, DA_WIDTH), row),
            pl.BlockSpec((tm, ML_WIDTH), row),
            pl.BlockSpec((DA_WIDTH, D_MODEL), const),
            pl.BlockSpec((ML_WIDTH, D_MODEL), const),
            pl.BlockSpec((1, D_MODEL), const),
            pl.BlockSpec((D_MODEL, 2 * LANE), const),
            pl.BlockSpec((tm, 1), lambda i: (i % nt, 0)),
        ],
        out_specs=[
            pl.BlockSpec((tm, D_MODEL), row),
            pl.BlockSpec((tm, D_MODEL), row),
            pl.BlockSpec((tm, LANE), row),
            pl.BlockSpec((N_EXPERTS, tm), lambda i: (0, i)),
        ],
        out_shape=[
            jax.ShapeDtypeStruct((t_all, D_MODEL), F32),
            jax.ShapeDtypeStruct((t_all, D_MODEL), BF16),
            jax.ShapeDtypeStruct((t_all, LANE), F32),
            jax.ShapeDtypeStruct((N_EXPERTS, t_all), F32),
        ],
        compiler_params=_cparams(1),
        name="outproj_router",
    )(xp, oa, hm, w_a, w_m, g_ffn, w_r, valid)


def _rope_tables(lp):
    pos = jnp.arange(lp, dtype=F32) - float(PAD_ROWS)
    inv = ROPE_THETA ** (-jnp.arange(0, ROPE_DIM, 2, dtype=F32) / ROPE_DIM)
    ang = pos[:, None] * inv[None, :]
    cos8, sin8 = jnp.cos(ang), jnp.sin(ang)
    half = ROPE_DIM // 2
    one_block = jnp.ones((lp, DA_QK_DIM - ROPE_DIM), F32)
    zero8 = jnp.zeros((lp, half), F32)
    zero_block = jnp.zeros((lp, DA_QK_DIM - ROPE_DIM), F32)
    cos_c = jnp.concatenate([cos8, cos8, one_block], axis=1)
    sa_c = jnp.concatenate([zero8, sin8, zero_block], axis=1)
    sb_c = jnp.concatenate([-sin8, zero8, zero_block], axis=1)
    tile2 = lambda a: jnp.concatenate([a, a], axis=1)
    return tile2(cos_c), tile2(sa_c), tile2(sb_c)


def _pad_tokens(x, meta):
    b = x.shape[0]
    m = jnp.broadcast_to(meta.astype(x.dtype)[None], (b, N_META, D_MODEL))
    z = jnp.zeros((b, PAD_ROWS, D_MODEL), x.dtype)
    return jnp.concatenate([z, m, x], axis=1)


def _mixer(xp, nb, lp, g_mix, w_in, b_gates, conv_w, g_q, g_k, lamv, g_da_out, g_ml_out, w_out,
           g_ffn, w_router):
    w_in_p = jnp.pad(w_in.astype(BF16), ((0, 0), (0, D_IN_PAD - D_IN)))
    bg = jnp.pad(b_gates, (0, LANE - N_GATES)).reshape(1, LANE)
    cos, sa, sb = _rope_tables(lp)
    gq = jnp.tile(g_q, 2).reshape(1, LANE)
    gk = jnp.tile(g_k, 2).reshape(1, LANE)
    qk, v, mqk, mv, mo, gt = _inproj(xp, w_in_p, g_mix.reshape(1, D_MODEL), bg, cos, sa, sb,
                                     gq, gk, nb, lp)
    bias = jnp.where(jnp.arange(FRONT) < PAD_ROWS, NEG_BIG, 0.0).astype(F32).reshape(1, FRONT)
    oa = _attention(lamv, qk, v, bias, g_da_out.reshape(1, LANE), nb, lp)
    hm = _mlstm(mqk, mv, mo, gt, conv_w, g_ml_out.reshape(1, LANE), nb, lp)
    w_o = w_out.astype(BF16)
    w_r = jnp.pad(w_router, ((0, 0), (0, LANE - N_EXPERTS)))
    w_r_hi = w_r.astype(BF16)
    w_r = jnp.concatenate([w_r_hi, (w_r - w_r_hi.astype(F32)).astype(BF16)], axis=1)
    valid = (jnp.arange(lp) >= PAD_ROWS).astype(F32).reshape(lp, 1)
    return _outproj(xp, oa, hm, w_o[:DA_WIDTH], w_o[DA_WIDTH:], g_ffn.reshape(1, D_MODEL), w_r,
                    valid, nb, lp)


TOK_TILE = LANE
WIN = 32
SUB = SUBLANE
ROW_ALIGN = 16
assert WIN % ROW_ALIGN == 0
CWIN = WIN + ROW_ALIGN
FFN_ROWS = 512
FF_CHUNK = 256


def _route_body(aff_ref, thr_ref, cnt_ref, eqa_ref, seg_ref, *, cap, nt):
    a = aff_ref[...]
    capf = float(cap)

    def search(i, v):
        trial = v | lax.shift_left(jnp.int32(1), 30 - i)
        c = jnp.sum((a >= pltpu.bitcast(trial, F32)).astype(F32), axis=-1, keepdims=True)
        return jnp.where(c >= capf, trial, v)

    thr_bits = lax.fori_loop(0, 31, search, jnp.zeros((N_EXPERTS, 1), I32))
    thr = pltpu.bitcast(thr_bits, F32)
    gt = a > thr
    eq = a == thr
    need = capf - jnp.sum(gt.astype(F32), axis=-1, keepdims=True)
    thr_ref[...] = jnp.broadcast_to(thr, (N_EXPERTS, LANE))

    slab = LANE * TOK_TILE
    ind = (lax.broadcasted_iota(I32, (slab, LANE), 0) // TOK_TILE
           == lax.broadcasted_iota(I32, (slab, LANE), 1)).astype(BF16)
    upper = (lax.broadcasted_iota(I32, (LANE, LANE), 0)
             < lax.broadcasted_iota(I32, (LANE, LANE), 1)).astype(BF16)
    eq_base = jnp.zeros((N_EXPERTS, 1), F32)
    seg_base = jnp.zeros((N_EXPERTS, 1), F32)
    for s0 in range(0, nt, LANE):
        n_t = min(LANE, nt - s0)
        cols = slice(s0 * TOK_TILE, (s0 + n_t) * TOK_TILE)
        gt_cnt = jnp.dot(gt[:, cols].astype(BF16), ind[:n_t * TOK_TILE], preferred_element_type=F32)
        eq_cnt = jnp.dot(eq[:, cols].astype(BF16), ind[:n_t * TOK_TILE], preferred_element_type=F32)
        eq_before = eq_base + jnp.dot(eq_cnt.astype(BF16), upper, preferred_element_type=F32)
        eq_allow = jnp.clip(need - eq_before, 0.0, eq_cnt)
        cnt = gt_cnt + eq_allow
        seg = seg_base + jnp.dot(cnt.astype(BF16), upper, preferred_element_type=F32)
        eq_base = eq_base + jnp.sum(eq_cnt, axis=-1, keepdims=True)
        seg_base = seg_base + jnp.sum(cnt, axis=-1, keepdims=True)
        cnt_ref[:, s0:s0 + LANE] = cnt.astype(I32)
        eqa_ref[:, s0:s0 + LANE] = eq_allow.astype(I32)
        seg_ref[:, s0:s0 + LANE] = seg.astype(I32)


def _route(aff_t, cap):
    tg = aff_t.shape[1]
    nt = tg // TOK_TILE
    ntp = -(-nt // LANE) * LANE
    tab = jax.ShapeDtypeStruct((N_EXPERTS, ntp), I32)
    return pl.pallas_call(
        functools.partial(_route_body, cap=cap, nt=nt),
        out_shape=[jax.ShapeDtypeStruct((N_EXPERTS, LANE), F32), tab, tab, tab],
        compiler_params=pltpu.CompilerParams(vmem_limit_bytes=VMEM_LIMIT),
        name="route",
    )(aff_t)


def _window_rows(cnt, p):
    return jnp.clip(cnt - p * WIN, 0, WIN)


def _dispatch_body(seg_sm, cnt_sm, xn_ref, aff_ref, thr_ref, eqa_ref, *rest,
                   nt, ntp, base, stride, tail_start):
    xg_ref, pbuf, stage, stage_x, zbuf, sem = rest[-6:]
    j = pl.program_id(0)

    if tail_start is not None:
        @pl.when(j == 0)
        def _zero_tail():
            zbuf[...] = jnp.zeros_like(zbuf)
            copies = [pltpu.make_async_copy(
                zbuf, xg_ref.at[pl.ds((e * stride + tail_start) * SUB, zbuf.shape[0])],
                sem.at[2].at[e])
                for e in range(N_EXPERTS)]
            for c in copies:
                c.start()
            for c in copies:
                c.wait()

    a = aff_ref[...]
    thr = thr_ref[...]
    lane = lax.broadcasted_iota(I32, (N_EXPERTS, LANE), 1)
    eqa = jnp.sum(jnp.where(lane == j % LANE, eqa_ref[...], 0).astype(F32), axis=-1, keepdims=True)
    upper = (lax.broadcasted_iota(I32, (LANE, LANE), 0)
             < lax.broadcasted_iota(I32, (LANE, LANE), 1)).astype(BF16)
    gt = a > thr
    eq = a == thr
    eq_rank = jnp.dot(eq.astype(BF16), upper, preferred_element_type=F32)
    sel = gt | (eq & (eq_rank < eqa))
    rank = jnp.dot(sel.astype(BF16), upper, preferred_element_type=F32)
    rank = jnp.where(sel, rank, -1.0)

    n_pass = (functools.reduce(jnp.maximum, [cnt_sm[e * ntp + j] for e in range(N_EXPERTS)])
              + WIN - 1) // WIN
    win_row = lax.broadcasted_iota(I32, (WIN, LANE), 0).astype(F32)

    def gather_rows(p, dst):
        off = (p * WIN).astype(F32) if not isinstance(p, int) else float(p * WIN)
        for e in range(N_EXPERTS):
            r_e = jnp.broadcast_to(rank[e:e + 1, :], (WIN, LANE))
            pbuf[e * WIN:(e + 1) * WIN, :] = (r_e == win_row + off).astype(BF16)
        g = jnp.dot(pbuf[...], xn_ref[...], preferred_element_type=F32)
        for s in range(SUB):
            dst[pl.ds(s, N_EXPERTS * WIN, stride=SUB), :] = g[:, s * LANE:(s + 1) * LANE]

    def copy(t, e, p, src, sem_row):
        n = _window_rows(cnt_sm[e * ntp + t], p) * SUB
        row = pl.multiple_of((e * stride + base + seg_sm[e * ntp + t] + p * WIN) * SUB, SUB)
        return pltpu.make_async_copy(src.at[pl.ds(e * WIN * SUB, n)], xg_ref.at[pl.ds(row, n)],
                                     sem_row.at[e])

    def for_experts(t, p, fn):
        for e in range(N_EXPERTS):
            @pl.when(cnt_sm[e * ntp + t] > p * WIN)
            def _():
                fn(e)

    slot = j % 2
    mine = stage.at[slot]

    @pl.when(j >= 2)
    def _reuse():
        for_experts(j - 2, 0, lambda e: copy(j - 2, e, 0, mine, sem.at[slot]).wait())

    gather_rows(0, mine)
    for_experts(j, 0, lambda e: copy(j, e, 0, mine, sem.at[slot]).start())

    def extra_pass(p, carry):
        gather_rows(p, stage_x)
        for_experts(j, p, lambda e: copy(j, e, p, stage_x, sem.at[2]).start())
        for_experts(j, p, lambda e: copy(j, e, p, stage_x, sem.at[2]).wait())
        return carry

    lax.fori_loop(1, n_pass, extra_pass, 0)

    @pl.when(j == nt - 1)
    def _drain():
        if nt >= 2:
            for_experts(j - 1, 0, lambda e: copy(j - 1, e, 0, stage.at[1 - slot], sem.at[1 - slot]).wait())
        for_experts(j, 0, lambda e: copy(j, e, 0, mine, sem.at[slot]).wait())


def _dispatch(seg, cnt, xn, aff_t, thr, eqa, xg, *, tile0, nt, ntp, base, stride, tail_start):
    first = xg is None
    zero_tail = first and stride > tail_start
    tail_rows = (stride - tail_start) * SUB if zero_tail else SUB
    kern = functools.partial(_dispatch_body, nt=nt, ntp=ntp, base=base, stride=stride,
                             tail_start=tail_start if zero_tail else None)
    in_specs = [
        pl.BlockSpec((TOK_TILE, D_MODEL), lambda j, s, c: (tile0 + j, 0)),
        pl.BlockSpec((N_EXPERTS, TOK_TILE), lambda j, s, c: (0, tile0 + j)),
        pl.BlockSpec((N_EXPERTS, LANE), lambda j, s, c: (0, 0)),
        pl.BlockSpec((N_EXPERTS, LANE), lambda j, s, c: (0, j // LANE)),
    ]
    args = [seg, cnt, xn, aff_t, thr, eqa]
    aliases = {}
    if not first:
        in_specs.append(pl.BlockSpec(memory_space=pl.ANY))
        args.append(xg)
        aliases = {len(args) - 1: 0}
    return pl.pallas_call(
        kern,
        grid_spec=pltpu.PrefetchScalarGridSpec(
            num_scalar_prefetch=2,
            grid=(nt,),
            in_specs=in_specs,
            out_specs=pl.BlockSpec(memory_space=pl.ANY),
            scratch_shapes=[
                pltpu.VMEM((N_EXPERTS * WIN, TOK_TILE), BF16),
                pltpu.VMEM((2, N_EXPERTS * WIN * SUB, LANE), F32),
                pltpu.VMEM((N_EXPERTS * WIN * SUB, LANE), F32),
                pltpu.VMEM((tail_rows, LANE), F32),
                pltpu.SemaphoreType.DMA((3, N_EXPERTS)),
            ]),
        out_shape=jax.ShapeDtypeStruct((N_EXPERTS * stride * SUB, LANE), F32),
        input_output_aliases=aliases,
        compiler_params=pltpu.CompilerParams(dimension_semantics=("arbitrary",),
                                             vmem_limit_bytes=VMEM_LIMIT, has_side_effects=True),
        name="dispatch",
    )(*args)


def _ffn_body(xg_ref, wg_ref, wu_ref, wd_ref, wr_ref, o_ref, xb):
    e = pl.program_id(0)
    tm = xb.shape[0]
    for s in range(SUB):
        xb[:, s * LANE:(s + 1) * LANE] = xg_ref[pl.ds(s, tm, stride=SUB), :].astype(BF16)
    x = xb[...]
    logits = jnp.dot(x, wr_ref[...], preferred_element_type=F32)
    lane = lax.broadcasted_iota(I32, logits.shape, 1)
    logits = jnp.where(lane < N_EXPERTS, logits, -jnp.inf)
    ex = jnp.exp(logits - jnp.max(logits, axis=-1, keepdims=True))
    aff = ex / jnp.sum(ex, axis=-1, keepdims=True)
    gate = jnp.sum(jnp.where(lane == e, aff, 0.0), axis=-1, keepdims=True)
    acc = jnp.zeros((tm, D_MODEL), F32)
    for f in range(0, D_FF, FF_CHUNK):
        g = jnp.dot(x, wg_ref[0, :, f:f + FF_CHUNK], preferred_element_type=F32)
        u = jnp.dot(x, wu_ref[0, :, f:f + FF_CHUNK], preferred_element_type=F32)
        h = (g * jax.nn.sigmoid(g) * u).astype(BF16)
        acc = acc + jnp.dot(h, wd_ref[0, f:f + FF_CHUNK, :], preferred_element_type=F32)
    o_ref[...] = (acc * gate).astype(BF16)


def _ffn(xg, wg, wu, wd, w_r, stride):
    nr = stride // FFN_ROWS
    blk = pl.BlockSpec((FFN_ROWS * SUB, LANE), lambda e, i: (e * nr + i, 0))
    return pl.pallas_call(
        _ffn_body,
        grid=(N_EXPERTS, nr),
        in_specs=[
            blk,
            pl.BlockSpec((1, D_MODEL, D_FF), lambda e, i: (e, 0, 0)),
            pl.BlockSpec((1, D_MODEL, D_FF), lambda e, i: (e, 0, 0)),
            pl.BlockSpec((1, D_FF, D_MODEL), lambda e, i: (e, 0, 0)),
            pl.BlockSpec((D_MODEL, LANE), lambda e, i: (0, 0)),
        ],
        out_specs=pl.BlockSpec((FFN_ROWS, D_MODEL), lambda e, i: (e * nr + i, 0)),
        out_shape=jax.ShapeDtypeStruct((N_EXPERTS * stride, D_MODEL), BF16),
        scratch_shapes=[pltpu.VMEM((FFN_ROWS, D_MODEL), BF16)],
        compiler_params=_cparams(2),
        name="expert_ffn",
    )(xg, wg, wu, wd, w_r)


def _combine_body(seg_sm, cnt_sm, x2_ref, aff_ref, thr_ref, eqa_ref, ye_ref, o_ref,
                  ybuf, ybuf_x, sem, *, nt, ntp, ntl, base, stride):
    j = pl.program_id(0)
    ncol = N_EXPERTS * CWIN

    def first_row(t, e):
        return e * stride + base + seg_sm[e * ntp + t]

    def copy(t, e, p, dst, sem_row):
        r0 = first_row(t, e) + p * WIN
        off = r0 & (ROW_ALIGN - 1)
        n = pl.multiple_of(
            (off + _window_rows(cnt_sm[e * ntp + t], p) + ROW_ALIGN - 1) & ~(ROW_ALIGN - 1), ROW_ALIGN)
        return pltpu.make_async_copy(ye_ref.at[pl.ds(pl.multiple_of(r0 - off, ROW_ALIGN), n)],
                                     dst.at[pl.ds(e * CWIN, n)], sem_row.at[e])

    def for_experts(t, p, fn):
        for e in range(N_EXPERTS):
            @pl.when(cnt_sm[e * ntp + t] > p * WIN)
            def _():
                fn(e)

    @pl.when(j == 0)
    def _init():
        ybuf[...] = jnp.zeros_like(ybuf)
        ybuf_x[...] = jnp.zeros_like(ybuf_x)

    nxt = jnp.minimum(j + 1, nt - 1)

    @pl.when((j + 1 < nt) & (nxt % ntl > 0))
    def _prefetch():
        for_experts(nxt, 0, lambda e: copy(nxt, e, 0, ybuf.at[nxt % 2], sem.at[nxt % 2]).start())

    @pl.when(j % ntl > 0)
    def _tile():
        a = aff_ref[...]
        thr = thr_ref[...]
        eqa = eqa_ref[0]
        lower = (lax.broadcasted_iota(I32, (TOK_TILE, TOK_TILE), 1)
                 < lax.broadcasted_iota(I32, (TOK_TILE, TOK_TILE), 0)).astype(BF16)
        gt = a > thr
        eq = a == thr
        eq_rank = jnp.dot(lower, eq.astype(BF16), preferred_element_type=F32)
        sel = gt | (eq & (eq_rank < eqa))
        rank = jnp.dot(lower, sel.astype(BF16), preferred_element_type=F32)
        rank = jnp.where(sel, rank, -1.0).astype(BF16)
        col = lax.broadcasted_iota(I32, (1, ncol), 1)
        col_of = lax.broadcasted_iota(I32, (LANE, ncol), 1)
        blk_lo = lax.broadcasted_iota(I32, (LANE, ncol), 0) * CWIN
        rep = ((col_of >= blk_lo) & (col_of < blk_lo + CWIN)).astype(BF16)
        rank_rep = jnp.dot(rank, rep, preferred_element_type=F32)
        col_slot = col
        for e in range(N_EXPERTS):
            col_slot = jnp.where(col >= e * CWIN,
                                 col - (e * CWIN + (first_row(j, e) & (ROW_ALIGN - 1))), col_slot)
        col_slot = col_slot.astype(F32)
        n_pass = (functools.reduce(jnp.maximum, [cnt_sm[e * ntp + j] for e in range(N_EXPERTS)])
                  + WIN - 1) // WIN

        def onehot(p):
            lo = p * WIN if isinstance(p, int) else (p * WIN).astype(F32)
            in_pass = (rank_rep >= lo) & (rank_rep < lo + WIN)
            return (in_pass & (rank_rep - lo == col_slot)).astype(BF16)

        slot = j % 2
        for_experts(j, 0, lambda e: copy(j, e, 0, ybuf.at[slot], sem.at[slot]).wait())
        acc = x2_ref[...] + jnp.dot(onehot(0), ybuf[slot], preferred_element_type=F32)

        def extra_pass(p, acc):
            for_experts(j, p, lambda e: copy(j, e, p, ybuf_x, sem.at[2]).start())
            hot = onehot(p)
            for_experts(j, p, lambda e: copy(j, e, p, ybuf_x, sem.at[2]).wait())
            return acc + jnp.dot(hot, ybuf_x[...], preferred_element_type=F32)

        o_ref[...] = lax.fori_loop(1, n_pass, extra_pass, acc)


def _combine(seg, cnt, x2, aff_c, thr_row, eqa_rows, ye, *, tile0, nt, ntp, ntl, base, stride):
    ncol = N_EXPERTS * CWIN
    n_out = nt // ntl * (ntl - 1)
    out_map = lambda j, s, c: (j // ntl * (ntl - 1) + jnp.maximum(j % ntl - 1, 0), 0)
    return pl.pallas_call(
        functools.partial(_combine_body, nt=nt, ntp=ntp, ntl=ntl, base=base, stride=stride),
        grid_spec=pltpu.PrefetchScalarGridSpec(
            num_scalar_prefetch=2,
            grid=(nt,),
            in_specs=[
                pl.BlockSpec((TOK_TILE, D_MODEL), lambda j, s, c: (tile0 + j, 0)),
                pl.BlockSpec((TOK_TILE, LANE), lambda j, s, c: (tile0 + j, 0)),
                pl.BlockSpec((1, LANE), lambda j, s, c: (0, 0)),
                pl.BlockSpec((1, 1, LANE), lambda j, s, c: (j, 0, 0)),
                pl.BlockSpec(memory_space=pl.ANY),
            ],
            out_specs=pl.BlockSpec((TOK_TILE, D_MODEL), out_map),
            scratch_shapes=[
                pltpu.VMEM((2, ncol, D_MODEL), BF16),
                pltpu.VMEM((ncol, D_MODEL), BF16),
                pltpu.SemaphoreType.DMA((3, N_EXPERTS)),
            ]),
        out_shape=jax.ShapeDtypeStruct((n_out * TOK_TILE, D_MODEL), F32),
        compiler_params=pltpu.CompilerParams(dimension_semantics=("arbitrary",),
                                             vmem_limit_bytes=VMEM_LIMIT),
        name="combine",
    )(seg, cnt, x2, aff_c, thr_row, eqa_rows, ye)


def _round_up(a, b):
    return -(-a // b) * b


def kernel(x_prompt, x_sample, meta, g_mix, w_in, b_gates, conv_w, g_q, g_k, lam_q1, lam_k1, lam_q2,
           lam_k2, g_da_out, g_ml_out, w_out, g_ffn, w_router, w_gate, w_up, w_down):
    assert x_prompt.shape[1:] == x_sample.shape[1:]
    groups = (x_prompt.shape[0], x_sample.shape[0])
    x = jnp.concatenate([x_prompt, x_sample], axis=0)
    nb, s, _ = x.shape
    lp = FRONT + s
    assert lp % LANE == 0
    ntl = lp // TOK_TILE
    xp = _pad_tokens(x, meta).reshape(nb * lp, D_MODEL)
    lamv = jnp.stack([lam_q1[0], lam_k1[0], lam_q2[0], lam_k2[0]])
    x2, xn, aff_c, aff_t = _mixer(xp, nb, lp, g_mix[0], w_in[0], b_gates[0], conv_w[0], g_q[0],
                                  g_k[0], lamv, g_da_out[0], g_ml_out[0], w_out[0], g_ffn[0],
                                  w_router[0])

    caps = [max(1, CAP_FACTOR * b * (N_META + s) // N_EXPERTS) for b in groups]
    bases = [sum(caps[:i]) for i in range(len(caps))]
    tail_start = sum(caps)
    stride = _round_up(tail_start, FFN_ROWS)
    infos = []
    tile0 = 0
    for b, cap in zip(groups, caps):
        nt = b * ntl
        thr, cnt, eqa, seg = _route(aff_t[:, tile0 * TOK_TILE:(tile0 + nt) * TOK_TILE], cap)
        ntp = cnt.shape[1]
        thr_row = jnp.pad(thr[:, 0], (0, LANE - N_EXPERTS), constant_values=2.0).reshape(1, LANE)
        eqa_rows = jnp.pad(eqa.T.astype(F32), ((0, 0), (0, LANE - N_EXPERTS))).reshape(ntp, 1, LANE)
        infos.append(dict(tile0=tile0, nt=nt, ntp=ntp, cap=cap, thr=thr, thr_row=thr_row, eqa=eqa,
                          eqa_rows=eqa_rows, seg=seg.reshape(-1), cnt=cnt.reshape(-1)))
        tile0 += nt

    xg = None
    for gi, info in enumerate(infos):
        xg = _dispatch(info["seg"], info["cnt"], xn, aff_t, info["thr"], info["eqa"], xg,
                       tile0=info["tile0"], nt=info["nt"], ntp=info["ntp"], base=bases[gi],
                       stride=stride, tail_start=tail_start)
    w_r = jnp.pad(w_router[0].astype(BF16), ((0, 0), (0, LANE - N_EXPERTS)))
    ye = _ffn(xg, w_gate[0].astype(BF16), w_up[0].astype(BF16), w_down[0].astype(BF16), w_r, stride)

    outs = []
    for gi, (b, info) in enumerate(zip(groups, infos)):
        y = _combine(info["seg"], info["cnt"], x2, aff_c, info["thr_row"], info["eqa_rows"], ye,
                     tile0=info["tile0"], nt=info["nt"], ntp=info["ntp"], ntl=ntl, base=bases[gi],
                     stride=stride)
        outs.append(y.reshape(b, s, D_MODEL))
    return tuple(outs)
```

```python
import functools
import math

import jax
import jax.numpy as jnp
from jax import lax
from jax.experimental import pallas as pl
from jax.experimental.pallas import tpu as pltpu

F32 = jnp.float32
BF16 = jnp.bfloat16
I32 = jnp.int32

D_MODEL = 1024
N_META = 16
LANE = 128
SUBLANE = 8
FRONT = LANE
PAD_ROWS = FRONT - N_META
RMS_EPS = 1e-6
DA_HEADS = 4
DA_WIDTH = 512
DA_QK_DIM = 64
ROPE_DIM = 16
ROPE_THETA = 500000.0
ML_HEADS = 4
ML_WIDTH = 512
ML_HEAD_DIM = 128
NEG_BIG = -1e30
N_GATES = 16
D_IN = 3600
D_IN_PAD = 3712
N_EXPERTS = 16
CAP_FACTOR = 2
D_FF = 2816
LAM_INIT = 0.8 - 0.6 * math.exp(-0.3 * 0)
LOG2E = math.log2(math.e)
MLSTM_CHUNK = LANE
MLSTM_UNROLL = 11
VMEM_LIMIT = 56 * 1024 * 1024


def _cparams(n_axes):
    return pltpu.CompilerParams(dimension_semantics=("arbitrary",) * n_axes,
                                vmem_limit_bytes=VMEM_LIMIT)


def _row_tile(lp):
    for t in (384, 256, 128):
        if lp % t == 0:
            return t
    raise ValueError(lp)


def _inproj_body(*refs, nt, nb_first, k):
    front_ref = refs[0]
    x_first, x_second = refs[1:1 + k], refs[1 + k:1 + 2 * k]
    (w_ref, gmix_ref, bg_ref, cos_ref, sa_ref, sb_ref, gq_ref, gk_ref,
     qk_ref, v_ref, mqk_ref, mv_ref, mo_ref, gt_ref, xp_ref) = refs[1 + 2 * k:]
    i = pl.program_id(0)
    load = lambda blocks: tuple(r[...] for r in blocks)
    parts = lax.cond(i // nt < nb_first, lambda: load(x_first), lambda: load(x_second))
    head = jnp.where(i % nt == 0, front_ref[...], parts[0])
    x = jnp.concatenate((head,) + tuple(parts[1:]), axis=0)
    xp_ref[...] = x
    h = x * lax.rsqrt(jnp.mean(x * x, axis=-1, keepdims=True) + RMS_EPS) * gmix_ref[...]
    z = jnp.dot(h.astype(BF16), w_ref[...], preferred_element_type=F32)
    lane = lax.broadcasted_iota(I32, (1, LANE), 1)
    lo = lane < DA_QK_DIM
    cos = cos_ref[...]
    sa = sa_ref[...]
    sb = sb_ref[...]

    def normrope(u, g, scale):
        sq = u * u
        s_lo = jnp.sum(jnp.where(lo, sq, 0.0), axis=-1, keepdims=True)
        s_hi = jnp.sum(jnp.where(lo, 0.0, sq), axis=-1, keepdims=True)
        ms = jnp.where(lo, s_lo, s_hi) * (1.0 / DA_QK_DIM)
        y = u * lax.rsqrt(ms + RMS_EPS) * g
        y = y * cos + pltpu.roll(y, 8, 1) * sa + pltpu.roll(y, LANE - 8, 1) * sb
        return y * scale

    for hh in range(DA_HEADS):
        c0 = hh * LANE
        qk_ref[:, c0:c0 + LANE] = normrope(z[:, c0:c0 + LANE], gq_ref[...],
                                           DA_QK_DIM ** -0.5 * LOG2E).astype(BF16)
        c1 = DA_WIDTH + hh * LANE
        qk_ref[:, c1:c1 + LANE] = normrope(z[:, c1:c1 + LANE], gk_ref[...], 1.0).astype(BF16)
    v_ref[...] = z[:, 1024:1536].astype(BF16)
    mqk_ref[...] = z[:, 1536:2560]
    mv_ref[...] = z[:, 2560:3072].astype(BF16)
    mo_ref[...] = z[:, 3072:3584]
    g = z[:, 3584:3712] + bg_ref[...]
    gt_ref[0] = g.T[:N_GATES, :]


def _inproj(xs, front, w_in_p, g_mix, bg, cos, sa, sb, gq, gk, lp):
    tm = _row_tile(lp)
    nt = lp // tm
    k = tm // LANE
    seq_blocks = (lp - FRONT) // LANE
    counts = [x.shape[0] for x in xs]
    nb = sum(counts)
    t_all = nb * lp
    row = lambda i: (i, 0)
    tab = lambda i: (i % nt, 0)
    const = lambda i: (0, 0)

    def x_block(first_seq, n_seq, r):
        def index(i):
            b = i // nt - first_seq
            blk = jnp.clip(b * seq_blocks + k * (i % nt) - 1 + r, 0, n_seq * seq_blocks - 1)
            return (jnp.where(b < 0, 0, jnp.where(b >= n_seq, n_seq * seq_blocks - 1, blk)), 0)
        return pl.BlockSpec((LANE, D_MODEL), index)

    x_specs = [x_block(0, counts[0], r) for r in range(k)]
    x_specs += [x_block(counts[0], counts[1], r) for r in range(k)]
    x_args = [xs[0].reshape(-1, D_MODEL)] * k + [xs[1].reshape(-1, D_MODEL)] * k
    return pl.pallas_call(
        functools.partial(_inproj_body, nt=nt, nb_first=counts[0], k=k),
        grid=(nb * nt,),
        in_specs=[pl.BlockSpec((LANE, D_MODEL), const)] + x_specs + [
            pl.BlockSpec((D_MODEL, D_IN_PAD), const),
            pl.BlockSpec((1, D_MODEL), const),
            pl.BlockSpec((1, LANE), const),
            pl.BlockSpec((tm, LANE), tab),
            pl.BlockSpec((tm, LANE), tab),
            pl.BlockSpec((tm, LANE), tab),
            pl.BlockSpec((1, LANE), const),
            pl.BlockSpec((1, LANE), const),
        ],
        out_specs=[
            pl.BlockSpec((tm, 1024), row),
            pl.BlockSpec((tm, 512), row),
            pl.BlockSpec((tm, 1024), row),
            pl.BlockSpec((tm, 512), row),
            pl.BlockSpec((tm, 512), row),
            pl.BlockSpec((1, N_GATES, tm), lambda i: (i // nt, 0, i % nt)),
            pl.BlockSpec((tm, D_MODEL), row),
        ],
        out_shape=[
            jax.ShapeDtypeStruct((t_all, 1024), BF16),
            jax.ShapeDtypeStruct((t_all, 512), BF16),
            jax.ShapeDtypeStruct((t_all, 1024), F32),
            jax.ShapeDtypeStruct((t_all, 512), BF16),
            jax.ShapeDtypeStruct((t_all, 512), F32),
            jax.ShapeDtypeStruct((nb, N_GATES, lp), F32),
            jax.ShapeDtypeStruct((t_all, D_MODEL), F32),
        ],
        compiler_params=_cparams(1),
        name="inproj",
    )(front, *x_args, w_in_p, g_mix, bg, cos, sa, sb, gq, gk)


def _attn_q_rows(lp):
    for t in (192, 128):
        if lp % t == 0:
            return t
    raise ValueError(lp)


def _attn_body(lam_ref, q_ref, k_ref, v_ref, bias_ref, g_ref, o_ref,
               va, s0, s1, m0, m1, e0, e1):
    lp = k_ref.shape[0]
    tq = s0.shape[1]
    nblk = lp // tq
    va[:, :LANE] = v_ref[...]
    va[:, LANE:] = jnp.where(lax.broadcasted_iota(I32, (lp, LANE), 1) == 0, 1.0, 0.0).astype(BF16)
    e1[...] = jnp.zeros_like(e1)

    lv = lam_ref[...]
    lam = (jnp.exp(jnp.sum(lv[0:1] * lv[1:2], axis=-1, keepdims=True))
           - jnp.exp(jnp.sum(lv[2:3] * lv[3:4], axis=-1, keepdims=True)) + LAM_INIT)
    first = lax.broadcasted_iota(I32, (tq, LANE), 1) < DA_QK_DIM
    nt_dims = (((1,), (1,)), ((), ()))

    def rows_of(blk):
        return pl.ds(pl.multiple_of(blk * tq, tq), tq)

    def stage_a(blk, s_ref, m_ref):
        q = q_ref[rows_of(blk), :]
        zero = jnp.zeros_like(q)
        for j, qj in enumerate((jnp.where(first, q, zero), jnp.where(first, zero, q))):
            s = lax.dot_general(qj, k_ref[...], nt_dims, preferred_element_type=F32)
            s_pad = s[:, :FRONT] + bias_ref[...]
            s_ref[j, :, :FRONT] = s_pad
            s_ref[j, :, FRONT:] = s[:, FRONT:]
            m_ref[j] = jnp.maximum(jnp.max(s_pad, axis=-1, keepdims=True),
                                   jnp.max(s[:, FRONT:], axis=-1, keepdims=True))

    def stage_b(s_ref, m_ref, e_ref):
        for j in range(2):
            e_ref[j] = jnp.exp2(s_ref[j] - m_ref[j]).astype(BF16)

    def stage_c(blk, e_ref):
        r0 = jnp.dot(e_ref[0], va[...], preferred_element_type=F32)
        r1 = jnp.dot(e_ref[1], va[...], preferred_element_type=F32)
        o = r0[:, :LANE] / r0[:, LANE:LANE + 1] - lam * (r1[:, :LANE] / r1[:, LANE:LANE + 1])
        o = o * lax.rsqrt(jnp.mean(o * o, axis=-1, keepdims=True) + RMS_EPS) * g_ref[...]
        o_ref[rows_of(blk), :] = (o * (1.0 - LAM_INIT)).astype(BF16)

    stage_a(0, s0, m0)

    def step(t, carry):
        nxt = jnp.minimum(t + 1, nblk - 1)
        prv = jnp.maximum(t - 1, 0)

        @pl.when(t % 2 == 0)
        def _even():
            stage_a(nxt, s1, m1)
            stage_b(s0, m0, e0)
            stage_c(prv, e1)

        @pl.when(t % 2 == 1)
        def _odd():
            stage_a(nxt, s0, m0)
            stage_b(s1, m1, e1)
            stage_c(prv, e0)

        return carry

    lax.fori_loop(0, nblk, step, 0)
    stage_c(nblk - 1, e0 if nblk % 2 == 1 else e1)


def _attention(lamv, qk, v, bias, g_out, nb, lp):
    tq = _attn_q_rows(lp)
    blk = lambda col: pl.BlockSpec((lp, LANE), col)
    return pl.pallas_call(
        _attn_body,
        grid=(nb, DA_HEADS),
        in_specs=[
            pl.BlockSpec((4, DA_QK_DIM), lambda b, h: (0, 0)),
            blk(lambda b, h: (b, h)),
            blk(lambda b, h: (b, DA_HEADS + h)),
            blk(lambda b, h: (b, h)),
            pl.BlockSpec((1, FRONT), lambda b, h: (0, 0)),
            pl.BlockSpec((1, LANE), lambda b, h: (0, 0)),
        ],
        out_specs=blk(lambda b, h: (b, h)),
        out_shape=jax.ShapeDtypeStruct((nb * lp, DA_WIDTH), BF16),
        scratch_shapes=[
            pltpu.VMEM((lp, 2 * LANE), BF16),
            pltpu.VMEM((2, tq, lp), F32), pltpu.VMEM((2, tq, lp), F32),
            pltpu.VMEM((2, tq, 1), F32), pltpu.VMEM((2, tq, 1), F32),
            pltpu.VMEM((2, tq, lp), BF16), pltpu.VMEM((2, tq, lp), BF16),
        ],
        compiler_params=_cparams(2),
        name="diff_attention",
    )(lamv, qk, qk, v, bias, g_out)


def _log_sigmoid(x):
    return jnp.minimum(x, 0.0) - jnp.log(1.0 + jnp.exp(-jnp.abs(x)))


def _lane_scan(x, forward, op, identity):
    lane = lax.broadcasted_iota(I32, x.shape, 1)
    sh = 1
    while sh < LANE:
        if forward:
            x = op(x, jnp.where(lane >= sh, pltpu.roll(x, sh, 1), identity))
        else:
            x = op(x, jnp.where(lane < LANE - sh, pltpu.roll(x, LANE - sh, 1), identity))
        sh *= 2
    return x


def _mlstm_body(q_ref, k_ref, v_ref, og_ref, gif_ref, gff_ref, gib_ref, gfb_ref,
                cwq_ref, cwk_ref, gml_ref, o_ref,
                qs, ks, va, hf, hb, rowv0, rowv1, uvec0, uvec1, scal0, scal1, colv, r1, kvs):
    gates = (gif_ref, gff_ref, gib_ref, gfb_ref)
    rowv, uvec, scal = (rowv0, rowv1), (uvec0, uvec1), (scal0, scal1)
    lp = q_ref.shape[0]
    nc = lp // MLSTM_CHUNK
    c = MLSTM_CHUNK
    row = lax.broadcasted_iota(I32, (lp, 1), 0)

    row_c = lax.broadcasted_iota(I32, (c, 1), 0)

    def conv_chunk(ci, carry):
        start = pl.multiple_of(ci * c, c)
        before = pl.ds(pl.multiple_of(jnp.maximum(start - SUBLANE, 0), SUBLANE), SUBLANE)
        after = pl.ds(pl.multiple_of(jnp.minimum(start + c, lp - SUBLANE), SUBLANE), SUBLANE)
        for src, w_ref, dst, scale in ((q_ref, cwq_ref, qs, ML_HEAD_DIM ** -0.5),
                                       (k_ref, cwk_ref, ks, 1.0)):
            x = src[pl.ds(start, c), :]
            w = w_ref[...]
            last_before = jnp.where(ci > 0, src[before, :][SUBLANE - 1:SUBLANE, :], 0.0)
            first_after = jnp.where(ci < nc - 1, src[after, :][0:1, :], 0.0)
            x_prev = jnp.where(row_c == 0, last_before, pltpu.roll(x, 1, 0))
            x_next = jnp.where(row_c == c - 1, first_after, pltpu.roll(x, c - 1, 0))
            y = x_prev * w[0:1] + x * w[1:2] + x_next * w[2:3]
            dst[pl.ds(start, c), :] = (y * jax.nn.sigmoid(y) * scale).astype(BF16)
        return carry

    lax.fori_loop(0, nc, conv_chunk, 0)
    va[:, :LANE] = v_ref[...]
    va[:, LANE:] = jnp.ones((lp, LANE), BF16)

    r_i = lax.broadcasted_iota(I32, (c, c), 0)
    c_i = lax.broadcasted_iota(I32, (c, c), 1)
    ncp = gates[0].shape[2]
    lane_c = lax.broadcasted_iota(I32, (ncp, c), 1)
    pos = lax.broadcasted_iota(I32, (ncp, c), 0) * c + lane_c
    valid = pos >= PAD_ROWS

    for d, forward in enumerate((True, False)):
        li = jnp.where(valid, gates[2 * d][0, 0], NEG_BIG)
        lf = jnp.where(valid, _log_sigmoid(gates[2 * d + 1][0, 0]), 0.0)
        cum = _lane_scan(lf, forward, jnp.add, 0.0)
        u = li - cum
        cm = _lane_scan(u, forward, jnp.maximum, -jnp.inf)
        b_last = cum[:, c - 1:c] if forward else cum[:, 0:1]
        a = b_last + u
        a_max = jnp.max(a, axis=-1, keepdims=True)
        rowv[d][...] = jnp.zeros_like(rowv[d])
        for r, vec in enumerate((cm, jnp.exp(a - a_max), cum)):
            rowv[d][pl.ds(r, ncp, stride=SUBLANE), :] = vec
        uvec[d][...] = u
        scal[d][0] = jnp.broadcast_to(b_last, (ncp, c))
        scal[d][1] = jnp.broadcast_to(a_max, (ncp, c))

    def rows_of(ci):
        return pl.ds(pl.multiple_of(ci * c, c), c)

    def intra(ci, d, slot, qc, kc, k_t, vc):
        forward = d == 0
        tile = rowv[d][pl.ds(pl.multiple_of(ci * SUBLANE, SUBLANE), SUBLANE), :]

        def column(r):
            return jnp.broadcast_to(tile[r:r + 1, :], (c, c)).T

        cm_c = column(0)
        cum_c = column(2)
        colv[d, 0, slot] = cm_c
        colv[d, 1, slot] = cum_c
        mask = (c_i <= r_i) if forward else (c_i >= r_i)
        w = jnp.exp(jnp.where(mask, uvec[d][pl.ds(ci, 1), :] - cm_c, -jnp.inf))
        s = lax.dot_general(qc, kc, (((1,), (1,)), ((), ())), preferred_element_type=F32) * w
        r1[d, slot] = jnp.dot(s.astype(BF16), vc, preferred_element_type=F32)
        kw_t = k_t * tile[1:2, :]
        kvs[d, slot] = jnp.dot(kw_t.astype(BF16), vc, preferred_element_type=F32)

    def inter_step(ci, d, slot, state, m_prev):
        cm_c = colv[d, 0, slot]
        cum_c = colv[d, 1, slot]
        m_loc = cum_c + cm_c
        inter = cum_c + m_prev
        m_t = jnp.maximum(m_loc, inter)
        r2 = jnp.dot(qs[rows_of(ci), :], state.astype(BF16), preferred_element_type=F32)
        w_intra = jnp.exp(m_loc - m_t)
        w_inter = jnp.exp(inter - m_t)
        r1v = r1[d, slot]
        num = w_intra * r1v[:, :LANE] + w_inter * r2[:, :LANE]
        den = w_intra * r1v[:, LANE:] + w_inter * r2[:, LANE:]
        h_out = num / jnp.maximum(jnp.abs(den), jnp.exp(-m_t))
        b_last = scal[d][0, pl.ds(ci, 1), :]
        a_max = scal[d][1, pl.ds(ci, 1), :]
        m_new = jnp.maximum(b_last + m_prev, a_max)
        keep = jnp.exp(b_last + m_prev - m_new)
        add = jnp.exp(a_max - m_new)
        state_new = (jnp.concatenate([keep, keep], axis=1) * state
                     + jnp.concatenate([add, add], axis=1) * kvs[d, slot])
        return h_out, state_new, m_new

    def intra_both(ci, carry):
        rows = rows_of(ci)
        qc = qs[rows, :]
        kc = ks[rows, :]
        vc = va[rows, :]
        k_t = kc.astype(F32).T
        intra(ci, 0, ci, qc, kc, k_t, vc)
        intra(ci, 1, ci, qc, kc, k_t, vc)
        return carry

    lax.fori_loop(0, nc, intra_both, 0, unroll=MLSTM_UNROLL)

    def body(i, carry):
        sf, mf, sb, mb = carry
        h_f, sf, mf = inter_step(i, 0, i, sf, mf)
        hf[rows_of(i), :] = h_f
        j = nc - 1 - i
        h_b, sb, mb = inter_step(j, 1, j, sb, mb)
        hb[rows_of(j), :] = h_b
        return sf, mf, sb, mb

    z_state = jnp.zeros((ML_HEAD_DIM, 2 * LANE), F32)
    z_m = jnp.zeros((1, LANE), F32)
    lax.fori_loop(0, nc, body, (z_state, z_m, z_state, z_m))

    hs = hf[...] + hb[...]
    y = hs * lax.rsqrt(jnp.mean(hs * hs, axis=-1, keepdims=True) + RMS_EPS) * gml_ref[...]
    o_ref[...] = (y * jax.nn.sigmoid(og_ref[...])).astype(BF16)


def _mlstm(mqk, mv, mo, gt, conv_w, g_ml, nb, lp):
    blk = lambda col: pl.BlockSpec((lp, LANE), col)
    c = MLSTM_CHUNK
    nc = lp // c
    ncp = _round_up(nc, SUBLANE)
    gt4 = jnp.pad(gt.reshape(nb, N_GATES, nc, c), ((0, 0), (0, 0), (0, ncp - nc), (0, 0)))
    gate = lambda g: pl.BlockSpec((1, 1, ncp, c), lambda b, h: (b, g * ML_HEADS + h, 0, 0))
    vec = lambda rows, cols=LANE: pltpu.VMEM((rows, cols), F32)
    return pl.pallas_call(
        _mlstm_body,
        grid=(nb, ML_HEADS),
        in_specs=[
            blk(lambda b, h: (b, h)),
            blk(lambda b, h: (b, ML_HEADS + h)),
            blk(lambda b, h: (b, h)),
            pl.BlockSpec((lp, LANE), lambda b, h: (b, h), pipeline_mode=pl.Buffered(1)),
            gate(0), gate(1), gate(2), gate(3),
            pl.BlockSpec((3, LANE), lambda b, h: (0, h)),
            pl.BlockSpec((3, LANE), lambda b, h: (0, ML_HEADS + h)),
            pl.BlockSpec((1, LANE), lambda b, h: (0, 0)),
        ],
        out_specs=blk(lambda b, h: (b, h)),
        out_shape=jax.ShapeDtypeStruct((nb * lp, ML_WIDTH), BF16),
        scratch_shapes=[
            pltpu.VMEM((lp, LANE), BF16),
            pltpu.VMEM((lp, LANE), BF16),
            pltpu.VMEM((lp, 2 * LANE), BF16),
            vec(lp), vec(lp),
            vec(ncp * SUBLANE), vec(ncp * SUBLANE),
            vec(ncp), vec(ncp),
            pltpu.VMEM((2, ncp, c), F32), pltpu.VMEM((2, ncp, c), F32),
            pltpu.VMEM((2, 2, nc, c, LANE), F32),
            pltpu.VMEM((2, nc, c, 2 * LANE), F32),
            pltpu.VMEM((2, nc, ML_HEAD_DIM, 2 * LANE), F32),
        ],
        compiler_params=_cparams(2),
        name="bidir_mlstm",
    )(mqk, mqk, mv, mo, gt4, gt4, gt4, gt4, conv_w, conv_w, g_ml)


def _outproj_body(x_ref, oa_ref, hm_ref, wa_ref, wm_ref, gffn_ref, wr_ref, valid_ref,
                  x2_ref, xn_ref, affc_ref, afft_ref):
    x2 = (x_ref[...]
          + jnp.dot(oa_ref[...], wa_ref[...], preferred_element_type=F32)
          + jnp.dot(hm_ref[...], wm_ref[...], preferred_element_type=F32))
    x2_ref[...] = x2
    xn = x2 * lax.rsqrt(jnp.mean(x2 * x2, axis=-1, keepdims=True) + RMS_EPS) * gffn_ref[...]
    xn_ref[...] = xn.astype(BF16)
    xh = xn.astype(BF16)
    xl = (xn - xh.astype(F32)).astype(BF16)
    r_h = jnp.dot(xh, wr_ref[...], preferred_element_type=F32)
    logits = (r_h[:, :LANE] + r_h[:, LANE:]
              + jnp.dot(xl, wr_ref[:, :LANE], preferred_element_type=F32))
    lane = lax.broadcasted_iota(I32, logits.shape, 1)
    real = lane < N_EXPERTS
    logits = jnp.where(real, logits, -jnp.inf)
    e = jnp.exp(logits - jnp.max(logits, axis=-1, keepdims=True))
    aff = e / jnp.sum(e, axis=-1, keepdims=True)
    aff = jnp.where(real & (valid_ref[...] > 0.0), aff, -1.0)
    affc_ref[...] = aff
    afft_ref[...] = aff.T[:N_EXPERTS, :]


def _outproj(xp, oa, hm, w_a, w_m, g_ffn, w_r, valid, nb, lp):
    tm = _row_tile(lp)
    nt = lp // tm
    t_all = nb * lp
    row = lambda i: (i, 0)
    const = lambda i: (0, 0)
    return pl.pallas_call(
        _outproj_body,
        grid=(nb * nt,),
        in_specs=[
            pl.BlockSpec((tm, D_MODEL), row),
            pl.BlockSpec((tm, DA_WIDTH), row),
            pl.BlockSpec((tm, ML_WIDTH), row),
            pl.BlockSpec((DA_WIDTH, D_MODEL), const),
            pl.BlockSpec((ML_WIDTH, D_MODEL), const),
            pl.BlockSpec((1, D_MODEL), const),
            pl.BlockSpec((D_MODEL, 2 * LANE), const),
            pl.BlockSpec((tm, 1), lambda i: (i % nt, 0)),
        ],
        out_specs=[
            pl.BlockSpec((tm, D_MODEL), row),
            pl.BlockSpec((tm, D_MODEL), row),
            pl.BlockSpec((tm, LANE), row),
            pl.BlockSpec((N_EXPERTS, tm), lambda i: (0, i)),
        ],
        out_shape=[
            jax.ShapeDtypeStruct((t_all, D_MODEL), F32),
            jax.ShapeDtypeStruct((t_all, D_MODEL), BF16),
            jax.ShapeDtypeStruct((t_all, LANE), F32),
            jax.ShapeDtypeStruct((N_EXPERTS, t_all), F32),
        ],
        compiler_params=_cparams(1),
        name="outproj_router",
    )(xp, oa, hm, w_a, w_m, g_ffn, w_r, valid)


def _rope_tables(lp):
    pos = jnp.arange(lp, dtype=F32) - float(PAD_ROWS)
    inv = ROPE_THETA ** (-jnp.arange(0, ROPE_DIM, 2, dtype=F32) / ROPE_DIM)
    ang = pos[:, None] * inv[None, :]
    cos8, sin8 = jnp.cos(ang), jnp.sin(ang)
    half = ROPE_DIM // 2
    one_block = jnp.ones((lp, DA_QK_DIM - ROPE_DIM), F32)
    zero8 = jnp.zeros((lp, half), F32)
    zero_block = jnp.zeros((lp, DA_QK_DIM - ROPE_DIM), F32)
    cos_c = jnp.concatenate([cos8, cos8, one_block], axis=1)
    sa_c = jnp.concatenate([zero8, sin8, zero_block], axis=1)
    sb_c = jnp.concatenate([-sin8, zero8, zero_block], axis=1)
    tile2 = lambda a: jnp.concatenate([a, a], axis=1)
    return tile2(cos_c), tile2(sa_c), tile2(sb_c)


def _mixer(xs, meta, lp, g_mix, w_in, b_gates, conv_w, g_q, g_k, lamv, g_da_out, g_ml_out, w_out,
           g_ffn, w_router):
    nb = sum(x.shape[0] for x in xs)
    w_in_p = jnp.pad(w_in.astype(BF16), ((0, 0), (0, D_IN_PAD - D_IN)))
    bg = jnp.pad(b_gates, (0, LANE - N_GATES)).reshape(1, LANE)
    cos, sa, sb = _rope_tables(lp)
    gq = jnp.tile(g_q, 2).reshape(1, LANE)
    gk = jnp.tile(g_k, 2).reshape(1, LANE)
    front = jnp.pad(meta.astype(F32), ((PAD_ROWS, 0), (0, 0)))
    qk, v, mqk, mv, mo, gt, xp = _inproj(xs, front, w_in_p, g_mix.reshape(1, D_MODEL), bg, cos,
                                         sa, sb, gq, gk, lp)
    bias = jnp.where(jnp.arange(FRONT) < PAD_ROWS, NEG_BIG, 0.0).astype(F32).reshape(1, FRONT)
    oa = _attention(lamv, qk, v, bias, g_da_out.reshape(1, LANE), nb, lp)
    hm = _mlstm(mqk, mv, mo, gt, conv_w, g_ml_out.reshape(1, LANE), nb, lp)
    w_o = w_out.astype(BF16)
    w_r = jnp.pad(w_router, ((0, 0), (0, LANE - N_EXPERTS)))
    w_r_hi = w_r.astype(BF16)
    w_r = jnp.concatenate([w_r_hi, (w_r - w_r_hi.astype(F32)).astype(BF16)], axis=1)
    valid = (jnp.arange(lp) >= PAD_ROWS).astype(F32).reshape(lp, 1)
    return _outproj(xp, oa, hm, w_o[:DA_WIDTH], w_o[DA_WIDTH:], g_ffn.reshape(1, D_MODEL), w_r,
                    valid, nb, lp)


TOK_TILE = LANE
WIN = 32
SUB = SUBLANE
ROW_ALIGN = 16
assert WIN % ROW_ALIGN == 0
CWIN = WIN + ROW_ALIGN
FFN_ROWS = 512
FF_CHUNK = 256


def _route_body(aff_ref, thr_ref, cnt_ref, eqa_ref, seg_ref, *, cap, nt):
    a = aff_ref[...]
    capf = float(cap)

    def search(i, v):
        trial = v | lax.shift_left(jnp.int32(1), 30 - i)
        c = jnp.sum((a >= pltpu.bitcast(trial, F32)).astype(F32), axis=-1, keepdims=True)
        return jnp.where(c >= capf, trial, v)

    thr_bits = lax.fori_loop(0, 31, search, jnp.zeros((N_EXPERTS, 1), I32))
    thr = pltpu.bitcast(thr_bits, F32)
    gt = a > thr
    eq = a == thr
    need = capf - jnp.sum(gt.astype(F32), axis=-1, keepdims=True)
    thr_ref[...] = jnp.broadcast_to(thr, (N_EXPERTS, LANE))

    slab = LANE * TOK_TILE
    ind = (lax.broadcasted_iota(I32, (slab, LANE), 0) // TOK_TILE
           == lax.broadcasted_iota(I32, (slab, LANE), 1)).astype(BF16)
    upper = (lax.broadcasted_iota(I32, (LANE, LANE), 0)
             < lax.broadcasted_iota(I32, (LANE, LANE), 1)).astype(BF16)
    eq_base = jnp.zeros((N_EXPERTS, 1), F32)
    seg_base = jnp.zeros((N_EXPERTS, 1), F32)
    for s0 in range(0, nt, LANE):
        n_t = min(LANE, nt - s0)
        cols = slice(s0 * TOK_TILE, (s0 + n_t) * TOK_TILE)
        gt_cnt = jnp.dot(gt[:, cols].astype(BF16), ind[:n_t * TOK_TILE], preferred_element_type=F32)
        eq_cnt = jnp.dot(eq[:, cols].astype(BF16), ind[:n_t * TOK_TILE], preferred_element_type=F32)
        eq_before = eq_base + jnp.dot(eq_cnt.astype(BF16), upper, preferred_element_type=F32)
        eq_allow = jnp.clip(need - eq_before, 0.0, eq_cnt)
        cnt = gt_cnt + eq_allow
        seg = seg_base + jnp.dot(cnt.astype(BF16), upper, preferred_element_type=F32)
        eq_base = eq_base + jnp.sum(eq_cnt, axis=-1, keepdims=True)
        seg_base = seg_base + jnp.sum(cnt, axis=-1, keepdims=True)
        cnt_ref[:, s0:s0 + LANE] = cnt.astype(I32)
        eqa_ref[:, s0:s0 + LANE] = eq_allow.astype(I32)
        seg_ref[:, s0:s0 + LANE] = seg.astype(I32)


def _route(aff_t, cap):
    tg = aff_t.shape[1]
    nt = tg // TOK_TILE
    ntp = -(-nt // LANE) * LANE
    tab = jax.ShapeDtypeStruct((N_EXPERTS, ntp), I32)
    return pl.pallas_call(
        functools.partial(_route_body, cap=cap, nt=nt),
        out_shape=[jax.ShapeDtypeStruct((N_EXPERTS, LANE), F32), tab, tab, tab],
        compiler_params=pltpu.CompilerParams(vmem_limit_bytes=VMEM_LIMIT),
        name="route",
    )(aff_t)


def _window_rows(cnt, p):
    return jnp.clip(cnt - p * WIN, 0, WIN)


def _dispatch_body(seg_sm, cnt_sm, xn_ref, aff_ref, thr_ref, eqa_ref, *rest,
                   nt, ntp, base, stride, tail_start):
    xg_ref, pbuf, stage, stage_x, zbuf, sem = rest[-6:]
    j = pl.program_id(0)

    if tail_start is not None:
        @pl.when(j == 0)
        def _zero_tail():
            zbuf[...] = jnp.zeros_like(zbuf)
            copies = [pltpu.make_async_copy(
                zbuf, xg_ref.at[pl.ds((e * stride + tail_start) * SUB, zbuf.shape[0])],
                sem.at[2].at[e])
                for e in range(N_EXPERTS)]
            for c in copies:
                c.start()
            for c in copies:
                c.wait()

    a = aff_ref[...]
    thr = thr_ref[...]
    lane = lax.broadcasted_iota(I32, (N_EXPERTS, LANE), 1)
    eqa = jnp.sum(jnp.where(lane == j % LANE, eqa_ref[...], 0).astype(F32), axis=-1, keepdims=True)
    upper = (lax.broadcasted_iota(I32, (LANE, LANE), 0)
             < lax.broadcasted_iota(I32, (LANE, LANE), 1)).astype(BF16)
    gt = a > thr
    eq = a == thr
    eq_rank = jnp.dot(eq.astype(BF16), upper, preferred_element_type=F32)
    sel = gt | (eq & (eq_rank < eqa))
    rank = jnp.dot(sel.astype(BF16), upper, preferred_element_type=F32)
    rank = jnp.where(sel, rank, -1.0)

    n_pass = (functools.reduce(jnp.maximum, [cnt_sm[e * ntp + j] for e in range(N_EXPERTS)])
              + WIN - 1) // WIN
    win_row = lax.broadcasted_iota(I32, (WIN, LANE), 0).astype(F32)

    def gather_rows(p, dst):
        off = (p * WIN).astype(F32) if not isinstance(p, int) else float(p * WIN)
        for e in range(N_EXPERTS):
            r_e = jnp.broadcast_to(rank[e:e + 1, :], (WIN, LANE))
            pbuf[e * WIN:(e + 1) * WIN, :] = (r_e == win_row + off).astype(BF16)
        g = jnp.dot(pbuf[...], xn_ref[...], preferred_element_type=F32)
        for s in range(SUB):
            dst[pl.ds(s, N_EXPERTS * WIN, stride=SUB), :] = g[:, s * LANE:(s + 1) * LANE]

    def copy(t, e, p, src, sem_row):
        n = _window_rows(cnt_sm[e * ntp + t], p) * SUB
        row = pl.multiple_of((e * stride + base + seg_sm[e * ntp + t] + p * WIN) * SUB, SUB)
        return pltpu.make_async_copy(src.at[pl.ds(e * WIN * SUB, n)], xg_ref.at[pl.ds(row, n)],
                                     sem_row.at[e])

    def for_experts(t, p, fn):
        for e in range(N_EXPERTS):
            @pl.when(cnt_sm[e * ntp + t] > p * WIN)
            def _():
                fn(e)

    slot = j % 2
    mine = stage.at[slot]

    @pl.when(j >= 2)
    def _reuse():
        for_experts(j - 2, 0, lambda e: copy(j - 2, e, 0, mine, sem.at[slot]).wait())

    gather_rows(0, mine)
    for_experts(j, 0, lambda e: copy(j, e, 0, mine, sem.at[slot]).start())

    def extra_pass(p, carry):
        gather_rows(p, stage_x)
        for_experts(j, p, lambda e: copy(j, e, p, stage_x, sem.at[2]).start())
        for_experts(j, p, lambda e: copy(j, e, p, stage_x, sem.at[2]).wait())
        return carry

    lax.fori_loop(1, n_pass, extra_pass, 0)

    @pl.when(j == nt - 1)
    def _drain():
        if nt >= 2:
            for_experts(j - 1, 0, lambda e: copy(j - 1, e, 0, stage.at[1 - slot], sem.at[1 - slot]).wait())
        for_experts(j, 0, lambda e: copy(j, e, 0, mine, sem.at[slot]).wait())


def _dispatch(seg, cnt, xn, aff_t, thr, eqa, xg, *, tile0, nt, ntp, base, stride, tail_start):
    first = xg is None
    zero_tail = first and stride > tail_start
    tail_rows = (stride - tail_start) * SUB if zero_tail else SUB
    kern = functools.partial(_dispatch_body, nt=nt, ntp=ntp, base=base, stride=stride,
                             tail_start=tail_start if zero_tail else None)
    in_specs = [
        pl.BlockSpec((TOK_TILE, D_MODEL), lambda j, s, c: (tile0 + j, 0)),
        pl.BlockSpec((N_EXPERTS, TOK_TILE), lambda j, s, c: (0, tile0 + j)),
        pl.BlockSpec((N_EXPERTS, LANE), lambda j, s, c: (0, 0)),
        pl.BlockSpec((N_EXPERTS, LANE), lambda j, s, c: (0, j // LANE)),
    ]
    args = [seg, cnt, xn, aff_t, thr, eqa]
    aliases = {}
    if not first:
        in_specs.append(pl.BlockSpec(memory_space=pl.ANY))
        args.append(xg)
        aliases = {len(args) - 1: 0}
    return pl.pallas_call(
        kern,
        grid_spec=pltpu.PrefetchScalarGridSpec(
            num_scalar_prefetch=2,
            grid=(nt,),
            in_specs=in_specs,
            out_specs=pl.BlockSpec(memory_space=pl.ANY),
            scratch_shapes=[
                pltpu.VMEM((N_EXPERTS * WIN, TOK_TILE), BF16),
                pltpu.VMEM((2, N_EXPERTS * WIN * SUB, LANE), F32),
                pltpu.VMEM((N_EXPERTS * WIN * SUB, LANE), F32),
                pltpu.VMEM((tail_rows, LANE), F32),
                pltpu.SemaphoreType.DMA((3, N_EXPERTS)),
            ]),
        out_shape=jax.ShapeDtypeStruct((N_EXPERTS * stride * SUB, LANE), F32),
        input_output_aliases=aliases,
        compiler_params=pltpu.CompilerParams(dimension_semantics=("arbitrary",),
                                             vmem_limit_bytes=VMEM_LIMIT, has_side_effects=True),
        name="dispatch",
    )(*args)


def _ffn_body(xg_ref, wg_ref, wu_ref, wd_ref, wr_ref, o_ref, xb):
    e = pl.program_id(0)
    tm = xb.shape[0]
    for s in range(SUB):
        xb[:, s * LANE:(s + 1) * LANE] = xg_ref[pl.ds(s, tm, stride=SUB), :].astype(BF16)
    x = xb[...]
    logits = jnp.dot(x, wr_ref[...], preferred_element_type=F32)
    lane = lax.broadcasted_iota(I32, logits.shape, 1)
    logits = jnp.where(lane < N_EXPERTS, logits, -jnp.inf)
    ex = jnp.exp(logits - jnp.max(logits, axis=-1, keepdims=True))
    aff = ex / jnp.sum(ex, axis=-1, keepdims=True)
    gate = jnp.sum(jnp.where(lane == e, aff, 0.0), axis=-1, keepdims=True)
    acc = jnp.zeros((tm, D_MODEL), F32)
    for f in range(0, D_FF, FF_CHUNK):
        g = jnp.dot(x, wg_ref[0, :, f:f + FF_CHUNK], preferred_element_type=F32)
        u = jnp.dot(x, wu_ref[0, :, f:f + FF_CHUNK], preferred_element_type=F32)
        h = (g * jax.nn.sigmoid(g) * u).astype(BF16)
        acc = acc + jnp.dot(h, wd_ref[0, f:f + FF_CHUNK, :], preferred_element_type=F32)
    o_ref[...] = (acc * gate).astype(BF16)


def _ffn(xg, wg, wu, wd, w_r, stride):
    nr = stride // FFN_ROWS
    blk = pl.BlockSpec((FFN_ROWS * SUB, LANE), lambda e, i: (e * nr + i, 0))
    return pl.pallas_call(
        _ffn_body,
        grid=(N_EXPERTS, nr),
        in_specs=[
            blk,
            pl.BlockSpec((1, D_MODEL, D_FF), lambda e, i: (e, 0, 0)),
            pl.BlockSpec((1, D_MODEL, D_FF), lambda e, i: (e, 0, 0)),
            pl.BlockSpec((1, D_FF, D_MODEL), lambda e, i: (e, 0, 0)),
            pl.BlockSpec((D_MODEL, LANE), lambda e, i: (0, 0)),
        ],
        out_specs=pl.BlockSpec((FFN_ROWS, D_MODEL), lambda e, i: (e * nr + i, 0)),
        out_shape=jax.ShapeDtypeStruct((N_EXPERTS * stride, D_MODEL), BF16),
        scratch_shapes=[pltpu.VMEM((FFN_ROWS, D_MODEL), BF16)],
        compiler_params=_cparams(2),
        name="expert_ffn",
    )(xg, wg, wu, wd, w_r)


def _combine_body(seg_sm, cnt_sm, x2_ref, aff_ref, thr_ref, eqa_ref, ye_ref, o_ref,
                  ybuf, ybuf_x, rep, sem, *, nt, ntp, ntl, base, stride):
    j = pl.program_id(0)
    ncol = N_EXPERTS * CWIN

    def first_row(t, e):
        return e * stride + base + seg_sm[e * ntp + t]

    def copy(t, e, p, dst, sem_row):
        r0 = first_row(t, e) + p * WIN
        off = r0 & (ROW_ALIGN - 1)
        n = pl.multiple_of(
            (off + _window_rows(cnt_sm[e * ntp + t], p) + ROW_ALIGN - 1) & ~(ROW_ALIGN - 1), ROW_ALIGN)
        return pltpu.make_async_copy(ye_ref.at[pl.ds(pl.multiple_of(r0 - off, ROW_ALIGN), n)],
                                     dst.at[pl.ds(e * CWIN, n)], sem_row.at[e])

    def for_experts(t, p, fn):
        for e in range(N_EXPERTS):
            @pl.when(cnt_sm[e * ntp + t] > p * WIN)
            def _():
                fn(e)

    @pl.when(j == 0)
    def _init():
        ybuf[...] = jnp.zeros_like(ybuf)
        ybuf_x[...] = jnp.zeros_like(ybuf_x)
        col_of = lax.broadcasted_iota(I32, (LANE, ncol), 1)
        blk_lo = lax.broadcasted_iota(I32, (LANE, ncol), 0) * CWIN
        rep[...] = ((col_of >= blk_lo) & (col_of < blk_lo + CWIN)).astype(BF16)

    nxt = jnp.minimum(j + 1, nt - 1)

    @pl.when((j + 1 < nt) & (nxt % ntl > 0))
    def _prefetch():
        for_experts(nxt, 0, lambda e: copy(nxt, e, 0, ybuf.at[nxt % 2], sem.at[nxt % 2]).start())

    @pl.when(j % ntl > 0)
    def _tile():
        a = aff_ref[...]
        thr = thr_ref[...]
        eqa = eqa_ref[0]
        lower = (lax.broadcasted_iota(I32, (TOK_TILE, TOK_TILE), 1)
                 < lax.broadcasted_iota(I32, (TOK_TILE, TOK_TILE), 0)).astype(BF16)
        gt = a > thr
        eq = a == thr
        eq_rank = jnp.dot(lower, eq.astype(BF16), preferred_element_type=F32)
        sel = gt | (eq & (eq_rank < eqa))
        rank = jnp.dot(lower, sel.astype(BF16), preferred_element_type=F32)
        rank = jnp.where(sel, rank, -1.0).astype(BF16)
        col = lax.broadcasted_iota(I32, (1, ncol), 1)
        rank_rep = jnp.dot(rank, rep[...], preferred_element_type=F32)
        col_slot = col
        for e in range(N_EXPERTS):
            col_slot = jnp.where(col >= e * CWIN,
                                 col - (e * CWIN + (first_row(j, e) & (ROW_ALIGN - 1))), col_slot)
        col_slot = col_slot.astype(F32)
        n_pass = (functools.reduce(jnp.maximum, [cnt_sm[e * ntp + j] for e in range(N_EXPERTS)])
                  + WIN - 1) // WIN

        def onehot(p):
            lo = p * WIN if isinstance(p, int) else (p * WIN).astype(F32)
            in_pass = (rank_rep >= lo) & (rank_rep < lo + WIN)
            return (in_pass & (rank_rep - lo == col_slot)).astype(BF16)

        slot = j % 2
        for_experts(j, 0, lambda e: copy(j, e, 0, ybuf.at[slot], sem.at[slot]).wait())
        acc = x2_ref[...] + jnp.dot(onehot(0), ybuf[slot], preferred_element_type=F32)

        def extra_pass(p, acc):
            for_experts(j, p, lambda e: copy(j, e, p, ybuf_x, sem.at[2]).start())
            hot = onehot(p)
            for_experts(j, p, lambda e: copy(j, e, p, ybuf_x, sem.at[2]).wait())
            return acc + jnp.dot(hot, ybuf_x[...], preferred_element_type=F32)

        o_ref[...] = lax.fori_loop(1, n_pass, extra_pass, acc)


def _combine(seg, cnt, x2, aff_c, thr_row, eqa_rows, ye, *, tile0, nt, ntp, ntl, base, stride):
    ncol = N_EXPERTS * CWIN
    n_out = nt // ntl * (ntl - 1)
    out_map = lambda j, s, c: (j // ntl * (ntl - 1) + jnp.maximum(j % ntl - 1, 0), 0)
    return pl.pallas_call(
        functools.partial(_combine_body, nt=nt, ntp=ntp, ntl=ntl, base=base, stride=stride),
        grid_spec=pltpu.PrefetchScalarGridSpec(
            num_scalar_prefetch=2,
            grid=(nt,),
            in_specs=[
                pl.BlockSpec((TOK_TILE, D_MODEL), lambda j, s, c: (tile0 + j, 0)),
                pl.BlockSpec((TOK_TILE, LANE), lambda j, s, c: (tile0 + j, 0)),
                pl.BlockSpec((1, LANE), lambda j, s, c: (0, 0)),
                pl.BlockSpec((1, 1, LANE), lambda j, s, c: (j, 0, 0)),
                pl.BlockSpec(memory_space=pl.ANY),
            ],
            out_specs=pl.BlockSpec((TOK_TILE, D_MODEL), out_map),
            scratch_shapes=[
                pltpu.VMEM((2, ncol, D_MODEL), BF16),
                pltpu.VMEM((ncol, D_MODEL), BF16),
                pltpu.VMEM((LANE, ncol), BF16),
                pltpu.SemaphoreType.DMA((3, N_EXPERTS)),
            ]),
        out_shape=jax.ShapeDtypeStruct((n_out * TOK_TILE, D_MODEL), F32),
        compiler_params=pltpu.CompilerParams(dimension_semantics=("arbitrary",),
                                             vmem_limit_bytes=VMEM_LIMIT),
        name="combine",
    )(seg, cnt, x2, aff_c, thr_row, eqa_rows, ye)


def _round_up(a, b):
    return -(-a // b) * b


def kernel(x_prompt, x_sample, meta, g_mix, w_in, b_gates, conv_w, g_q, g_k, lam_q1, lam_k1, lam_q2,
           lam_k2, g_da_out, g_ml_out, w_out, g_ffn, w_router, w_gate, w_up, w_down):
    assert x_prompt.shape[1:] == x_sample.shape[1:]
    groups = (x_prompt.shape[0], x_sample.shape[0])
    s = x_prompt.shape[1]
    lp = FRONT + s
    assert s % LANE == 0
    ntl = lp // TOK_TILE
    lamv = jnp.stack([lam_q1[0], lam_k1[0], lam_q2[0], lam_k2[0]])
    x2, xn, aff_c, aff_t = _mixer((x_prompt, x_sample), meta, lp, g_mix[0], w_in[0], b_gates[0],
                                  conv_w[0], g_q[0], g_k[0], lamv, g_da_out[0], g_ml_out[0],
                                  w_out[0], g_ffn[0], w_router[0])

    caps = [max(1, CAP_FACTOR * b * (N_META + s) // N_EXPERTS) for b in groups]
    bases = [sum(caps[:i]) for i in range(len(caps))]
    tail_start = sum(caps)
    stride = _round_up(tail_start, FFN_ROWS)
    infos = []
    tile0 = 0
    for b, cap in zip(groups, caps):
        nt = b * ntl
        thr, cnt, eqa, seg = _route(aff_t[:, tile0 * TOK_TILE:(tile0 + nt) * TOK_TILE], cap)
        ntp = cnt.shape[1]
        thr_row = jnp.pad(thr[:, 0], (0, LANE - N_EXPERTS), constant_values=2.0).reshape(1, LANE)
        eqa_rows = jnp.pad(eqa.T.astype(F32), ((0, 0), (0, LANE - N_EXPERTS))).reshape(ntp, 1, LANE)
        infos.append(dict(tile0=tile0, nt=nt, ntp=ntp, cap=cap, thr=thr, thr_row=thr_row, eqa=eqa,
                          eqa_rows=eqa_rows, seg=seg.reshape(-1), cnt=cnt.reshape(-1)))
        tile0 += nt

    xg = None
    for gi, info in enumerate(infos):
        xg = _dispatch(info["seg"], info["cnt"], xn, aff_t, info["thr"], info["eqa"], xg,
                       tile0=info["tile0"], nt=info["nt"], ntp=info["ntp"], base=bases[gi],
                       stride=stride, tail_start=tail_start)
    w_r = jnp.pad(w_router[0].astype(BF16), ((0, 0), (0, LANE - N_EXPERTS)))
    ye = _ffn(xg, w_gate[0].astype(BF16), w_up[0].astype(BF16), w_down[0].astype(BF16), w_r, stride)

    outs = []
    for gi, (b, info) in enumerate(zip(groups, infos)):
        y = _combine(info["seg"], info["cnt"], x2, aff_c, info["thr_row"], info["eqa_rows"], ye,
                     tile0=info["tile0"], nt=info["nt"], ntp=info["ntp"], ntl=ntl, base=bases[gi],
                     stride=stride)
        outs.append(y.reshape(b, s, D_MODEL))
    return tuple(outs)
```

```python
import functools
import math

import jax
import jax.numpy as jnp
from jax import lax
from jax.experimental import pallas as pl
from jax.experimental.pallas import tpu as pltpu

F32 = jnp.float32
BF16 = jnp.bfloat16
I32 = jnp.int32

D_MODEL = 1024
N_META = 16
LANE = 128
SUBLANE = 8
FRONT = LANE
PAD_ROWS = FRONT - N_META
RMS_EPS = 1e-6
DA_HEADS = 4
DA_WIDTH = 512
DA_QK_DIM = 64
ROPE_DIM = 16
ROPE_THETA = 500000.0
ML_HEADS = 4
ML_WIDTH = 512
ML_HEAD_DIM = 128
NEG_BIG = -1e30
N_GATES = 16
D_IN = 3600
D_IN_PAD = 3712
N_EXPERTS = 16
CAP_FACTOR = 2
D_FF = 2816
LAM_INIT = 0.8 - 0.6 * math.exp(-0.3 * 0)
LOG2E = math.log2(math.e)
MLSTM_CHUNK = LANE
MLSTM_UNROLL = 11
VMEM_LIMIT = 56 * 1024 * 1024


def _cparams(n_axes):
    return pltpu.CompilerParams(dimension_semantics=("arbitrary",) * n_axes,
                                vmem_limit_bytes=VMEM_LIMIT)


def _row_tile(lp):
    for t in (384, 256, 128):
        if lp % t == 0:
            return t
    raise ValueError(lp)


def _inproj_body(*refs, nt, nb_first, k):
    front_ref = refs[0]
    x_first, x_second = refs[1:1 + k], refs[1 + k:1 + 2 * k]
    (w_ref, gmix_ref, bg_ref, cos_ref, sa_ref, sb_ref, gq_ref, gk_ref,
     qk_ref, v_ref, mqk_ref, mv_ref, mo_ref, gt_ref, xp_ref) = refs[1 + 2 * k:]
    i = pl.program_id(0)
    in_first = i // nt < nb_first
    parts = [jnp.where(in_first, a[...], b[...]) for a, b in zip(x_first, x_second)]
    head = jnp.where(i % nt == 0, front_ref[...], parts[0])
    x = jnp.concatenate((head,) + tuple(parts[1:]), axis=0)
    xp_ref[...] = x
    h = x * lax.rsqrt(jnp.mean(x * x, axis=-1, keepdims=True) + RMS_EPS) * gmix_ref[...]
    z = jnp.dot(h.astype(BF16), w_ref[...], preferred_element_type=F32)
    lane = lax.broadcasted_iota(I32, (1, LANE), 1)
    lo = lane < DA_QK_DIM
    cos = cos_ref[...]
    sa = sa_ref[...]
    sb = sb_ref[...]

    def normrope(u, g, scale):
        sq = u * u
        s_lo = jnp.sum(jnp.where(lo, sq, 0.0), axis=-1, keepdims=True)
        s_hi = jnp.sum(jnp.where(lo, 0.0, sq), axis=-1, keepdims=True)
        ms = jnp.where(lo, s_lo, s_hi) * (1.0 / DA_QK_DIM)
        y = u * lax.rsqrt(ms + RMS_EPS) * g
        y = y * cos + pltpu.roll(y, 8, 1) * sa + pltpu.roll(y, LANE - 8, 1) * sb
        return y * scale

    for hh in range(DA_HEADS):
        c0 = hh * LANE
        qk_ref[:, c0:c0 + LANE] = normrope(z[:, c0:c0 + LANE], gq_ref[...],
                                           DA_QK_DIM ** -0.5 * LOG2E).astype(BF16)
        c1 = DA_WIDTH + hh * LANE
        qk_ref[:, c1:c1 + LANE] = normrope(z[:, c1:c1 + LANE], gk_ref[...], 1.0).astype(BF16)
    v_ref[...] = z[:, 1024:1536].astype(BF16)
    mqk_ref[...] = z[:, 1536:2560]
    mv_ref[...] = z[:, 2560:3072].astype(BF16)
    mo_ref[...] = z[:, 3072:3584]
    g = z[:, 3584:3712] + bg_ref[...]
    gt_ref[0] = g.T[:N_GATES, :]


def _inproj(xs, front, w_in_p, g_mix, bg, cos, sa, sb, gq, gk, lp):
    tm = _row_tile(lp)
    nt = lp // tm
    k = tm // LANE
    seq_blocks = (lp - FRONT) // LANE
    counts = [x.shape[0] for x in xs]
    nb = sum(counts)
    t_all = nb * lp
    row = lambda i: (i, 0)
    tab = lambda i: (i % nt, 0)
    const = lambda i: (0, 0)

    def x_block(first_seq, n_seq, r):
        def index(i):
            b = i // nt - first_seq
            blk = jnp.clip(b * seq_blocks + k * (i % nt) - 1 + r, 0, n_seq * seq_blocks - 1)
            return (jnp.where(b < 0, 0, jnp.where(b >= n_seq, n_seq * seq_blocks - 1, blk)), 0)
        return pl.BlockSpec((LANE, D_MODEL), index)

    x_specs = [x_block(0, counts[0], r) for r in range(k)]
    x_specs += [x_block(counts[0], counts[1], r) for r in range(k)]
    x_args = [xs[0].reshape(-1, D_MODEL)] * k + [xs[1].reshape(-1, D_MODEL)] * k
    return pl.pallas_call(
        functools.partial(_inproj_body, nt=nt, nb_first=counts[0], k=k),
        grid=(nb * nt,),
        in_specs=[pl.BlockSpec((LANE, D_MODEL), const)] + x_specs + [
            pl.BlockSpec((D_MODEL, D_IN_PAD), const),
            pl.BlockSpec((1, D_MODEL), const),
            pl.BlockSpec((1, LANE), const),
            pl.BlockSpec((tm, LANE), tab),
            pl.BlockSpec((tm, LANE), tab),
            pl.BlockSpec((tm, LANE), tab),
            pl.BlockSpec((1, LANE), const),
            pl.BlockSpec((1, LANE), const),
        ],
        out_specs=[
            pl.BlockSpec((tm, 1024), row),
            pl.BlockSpec((tm, 512), row),
            pl.BlockSpec((tm, 1024), row),
            pl.BlockSpec((tm, 512), row),
            pl.BlockSpec((tm, 512), row),
            pl.BlockSpec((1, N_GATES, tm), lambda i: (i // nt, 0, i % nt)),
            pl.BlockSpec((tm, D_MODEL), row),
        ],
        out_shape=[
            jax.ShapeDtypeStruct((t_all, 1024), BF16),
            jax.ShapeDtypeStruct((t_all, 512), BF16),
            jax.ShapeDtypeStruct((t_all, 1024), F32),
            jax.ShapeDtypeStruct((t_all, 512), BF16),
            jax.ShapeDtypeStruct((t_all, 512), F32),
            jax.ShapeDtypeStruct((nb, N_GATES, lp), F32),
            jax.ShapeDtypeStruct((t_all, D_MODEL), F32),
        ],
        compiler_params=_cparams(1),
        name="inproj",
    )(front, *x_args, w_in_p, g_mix, bg, cos, sa, sb, gq, gk)


def _attn_q_rows(lp):
    for t in (192, 128):
        if lp % t == 0:
            return t
    raise ValueError(lp)


def _attn_body(lam_ref, q_ref, k_ref, v_ref, bias_ref, g_ref, o_ref,
               va, s0, s1, m0, m1, e0, e1):
    lp = k_ref.shape[0]
    tq = s0.shape[1]
    nblk = lp // tq
    va[:, :LANE] = v_ref[...]
    va[:, LANE:] = jnp.where(lax.broadcasted_iota(I32, (lp, LANE), 1) == 0, 1.0, 0.0).astype(BF16)

    lv = lam_ref[...]
    lam = (jnp.exp(jnp.sum(lv[0:1] * lv[1:2], axis=-1, keepdims=True))
           - jnp.exp(jnp.sum(lv[2:3] * lv[3:4], axis=-1, keepdims=True)) + LAM_INIT)
    first = lax.broadcasted_iota(I32, (tq, LANE), 1) < DA_QK_DIM
    nt_dims = (((1,), (1,)), ((), ()))

    def rows_of(blk):
        return pl.ds(pl.multiple_of(blk * tq, tq), tq)

    def stage_a(blk, s_ref, m_ref):
        q = q_ref[rows_of(blk), :]
        zero = jnp.zeros_like(q)
        for j, qj in enumerate((jnp.where(first, q, zero), jnp.where(first, zero, q))):
            s = lax.dot_general(qj, k_ref[...], nt_dims, preferred_element_type=F32)
            s_pad = s[:, :FRONT] + bias_ref[...]
            s_ref[j, :, :FRONT] = s_pad
            s_ref[j, :, FRONT:] = s[:, FRONT:]
            m_ref[j] = jnp.maximum(jnp.max(s_pad, axis=-1, keepdims=True),
                                   jnp.max(s[:, FRONT:], axis=-1, keepdims=True))

    def stage_b(s_ref, m_ref, e_ref):
        for j in range(2):
            e_ref[j] = jnp.exp2(s_ref[j] - m_ref[j]).astype(BF16)

    def stage_c(blk, e_ref):
        r0 = jnp.dot(e_ref[0], va[...], preferred_element_type=F32)
        r1 = jnp.dot(e_ref[1], va[...], preferred_element_type=F32)
        o = r0[:, :LANE] / r0[:, LANE:LANE + 1] - lam * (r1[:, :LANE] / r1[:, LANE:LANE + 1])
        o = o * lax.rsqrt(jnp.mean(o * o, axis=-1, keepdims=True) + RMS_EPS) * g_ref[...]
        o_ref[rows_of(blk), :] = (o * (1.0 - LAM_INIT)).astype(BF16)

    stage_a(0, s0, m0)
    stage_a(min(1, nblk - 1), s1, m1)
    stage_b(s0, m0, e0)

    def step(t, carry):
        nxt = jnp.minimum(t + 1, nblk - 1)
        prv = t - 1

        @pl.when(t % 2 == 0)
        def _even():
            stage_a(nxt, s1, m1)
            stage_b(s0, m0, e0)
            stage_c(prv, e1)

        @pl.when(t % 2 == 1)
        def _odd():
            stage_a(nxt, s0, m0)
            stage_b(s1, m1, e1)
            stage_c(prv, e0)

        return carry

    lax.fori_loop(1, nblk, step, 0)
    stage_c(nblk - 1, e0 if nblk % 2 == 1 else e1)


def _attention(lamv, qk, v, bias, g_out, nb, lp):
    tq = _attn_q_rows(lp)
    blk = lambda col: pl.BlockSpec((lp, LANE), col)
    return pl.pallas_call(
        _attn_body,
        grid=(nb, DA_HEADS),
        in_specs=[
            pl.BlockSpec((4, DA_QK_DIM), lambda b, h: (0, 0)),
            blk(lambda b, h: (b, h)),
            blk(lambda b, h: (b, DA_HEADS + h)),
            blk(lambda b, h: (b, h)),
            pl.BlockSpec((1, FRONT), lambda b, h: (0, 0)),
            pl.BlockSpec((1, LANE), lambda b, h: (0, 0)),
        ],
        out_specs=blk(lambda b, h: (b, h)),
        out_shape=jax.ShapeDtypeStruct((nb * lp, DA_WIDTH), BF16),
        scratch_shapes=[
            pltpu.VMEM((lp, 2 * LANE), BF16),
            pltpu.VMEM((2, tq, lp), F32), pltpu.VMEM((2, tq, lp), F32),
            pltpu.VMEM((2, tq, 1), F32), pltpu.VMEM((2, tq, 1), F32),
            pltpu.VMEM((2, tq, lp), BF16), pltpu.VMEM((2, tq, lp), BF16),
        ],
        compiler_params=_cparams(2),
        name="diff_attention",
    )(lamv, qk, qk, v, bias, g_out)


def _log_sigmoid(x):
    return jnp.minimum(x, 0.0) - jnp.log(1.0 + jnp.exp(-jnp.abs(x)))


def _lane_scan(x, forward, op, identity):
    lane = lax.broadcasted_iota(I32, x.shape, 1)
    sh = 1
    while sh < LANE:
        if forward:
            x = op(x, jnp.where(lane >= sh, pltpu.roll(x, sh, 1), identity))
        else:
            x = op(x, jnp.where(lane < LANE - sh, pltpu.roll(x, LANE - sh, 1), identity))
        sh *= 2
    return x


def _mlstm_body(q_ref, k_ref, v_ref, og_ref, gif_ref, gff_ref, gib_ref, gfb_ref,
                cwq_ref, cwk_ref, gml_ref, o_ref,
                qs, ks, va, hf, hb, rowv0, rowv1, uvec0, uvec1, scal0, scal1, colv, r1, kvs):
    gates = (gif_ref, gff_ref, gib_ref, gfb_ref)
    rowv, uvec, scal = (rowv0, rowv1), (uvec0, uvec1), (scal0, scal1)
    lp = q_ref.shape[0]
    nc = lp // MLSTM_CHUNK
    c = MLSTM_CHUNK
    row = lax.broadcasted_iota(I32, (lp, 1), 0)

    row_c = lax.broadcasted_iota(I32, (c, 1), 0)

    def conv_chunk(ci, carry):
        start = pl.multiple_of(ci * c, c)
        before = pl.ds(pl.multiple_of(jnp.maximum(start - SUBLANE, 0), SUBLANE), SUBLANE)
        after = pl.ds(pl.multiple_of(jnp.minimum(start + c, lp - SUBLANE), SUBLANE), SUBLANE)
        for src, w_ref, dst, scale in ((q_ref, cwq_ref, qs, ML_HEAD_DIM ** -0.5),
                                       (k_ref, cwk_ref, ks, 1.0)):
            x = src[pl.ds(start, c), :]
            w = w_ref[...]
            last_before = jnp.where(ci > 0, src[before, :][SUBLANE - 1:SUBLANE, :], 0.0)
            first_after = jnp.where(ci < nc - 1, src[after, :][0:1, :], 0.0)
            x_prev = jnp.where(row_c == 0, last_before, pltpu.roll(x, 1, 0))
            x_next = jnp.where(row_c == c - 1, first_after, pltpu.roll(x, c - 1, 0))
            y = x_prev * w[0:1] + x * w[1:2] + x_next * w[2:3]
            dst[pl.ds(start, c), :] = (y * jax.nn.sigmoid(y) * scale).astype(BF16)
        return carry

    lax.fori_loop(0, nc, conv_chunk, 0)
    va[:, :LANE] = v_ref[...]
    va[:, LANE:] = jnp.ones((lp, LANE), BF16)

    r_i = lax.broadcasted_iota(I32, (c, c), 0)
    c_i = lax.broadcasted_iota(I32, (c, c), 1)
    ncp = gates[0].shape[2]
    lane_c = lax.broadcasted_iota(I32, (ncp, c), 1)
    pos = lax.broadcasted_iota(I32, (ncp, c), 0) * c + lane_c
    valid = pos >= PAD_ROWS

    for d, forward in enumerate((True, False)):
        li = jnp.where(valid, gates[2 * d][0, 0], NEG_BIG)
        lf = jnp.where(valid, _log_sigmoid(gates[2 * d + 1][0, 0]), 0.0)
        cum = _lane_scan(lf, forward, jnp.add, 0.0)
        u = li - cum
        cm = _lane_scan(u, forward, jnp.maximum, -jnp.inf)
        b_last = cum[:, c - 1:c] if forward else cum[:, 0:1]
        a = b_last + u
        a_max = jnp.max(a, axis=-1, keepdims=True)
        rowv[d][...] = jnp.zeros_like(rowv[d])
        for r, vec in enumerate((cm, jnp.exp(a - a_max), cum)):
            rowv[d][pl.ds(r, ncp, stride=SUBLANE), :] = vec
        uvec[d][...] = u
        scal[d][0] = jnp.broadcast_to(b_last, (ncp, c))
        scal[d][1] = jnp.broadcast_to(a_max, (ncp, c))

    def rows_of(ci):
        return pl.ds(pl.multiple_of(ci * c, c), c)

    def intra(ci, d, slot, qc, kc, k_t, vc):
        forward = d == 0
        tile = rowv[d][pl.ds(pl.multiple_of(ci * SUBLANE, SUBLANE), SUBLANE), :]

        def column(r):
            return jnp.broadcast_to(tile[r:r + 1, :], (c, c)).T

        cm_c = column(0)
        cum_c = column(2)
        colv[d, 0, slot] = cm_c
        colv[d, 1, slot] = cum_c
        mask = (c_i <= r_i) if forward else (c_i >= r_i)
        w = jnp.exp(jnp.where(mask, uvec[d][pl.ds(ci, 1), :] - cm_c, -jnp.inf))
        s = lax.dot_general(qc, kc, (((1,), (1,)), ((), ())), preferred_element_type=F32) * w
        r1[d, slot] = jnp.dot(s.astype(BF16), vc, preferred_element_type=F32)
        kw_t = k_t * tile[1:2, :]
        kvs[d, slot] = jnp.dot(kw_t.astype(BF16), vc, preferred_element_type=F32)

    def inter_step(ci, d, slot, state, m_prev):
        cm_c = colv[d, 0, slot]
        cum_c = colv[d, 1, slot]
        m_loc = cum_c + cm_c
        inter = cum_c + m_prev
        m_t = jnp.maximum(m_loc, inter)
        r2 = jnp.dot(qs[rows_of(ci), :], state.astype(BF16), preferred_element_type=F32)
        w_intra = jnp.exp(m_loc - m_t)
        w_inter = jnp.exp(inter - m_t)
        r1v = r1[d, slot]
        num = w_intra * r1v[:, :LANE] + w_inter * r2[:, :LANE]
        den = w_intra * r1v[:, LANE:] + w_inter * r2[:, LANE:]
        h_out = num / jnp.maximum(jnp.abs(den), jnp.exp(-m_t))
        b_last = scal[d][0, pl.ds(ci, 1), :]
        a_max = scal[d][1, pl.ds(ci, 1), :]
        m_new = jnp.maximum(b_last + m_prev, a_max)
        keep = jnp.exp(b_last + m_prev - m_new)
        add = jnp.exp(a_max - m_new)
        state_new = (jnp.concatenate([keep, keep], axis=1) * state
                     + jnp.concatenate([add, add], axis=1) * kvs[d, slot])
        return h_out, state_new, m_new

    def intra_both(ci, carry):
        rows = rows_of(ci)
        qc = qs[rows, :]
        kc = ks[rows, :]
        vc = va[rows, :]
        k_t = kc.astype(F32).T
        intra(ci, 0, ci, qc, kc, k_t, vc)
        intra(ci, 1, ci, qc, kc, k_t, vc)
        return carry

    lax.fori_loop(0, nc, intra_both, 0, unroll=MLSTM_UNROLL)

    def body(i, carry):
        sf, mf, sb, mb = carry
        h_f, sf, mf = inter_step(i, 0, i, sf, mf)
        hf[rows_of(i), :] = h_f
        j = nc - 1 - i
        h_b, sb, mb = inter_step(j, 1, j, sb, mb)
        hb[rows_of(j), :] = h_b
        return sf, mf, sb, mb

    z_state = jnp.zeros((ML_HEAD_DIM, 2 * LANE), F32)
    z_m = jnp.zeros((1, LANE), F32)
    lax.fori_loop(0, nc, body, (z_state, z_m, z_state, z_m))

    hs = hf[...] + hb[...]
    y = hs * lax.rsqrt(jnp.mean(hs * hs, axis=-1, keepdims=True) + RMS_EPS) * gml_ref[...]
    o_ref[...] = (y * jax.nn.sigmoid(og_ref[...])).astype(BF16)


def _mlstm(mqk, mv, mo, gt, conv_w, g_ml, nb, lp):
    blk = lambda col: pl.BlockSpec((lp, LANE), col)
    c = MLSTM_CHUNK
    nc = lp // c
    ncp = _round_up(nc, SUBLANE)
    gt4 = jnp.pad(gt.reshape(nb, N_GATES, nc, c), ((0, 0), (0, 0), (0, ncp - nc), (0, 0)))
    gate = lambda g: pl.BlockSpec((1, 1, ncp, c), lambda b, h: (b, g * ML_HEADS + h, 0, 0))
    vec = lambda rows, cols=LANE: pltpu.VMEM((rows, cols), F32)
    return pl.pallas_call(
        _mlstm_body,
        grid=(nb, ML_HEADS),
        in_specs=[
            blk(lambda b, h: (b, h)),
            blk(lambda b, h: (b, ML_HEADS + h)),
            blk(lambda b, h: (b, h)),
            pl.BlockSpec((lp, LANE), lambda b, h: (b, h), pipeline_mode=pl.Buffered(1)),
            gate(0), gate(1), gate(2), gate(3),
            pl.BlockSpec((3, LANE), lambda b, h: (0, h)),
            pl.BlockSpec((3, LANE), lambda b, h: (0, ML_HEADS + h)),
            pl.BlockSpec((1, LANE), lambda b, h: (0, 0)),
        ],
        out_specs=blk(lambda b, h: (b, h)),
        out_shape=jax.ShapeDtypeStruct((nb * lp, ML_WIDTH), BF16),
        scratch_shapes=[
            pltpu.VMEM((lp, LANE), BF16),
            pltpu.VMEM((lp, LANE), BF16),
            pltpu.VMEM((lp, 2 * LANE), BF16),
            vec(lp), vec(lp),
            vec(ncp * SUBLANE), vec(ncp * SUBLANE),
            vec(ncp), vec(ncp),
            pltpu.VMEM((2, ncp, c), F32), pltpu.VMEM((2, ncp, c), F32),
            pltpu.VMEM((2, 2, nc, c, LANE), F32),
            pltpu.VMEM((2, nc, c, 2 * LANE), F32),
            pltpu.VMEM((2, nc, ML_HEAD_DIM, 2 * LANE), F32),
        ],
        compiler_params=_cparams(2),
        name="bidir_mlstm",
    )(mqk, mqk, mv, mo, gt4, gt4, gt4, gt4, conv_w, conv_w, g_ml)


def _outproj_body(x_ref, oa_ref, hm_ref, wa_ref, wm_ref, gffn_ref, wr_ref, valid_ref,
                  x2_ref, xn_ref, affc_ref, afft_ref):
    x2 = (x_ref[...]
          + jnp.dot(oa_ref[...], wa_ref[...], preferred_element_type=F32)
          + jnp.dot(hm_ref[...], wm_ref[...], preferred_element_type=F32))
    x2_ref[...] = x2
    xn = x2 * lax.rsqrt(jnp.mean(x2 * x2, axis=-1, keepdims=True) + RMS_EPS) * gffn_ref[...]
    xn_ref[...] = xn.astype(BF16)
    xh = xn.astype(BF16)
    xl = (xn - xh.astype(F32)).astype(BF16)
    r_h = jnp.dot(xh, wr_ref[...], preferred_element_type=F32)
    logits = (r_h[:, :LANE] + r_h[:, LANE:]
              + jnp.dot(xl, wr_ref[:, :LANE], preferred_element_type=F32))
    lane = lax.broadcasted_iota(I32, logits.shape, 1)
    real = lane < N_EXPERTS
    logits = jnp.where(real, logits, -jnp.inf)
    e = jnp.exp(logits - jnp.max(logits, axis=-1, keepdims=True))
    aff = e / jnp.sum(e, axis=-1, keepdims=True)
    aff = jnp.where(real & (valid_ref[...] > 0.0), aff, -1.0)
    affc_ref[...] = aff
    afft_ref[...] = aff.T[:N_EXPERTS, :]


def _outproj(xp, oa, hm, w_a, w_m, g_ffn, w_r, valid, nb, lp):
    tm = _row_tile(lp)
    nt = lp // tm
    t_all = nb * lp
    row = lambda i: (i, 0)
    const = lambda i: (0, 0)
    return pl.pallas_call(
        _outproj_body,
        grid=(nb * nt,),
        in_specs=[
            pl.BlockSpec((tm, D_MODEL), row),
            pl.BlockSpec((tm, DA_WIDTH), row),
            pl.BlockSpec((tm, ML_WIDTH), row),
            pl.BlockSpec((DA_WIDTH, D_MODEL), const),
            pl.BlockSpec((ML_WIDTH, D_MODEL), const),
            pl.BlockSpec((1, D_MODEL), const),
            pl.BlockSpec((D_MODEL, 2 * LANE), const),
            pl.BlockSpec((tm, 1), lambda i: (i % nt, 0)),
        ],
        out_specs=[
            pl.BlockSpec((tm, D_MODEL), row),
            pl.BlockSpec((tm, D_MODEL), row),
            pl.BlockSpec((tm, LANE), row),
            pl.BlockSpec((N_EXPERTS, tm), lambda i: (0, i)),
        ],
        out_shape=[
            jax.ShapeDtypeStruct((t_all, D_MODEL), F32),
            jax.ShapeDtypeStruct((t_all, D_MODEL), BF16),
            jax.ShapeDtypeStruct((t_all, LANE), F32),
            jax.ShapeDtypeStruct((N_EXPERTS, t_all), F32),
        ],
        compiler_params=_cparams(1),
        name="outproj_router",
    )(xp, oa, hm, w_a, w_m, g_ffn, w_r, valid)


def _rope_tables(lp):
    pos = jnp.arange(lp, dtype=F32) - float(PAD_ROWS)
    inv = ROPE_THETA ** (-jnp.arange(0, ROPE_DIM, 2, dtype=F32) / ROPE_DIM)
    ang = pos[:, None] * inv[None, :]
    cos8, sin8 = jnp.cos(ang), jnp.sin(ang)
    half = ROPE_DIM // 2
    one_block = jnp.ones((lp, DA_QK_DIM - ROPE_DIM), F32)
    zero8 = jnp.zeros((lp, half), F32)
    zero_block = jnp.zeros((lp, DA_QK_DIM - ROPE_DIM), F32)
    cos_c = jnp.concatenate([cos8, cos8, one_block], axis=1)
    sa_c = jnp.concatenate([zero8, sin8, zero_block], axis=1)
    sb_c = jnp.concatenate([-sin8, zero8, zero_block], axis=1)
    tile2 = lambda a: jnp.concatenate([a, a], axis=1)
    return tile2(cos_c), tile2(sa_c), tile2(sb_c)


def _mixer(xs, meta, lp, g_mix, w_in, b_gates, conv_w, g_q, g_k, lamv, g_da_out, g_ml_out, w_out,
           g_ffn, w_router):
    nb = sum(x.shape[0] for x in xs)
    w_in_p = jnp.pad(w_in.astype(BF16), ((0, 0), (0, D_IN_PAD - D_IN)))
    bg = jnp.pad(b_gates, (0, LANE - N_GATES)).reshape(1, LANE)
    cos, sa, sb = _rope_tables(lp)
    gq = jnp.tile(g_q, 2).reshape(1, LANE)
    gk = jnp.tile(g_k, 2).reshape(1, LANE)
    front = jnp.pad(meta.astype(F32), ((PAD_ROWS, 0), (0, 0)))
    qk, v, mqk, mv, mo, gt, xp = _inproj(xs, front, w_in_p, g_mix.reshape(1, D_MODEL), bg, cos,
                                         sa, sb, gq, gk, lp)
    bias = jnp.where(jnp.arange(FRONT) < PAD_ROWS, NEG_BIG, 0.0).astype(F32).reshape(1, FRONT)
    oa = _attention(lamv, qk, v, bias, g_da_out.reshape(1, LANE), nb, lp)
    hm = _mlstm(mqk, mv, mo, gt, conv_w, g_ml_out.reshape(1, LANE), nb, lp)
    w_o = w_out.astype(BF16)
    w_r = jnp.pad(w_router, ((0, 0), (0, LANE - N_EXPERTS)))
    w_r_hi = w_r.astype(BF16)
    w_r = jnp.concatenate([w_r_hi, (w_r - w_r_hi.astype(F32)).astype(BF16)], axis=1)
    valid = (jnp.arange(lp) >= PAD_ROWS).astype(F32).reshape(lp, 1)
    return _outproj(xp, oa, hm, w_o[:DA_WIDTH], w_o[DA_WIDTH:], g_ffn.reshape(1, D_MODEL), w_r,
                    valid, nb, lp)


TOK_TILE = LANE
WIN = 32
SUB = SUBLANE
ROW_ALIGN = 16
assert WIN % ROW_ALIGN == 0
CWIN = WIN + ROW_ALIGN
FFN_ROWS = 512
FF_CHUNK = 256


def _route_body(aff_ref, thr_ref, cnt_ref, eqa_ref, seg_ref, *, cap, nt):
    a = aff_ref[...]
    capf = float(cap)

    def search(i, v):
        trial = v | lax.shift_left(jnp.int32(1), 30 - i)
        c = jnp.sum((a >= pltpu.bitcast(trial, F32)).astype(F32), axis=-1, keepdims=True)
        return jnp.where(c >= capf, trial, v)

    thr_bits = lax.fori_loop(0, 31, search, jnp.zeros((N_EXPERTS, 1), I32))
    thr = pltpu.bitcast(thr_bits, F32)
    gt = a > thr
    eq = a == thr
    need = capf - jnp.sum(gt.astype(F32), axis=-1, keepdims=True)
    thr_ref[...] = jnp.broadcast_to(thr, (N_EXPERTS, LANE))

    slab = LANE * TOK_TILE
    ind = (lax.broadcasted_iota(I32, (slab, LANE), 0) // TOK_TILE
           == lax.broadcasted_iota(I32, (slab, LANE), 1)).astype(BF16)
    upper = (lax.broadcasted_iota(I32, (LANE, LANE), 0)
             < lax.broadcasted_iota(I32, (LANE, LANE), 1)).astype(BF16)
    eq_base = jnp.zeros((N_EXPERTS, 1), F32)
    seg_base = jnp.zeros((N_EXPERTS, 1), F32)
    for s0 in range(0, nt, LANE):
        n_t = min(LANE, nt - s0)
        cols = slice(s0 * TOK_TILE, (s0 + n_t) * TOK_TILE)
        gt_cnt = jnp.dot(gt[:, cols].astype(BF16), ind[:n_t * TOK_TILE], preferred_element_type=F32)
        eq_cnt = jnp.dot(eq[:, cols].astype(BF16), ind[:n_t * TOK_TILE], preferred_element_type=F32)
        eq_before = eq_base + jnp.dot(eq_cnt.astype(BF16), upper, preferred_element_type=F32)
        eq_allow = jnp.clip(need - eq_before, 0.0, eq_cnt)
        cnt = gt_cnt + eq_allow
        seg = seg_base + jnp.dot(cnt.astype(BF16), upper, preferred_element_type=F32)
        eq_base = eq_base + jnp.sum(eq_cnt, axis=-1, keepdims=True)
        seg_base = seg_base + jnp.sum(cnt, axis=-1, keepdims=True)
        cnt_ref[:, s0:s0 + LANE] = cnt.astype(I32)
        eqa_ref[:, s0:s0 + LANE] = eq_allow.astype(I32)
        seg_ref[:, s0:s0 + LANE] = seg.astype(I32)


def _route(aff_t, cap):
    tg = aff_t.shape[1]
    nt = tg // TOK_TILE
    ntp = -(-nt // LANE) * LANE
    tab = jax.ShapeDtypeStruct((N_EXPERTS, ntp), I32)
    return pl.pallas_call(
        functools.partial(_route_body, cap=cap, nt=nt),
        out_shape=[jax.ShapeDtypeStruct((N_EXPERTS, LANE), F32), tab, tab, tab],
        compiler_params=pltpu.CompilerParams(vmem_limit_bytes=VMEM_LIMIT),
        name="route",
    )(aff_t)


def _window_rows(cnt, p):
    return jnp.clip(cnt - p * WIN, 0, WIN)


def _dispatch_body(seg_sm, cnt_sm, xn_ref, aff_ref, thr_ref, eqa_ref, *rest,
                   nt, ntp, base, stride, tail_start):
    xg_ref, pbuf, stage, stage_x, zbuf, sem = rest[-6:]
    j = pl.program_id(0)

    if tail_start is not None:
        @pl.when(j == 0)
        def _zero_tail():
            zbuf[...] = jnp.zeros_like(zbuf)
            copies = [pltpu.make_async_copy(
                zbuf, xg_ref.at[pl.ds((e * stride + tail_start) * SUB, zbuf.shape[0])],
                sem.at[2].at[e])
                for e in range(N_EXPERTS)]
            for c in copies:
                c.start()
            for c in copies:
                c.wait()

    a = aff_ref[...]
    thr = thr_ref[...]
    lane = lax.broadcasted_iota(I32, (N_EXPERTS, LANE), 1)
    eqa = jnp.sum(jnp.where(lane == j % LANE, eqa_ref[...], 0).astype(F32), axis=-1, keepdims=True)
    upper = (lax.broadcasted_iota(I32, (LANE, LANE), 0)
             < lax.broadcasted_iota(I32, (LANE, LANE), 1)).astype(BF16)
    gt = a > thr
    eq = a == thr
    eq_rank = jnp.dot(eq.astype(BF16), upper, preferred_element_type=F32)
    sel = gt | (eq & (eq_rank < eqa))
    rank = jnp.dot(sel.astype(BF16), upper, preferred_element_type=F32)
    rank = jnp.where(sel, rank, -1.0)

    n_pass = (functools.reduce(jnp.maximum, [cnt_sm[e * ntp + j] for e in range(N_EXPERTS)])
              + WIN - 1) // WIN
    win_row = lax.broadcasted_iota(I32, (WIN, LANE), 0).astype(F32)

    def gather_rows(p, dst):
        off = (p * WIN).astype(F32) if not isinstance(p, int) else float(p * WIN)
        for e in range(N_EXPERTS):
            r_e = jnp.broadcast_to(rank[e:e + 1, :], (WIN, LANE))
            pbuf[e * WIN:(e + 1) * WIN, :] = (r_e == win_row + off).astype(BF16)
        g = jnp.dot(pbuf[...], xn_ref[...], preferred_element_type=F32)
        for s in range(SUB):
            dst[pl.ds(s, N_EXPERTS * WIN, stride=SUB), :] = g[:, s * LANE:(s + 1) * LANE]

    def copy(t, e, p, src, sem_row):
        n = _window_rows(cnt_sm[e * ntp + t], p) * SUB
        row = pl.multiple_of((e * stride + base + seg_sm[e * ntp + t] + p * WIN) * SUB, SUB)
        return pltpu.make_async_copy(src.at[pl.ds(e * WIN * SUB, n)], xg_ref.at[pl.ds(row, n)],
                                     sem_row.at[e])

    def for_experts(t, p, fn):
        for e in range(N_EXPERTS):
            @pl.when(cnt_sm[e * ntp + t] > p * WIN)
            def _():
                fn(e)

    slot = j % 2
    mine = stage.at[slot]

    @pl.when(j >= 2)
    def _reuse():
        for_experts(j - 2, 0, lambda e: copy(j - 2, e, 0, mine, sem.at[slot]).wait())

    gather_rows(0, mine)
    for_experts(j, 0, lambda e: copy(j, e, 0, mine, sem.at[slot]).start())

    def extra_pass(p, carry):
        gather_rows(p, stage_x)
        for_experts(j, p, lambda e: copy(j, e, p, stage_x, sem.at[2]).start())
        for_experts(j, p, lambda e: copy(j, e, p, stage_x, sem.at[2]).wait())
        return carry

    lax.fori_loop(1, n_pass, extra_pass, 0)

    @pl.when(j == nt - 1)
    def _drain():
        if nt >= 2:
            for_experts(j - 1, 0, lambda e: copy(j - 1, e, 0, stage.at[1 - slot], sem.at[1 - slot]).wait())
        for_experts(j, 0, lambda e: copy(j, e, 0, mine, sem.at[slot]).wait())


def _dispatch(seg, cnt, xn, aff_t, thr, eqa, xg, *, tile0, nt, ntp, base, stride, tail_start):
    first = xg is None
    zero_tail = first and stride > tail_start
    tail_rows = (stride - tail_start) * SUB if zero_tail else SUB
    kern = functools.partial(_dispatch_body, nt=nt, ntp=ntp, base=base, stride=stride,
                             tail_start=tail_start if zero_tail else None)
    in_specs = [
        pl.BlockSpec((TOK_TILE, D_MODEL), lambda j, s, c: (tile0 + j, 0)),
        pl.BlockSpec((N_EXPERTS, TOK_TILE), lambda j, s, c: (0, tile0 + j)),
        pl.BlockSpec((N_EXPERTS, LANE), lambda j, s, c: (0, 0)),
        pl.BlockSpec((N_EXPERTS, LANE), lambda j, s, c: (0, j // LANE)),
    ]
    args = [seg, cnt, xn, aff_t, thr, eqa]
    aliases = {}
    if not first:
        in_specs.append(pl.BlockSpec(memory_space=pl.ANY))
        args.append(xg)
        aliases = {len(args) - 1: 0}
    return pl.pallas_call(
        kern,
        grid_spec=pltpu.PrefetchScalarGridSpec(
            num_scalar_prefetch=2,
            grid=(nt,),
            in_specs=in_specs,
            out_specs=pl.BlockSpec(memory_space=pl.ANY),
            scratch_shapes=[
                pltpu.VMEM((N_EXPERTS * WIN, TOK_TILE), BF16),
                pltpu.VMEM((2, N_EXPERTS * WIN * SUB, LANE), F32),
                pltpu.VMEM((N_EXPERTS * WIN * SUB, LANE), F32),
                pltpu.VMEM((tail_rows, LANE), F32),
                pltpu.SemaphoreType.DMA((3, N_EXPERTS)),
            ]),
        out_shape=jax.ShapeDtypeStruct((N_EXPERTS * stride * SUB, LANE), F32),
        input_output_aliases=aliases,
        compiler_params=pltpu.CompilerParams(dimension_semantics=("arbitrary",),
                                             vmem_limit_bytes=VMEM_LIMIT, has_side_effects=True),
        name="dispatch",
    )(*args)


def _ffn_body(xg_ref, wg_ref, wu_ref, wd_ref, wr_ref, o_ref, xb):
    e = pl.program_id(0)
    tm = xb.shape[0]
    for s in range(SUB):
        xb[:, s * LANE:(s + 1) * LANE] = xg_ref[pl.ds(s, tm, stride=SUB), :].astype(BF16)
    x = xb[...]
    logits = jnp.dot(x, wr_ref[...], preferred_element_type=F32)
    lane = lax.broadcasted_iota(I32, logits.shape, 1)
    logits = jnp.where(lane < N_EXPERTS, logits, -jnp.inf)
    ex = jnp.exp(logits - jnp.max(logits, axis=-1, keepdims=True))
    aff = ex / jnp.sum(ex, axis=-1, keepdims=True)
    gate = jnp.sum(jnp.where(lane == e, aff, 0.0), axis=-1, keepdims=True)
    acc = jnp.zeros((tm, D_MODEL), F32)
    for f in range(0, D_FF, FF_CHUNK):
        g = jnp.dot(x, wg_ref[0, :, f:f + FF_CHUNK], preferred_element_type=F32)
        u = jnp.dot(x, wu_ref[0, :, f:f + FF_CHUNK], preferred_element_type=F32)
        h = (g * jax.nn.sigmoid(g) * u).astype(BF16)
        acc = acc + jnp.dot(h, wd_ref[0, f:f + FF_CHUNK, :], preferred_element_type=F32)
    o_ref[...] = (acc * gate).astype(BF16)


def _ffn(xg, wg, wu, wd, w_r, stride):
    nr = stride // FFN_ROWS
    blk = pl.BlockSpec((FFN_ROWS * SUB, LANE), lambda e, i: (e * nr + i, 0))
    return pl.pallas_call(
        _ffn_body,
        grid=(N_EXPERTS, nr),
        in_specs=[
            blk,
            pl.BlockSpec((1, D_MODEL, D_FF), lambda e, i: (e, 0, 0)),
            pl.BlockSpec((1, D_MODEL, D_FF), lambda e, i: (e, 0, 0)),
            pl.BlockSpec((1, D_FF, D_MODEL), lambda e, i: (e, 0, 0)),
            pl.BlockSpec((D_MODEL, LANE), lambda e, i: (0, 0)),
        ],
        out_specs=pl.BlockSpec((FFN_ROWS, D_MODEL), lambda e, i: (e * nr + i, 0)),
        out_shape=jax.ShapeDtypeStruct((N_EXPERTS * stride, D_MODEL), BF16),
        scratch_shapes=[pltpu.VMEM((FFN_ROWS, D_MODEL), BF16)],
        compiler_params=_cparams(2),
        name="expert_ffn",
    )(xg, wg, wu, wd, w_r)


def _combine_body(seg_sm, cnt_sm, x2_ref, aff_ref, thr_ref, eqa_ref, ye_ref, o_ref,
                  ybuf, ybuf_x, rep, sem, *, nt, ntp, ntl, base, stride):
    j = pl.program_id(0)
    ncol = N_EXPERTS * CWIN

    def first_row(t, e):
        return e * stride + base + seg_sm[e * ntp + t]

    def copy(t, e, p, dst, sem_row):
        r0 = first_row(t, e) + p * WIN
        off = r0 & (ROW_ALIGN - 1)
        n = pl.multiple_of(
            (off + _window_rows(cnt_sm[e * ntp + t], p) + ROW_ALIGN - 1) & ~(ROW_ALIGN - 1), ROW_ALIGN)
        return pltpu.make_async_copy(ye_ref.at[pl.ds(pl.multiple_of(r0 - off, ROW_ALIGN), n)],
                                     dst.at[pl.ds(e * CWIN, n)], sem_row.at[e])

    def for_experts(t, p, fn):
        for e in range(N_EXPERTS):
            @pl.when(cnt_sm[e * ntp + t] > p * WIN)
            def _():
                fn(e)

    @pl.when(j == 0)
    def _init():
        ybuf[...] = jnp.zeros_like(ybuf)
        ybuf_x[...] = jnp.zeros_like(ybuf_x)
        col_of = lax.broadcasted_iota(I32, (LANE, ncol), 1)
        blk_lo = lax.broadcasted_iota(I32, (LANE, ncol), 0) * CWIN
        rep[...] = ((col_of >= blk_lo) & (col_of < blk_lo + CWIN)).astype(BF16)

    nxt = jnp.minimum(j + 1, nt - 1)

    @pl.when((j + 1 < nt) & (nxt % ntl > 0))
    def _prefetch():
        for_experts(nxt, 0, lambda e: copy(nxt, e, 0, ybuf.at[nxt % 2], sem.at[nxt % 2]).start())

    @pl.when(j % ntl > 0)
    def _tile():
        a = aff_ref[...]
        thr = thr_ref[...]
        eqa = eqa_ref[0]
        lower = (lax.broadcasted_iota(I32, (TOK_TILE, TOK_TILE), 1)
                 < lax.broadcasted_iota(I32, (TOK_TILE, TOK_TILE), 0)).astype(BF16)
        gt = a > thr
        eq = a == thr
        eq_rank = jnp.dot(lower, eq.astype(BF16), preferred_element_type=F32)
        sel = gt | (eq & (eq_rank < eqa))
        rank = jnp.dot(lower, sel.astype(BF16), preferred_element_type=F32)
        rank = jnp.where(sel, rank, -1.0).astype(BF16)
        col = lax.broadcasted_iota(I32, (1, ncol), 1)
        rank_rep = jnp.dot(rank, rep[...], preferred_element_type=F32)
        col_slot = col
        for e in range(N_EXPERTS):
            col_slot = jnp.where(col >= e * CWIN,
                                 col - (e * CWIN + (first_row(j, e) & (ROW_ALIGN - 1))), col_slot)
        col_slot = col_slot.astype(F32)
        n_pass = (functools.reduce(jnp.maximum, [cnt_sm[e * ntp + j] for e in range(N_EXPERTS)])
                  + WIN - 1) // WIN

        def onehot(p):
            lo = p * WIN if isinstance(p, int) else (p * WIN).astype(F32)
            in_pass = (rank_rep >= lo) & (rank_rep < lo + WIN)
            return (in_pass & (rank_rep - lo == col_slot)).astype(BF16)

        slot = j % 2
        for_experts(j, 0, lambda e: copy(j, e, 0, ybuf.at[slot], sem.at[slot]).wait())
        acc = x2_ref[...] + jnp.dot(onehot(0), ybuf[slot], preferred_element_type=F32)

        def extra_pass(p, acc):
            for_experts(j, p, lambda e: copy(j, e, p, ybuf_x, sem.at[2]).start())
            hot = onehot(p)
            for_experts(j, p, lambda e: copy(j, e, p, ybuf_x, sem.at[2]).wait())
            return acc + jnp.dot(hot, ybuf_x[...], preferred_element_type=F32)

        o_ref[...] = lax.fori_loop(1, n_pass, extra_pass, acc)


def _combine(seg, cnt, x2, aff_c, thr_row, eqa_rows, ye, *, tile0, nt, ntp, ntl, base, stride):
    ncol = N_EXPERTS * CWIN
    n_out = nt // ntl * (ntl - 1)
    out_map = lambda j, s, c: (j // ntl * (ntl - 1) + jnp.maximum(j % ntl - 1, 0), 0)
    return pl.pallas_call(
        functools.partial(_combine_body, nt=nt, ntp=ntp, ntl=ntl, base=base, stride=stride),
        grid_spec=pltpu.PrefetchScalarGridSpec(
            num_scalar_prefetch=2,
            grid=(nt,),
            in_specs=[
                pl.BlockSpec((TOK_TILE, D_MODEL), lambda j, s, c: (tile0 + j, 0)),
                pl.BlockSpec((TOK_TILE, LANE), lambda j, s, c: (tile0 + j, 0)),
                pl.BlockSpec((1, LANE), lambda j, s, c: (0, 0)),
                pl.BlockSpec((1, 1, LANE), lambda j, s, c: (j, 0, 0)),
                pl.BlockSpec(memory_space=pl.ANY),
            ],
            out_specs=pl.BlockSpec((TOK_TILE, D_MODEL), out_map),
            scratch_shapes=[
                pltpu.VMEM((2, ncol, D_MODEL), BF16),
                pltpu.VMEM((ncol, D_MODEL), BF16),
                pltpu.VMEM((LANE, ncol), BF16),
                pltpu.SemaphoreType.DMA((3, N_EXPERTS)),
            ]),
        out_shape=jax.ShapeDtypeStruct((n_out * TOK_TILE, D_MODEL), F32),
        compiler_params=pltpu.CompilerParams(dimension_semantics=("arbitrary",),
                                             vmem_limit_bytes=VMEM_LIMIT),
        name="combine",
    )(seg, cnt, x2, aff_c, thr_row, eqa_rows, ye)


def _round_up(a, b):
    return -(-a // b) * b


def kernel(x_prompt, x_sample, meta, g_mix, w_in, b_gates, conv_w, g_q, g_k, lam_q1, lam_k1, lam_q2,
           lam_k2, g_da_out, g_ml_out, w_out, g_ffn, w_router, w_gate, w_up, w_down):
    assert x_prompt.shape[1:] == x_sample.shape[1:]
    groups = (x_prompt.shape[0], x_sample.shape[0])
    s = x_prompt.shape[1]
    lp = FRONT + s
    assert s % LANE == 0
    ntl = lp // TOK_TILE
    lamv = jnp.stack([lam_q1[0], lam_k1[0], lam_q2[0], lam_k2[0]])
    x2, xn, aff_c, aff_t = _mixer((x_prompt, x_sample), meta, lp, g_mix[0], w_in[0], b_gates[0],
                                  conv_w[0], g_q[0], g_k[0], lamv, g_da_out[0], g_ml_out[0],
                                  w_out[0], g_ffn[0], w_router[0])

    caps = [max(1, CAP_FACTOR * b * (N_META + s) // N_EXPERTS) for b in groups]
    bases = [sum(caps[:i]) for i in range(len(caps))]
    tail_start = sum(caps)
    stride = _round_up(tail_start, FFN_ROWS)
    infos = []
    tile0 = 0
    for b, cap in zip(groups, caps):
        nt = b * ntl
        thr, cnt, eqa, seg = _route(aff_t[:, tile0 * TOK_TILE:(tile0 + nt) * TOK_TILE], cap)
        ntp = cnt.shape[1]
        thr_row = jnp.pad(thr[:, 0], (0, LANE - N_EXPERTS), constant_values=2.0).reshape(1, LANE)
        eqa_rows = jnp.pad(eqa.T.astype(F32), ((0, 0), (0, LANE - N_EXPERTS))).reshape(ntp, 1, LANE)
        infos.append(dict(tile0=tile0, nt=nt, ntp=ntp, cap=cap, thr=thr, thr_row=thr_row, eqa=eqa,
                          eqa_rows=eqa_rows, seg=seg.reshape(-1), cnt=cnt.reshape(-1)))
        tile0 += nt

    xg = None
    for gi, info in enumerate(infos):
        xg = _dispatch(info["seg"], info["cnt"], xn, aff_t, info["thr"], info["eqa"], xg,
                       tile0=info["tile0"], nt=info["nt"], ntp=info["ntp"], base=bases[gi],
                       stride=stride, tail_start=tail_start)
    w_r = jnp.pad(w_router[0].astype(BF16), ((0, 0), (0, LANE - N_EXPERTS)))
    ye = _ffn(xg, w_gate[0].astype(BF16), w_up[0].astype(BF16), w_down[0].astype(BF16), w_r, stride)

    outs = []
    for gi, (b, info) in enumerate(zip(groups, infos)):
        y = _combine(info["seg"], info["cnt"], x2, aff_c, info["thr_row"], info["eqa_rows"], ye,
                     tile0=info["tile0"], nt=info["nt"], ntp=info["ntp"], ntl=ntl, base=bases[gi],
                     stride=stride)
        outs.append(y.reshape(b, s, D_MODEL))
    return tuple(outs)
```

```python
import functools
import math

import jax
import jax.numpy as jnp
from jax import lax
from jax.experimental import pallas as pl
from jax.experimental.pallas import tpu as pltpu

F32 = jnp.float32
BF16 = jnp.bfloat16
I32 = jnp.int32

D_MODEL = 1024
N_META = 16
LANE = 128
SUBLANE = 8
FRONT = LANE
PAD_ROWS = FRONT - N_META
RMS_EPS = 1e-6
DA_HEADS = 4
DA_WIDTH = 512
DA_QK_DIM = 64
ROPE_DIM = 16
ROPE_THETA = 500000.0
ML_HEADS = 4
ML_WIDTH = 512
ML_HEAD_DIM = 128
NEG_BIG = -1e30
N_GATES = 16
D_IN = 3600
D_IN_PAD = 3712
N_EXPERTS = 16
CAP_FACTOR = 2
D_FF = 2816
LAM_INIT = 0.8 - 0.6 * math.exp(-0.3 * 0)
LOG2E = math.log2(math.e)
MLSTM_CHUNK = LANE
MLSTM_UNROLL = 11
VMEM_LIMIT = 56 * 1024 * 1024


def _cparams(n_axes):
    return pltpu.CompilerParams(dimension_semantics=("arbitrary",) * n_axes,
                                vmem_limit_bytes=VMEM_LIMIT)


def _row_tile(lp):
    for t in (384, 256, 128):
        if lp % t == 0:
            return t
    raise ValueError(lp)


def _inproj_body(*refs, nt, nb_first, k):
    front_ref = refs[0]
    x_first, x_second = refs[1:1 + k], refs[1 + k:1 + 2 * k]
    (w_ref, gmix_ref, bg_ref, cos_ref, sa_ref, sb_ref, gq_ref, gk_ref,
     qk_ref, v_ref, mqk_ref, mv_ref, mo_ref, gt_ref, xp_ref) = refs[1 + 2 * k:]
    i = pl.program_id(0)
    in_first = i // nt < nb_first
    parts = [jnp.where(in_first, a[...], b[...]) for a, b in zip(x_first, x_second)]
    head = jnp.where(i % nt == 0, front_ref[...], parts[0])
    x = jnp.concatenate((head,) + tuple(parts[1:]), axis=0)
    xp_ref[...] = x
    h = x * lax.rsqrt(jnp.mean(x * x, axis=-1, keepdims=True) + RMS_EPS) * gmix_ref[...]
    z = jnp.dot(h.astype(BF16), w_ref[...], preferred_element_type=F32)
    lane = lax.broadcasted_iota(I32, (1, LANE), 1)
    lo = lane < DA_QK_DIM
    cos = cos_ref[...]
    sa = sa_ref[...]
    sb = sb_ref[...]

    def normrope(u, g, scale):
        sq = u * u
        s_lo = jnp.sum(jnp.where(lo, sq, 0.0), axis=-1, keepdims=True)
        s_hi = jnp.sum(jnp.where(lo, 0.0, sq), axis=-1, keepdims=True)
        ms = jnp.where(lo, s_lo, s_hi) * (1.0 / DA_QK_DIM)
        y = u * lax.rsqrt(ms + RMS_EPS) * g
        y = y * cos + pltpu.roll(y, 8, 1) * sa + pltpu.roll(y, LANE - 8, 1) * sb
        return y * scale

    for hh in range(DA_HEADS):
        c0 = hh * LANE
        qk_ref[:, c0:c0 + LANE] = normrope(z[:, c0:c0 + LANE], gq_ref[...],
                                           DA_QK_DIM ** -0.5 * LOG2E).astype(BF16)
        c1 = DA_WIDTH + hh * LANE
        qk_ref[:, c1:c1 + LANE] = normrope(z[:, c1:c1 + LANE], gk_ref[...], 1.0).astype(BF16)
    v_ref[...] = z[:, 1024:1536].astype(BF16)
    mqk_ref[...] = z[:, 1536:2560]
    mv_ref[...] = z[:, 2560:3072].astype(BF16)
    mo_ref[...] = z[:, 3072:3584]
    g = z[:, 3584:3712] + bg_ref[...]
    gt_ref[0] = g.T[:N_GATES, :]


def _inproj(xs, front, w_in_p, g_mix, bg, cos, sa, sb, gq, gk, lp):
    tm = _row_tile(lp)
    nt = lp // tm
    k = tm // LANE
    seq_blocks = (lp - FRONT) // LANE
    counts = [x.shape[0] for x in xs]
    nb = sum(counts)
    t_all = nb * lp
    row = lambda i: (i, 0)
    tab = lambda i: (i % nt, 0)
    const = lambda i: (0, 0)

    def x_block(first_seq, n_seq, r):
        def index(i):
            b = i // nt - first_seq
            blk = jnp.clip(b * seq_blocks + k * (i % nt) - 1 + r, 0, n_seq * seq_blocks - 1)
            return (jnp.where(b < 0, 0, jnp.where(b >= n_seq, n_seq * seq_blocks - 1, blk)), 0)
        return pl.BlockSpec((LANE, D_MODEL), index)

    x_specs = [x_block(0, counts[0], r) for r in range(k)]
    x_specs += [x_block(counts[0], counts[1], r) for r in range(k)]
    x_args = [xs[0].reshape(-1, D_MODEL)] * k + [xs[1].reshape(-1, D_MODEL)] * k
    return pl.pallas_call(
        functools.partial(_inproj_body, nt=nt, nb_first=counts[0], k=k),
        grid=(nb * nt,),
        in_specs=[pl.BlockSpec((LANE, D_MODEL), const)] + x_specs + [
            pl.BlockSpec((D_MODEL, D_IN_PAD), const),
            pl.BlockSpec((1, D_MODEL), const),
            pl.BlockSpec((1, LANE), const),
            pl.BlockSpec((tm, LANE), tab),
            pl.BlockSpec((tm, LANE), tab),
            pl.BlockSpec((tm, LANE), tab),
            pl.BlockSpec((1, LANE), const),
            pl.BlockSpec((1, LANE), const),
        ],
        out_specs=[
            pl.BlockSpec((tm, 1024), row),
            pl.BlockSpec((tm, 512), row),
            pl.BlockSpec((tm, 1024), row),
            pl.BlockSpec((tm, 512), row),
            pl.BlockSpec((tm, 512), row),
            pl.BlockSpec((1, N_GATES, tm), lambda i: (i // nt, 0, i % nt)),
            pl.BlockSpec((tm, D_MODEL), row),
        ],
        out_shape=[
            jax.ShapeDtypeStruct((t_all, 1024), BF16),
            jax.ShapeDtypeStruct((t_all, 512), BF16),
            jax.ShapeDtypeStruct((t_all, 1024), F32),
            jax.ShapeDtypeStruct((t_all, 512), BF16),
            jax.ShapeDtypeStruct((t_all, 512), F32),
            jax.ShapeDtypeStruct((nb, N_GATES, lp), F32),
            jax.ShapeDtypeStruct((t_all, D_MODEL), F32),
        ],
        compiler_params=_cparams(1),
        name="inproj",
    )(front, *x_args, w_in_p, g_mix, bg, cos, sa, sb, gq, gk)


def _attn_q_rows(lp):
    for t in (192, 128):
        if lp % t == 0:
            return t
    raise ValueError(lp)


def _attn_body(lam_ref, q_ref, k_ref, v_ref, bias_ref, g_ref, o_ref,
               va, s0, s1, m0, m1, e0, e1):
    lp = k_ref.shape[0]
    tq = s0.shape[1]
    nblk = lp // tq
    va[:, :LANE] = v_ref[...]
    va[:, LANE:] = jnp.where(lax.broadcasted_iota(I32, (lp, LANE), 1) == 0, 1.0, 0.0).astype(BF16)

    lv = lam_ref[...]
    lam = (jnp.exp(jnp.sum(lv[0:1] * lv[1:2], axis=-1, keepdims=True))
           - jnp.exp(jnp.sum(lv[2:3] * lv[3:4], axis=-1, keepdims=True)) + LAM_INIT)
    first = lax.broadcasted_iota(I32, (tq, LANE), 1) < DA_QK_DIM
    nt_dims = (((1,), (1,)), ((), ()))

    def rows_of(blk):
        return pl.ds(pl.multiple_of(blk * tq, tq), tq)

    def stage_a(blk, s_ref, m_ref):
        q = q_ref[rows_of(blk), :]
        zero = jnp.zeros_like(q)
        for j, qj in enumerate((jnp.where(first, q, zero), jnp.where(first, zero, q))):
            s = lax.dot_general(qj, k_ref[...], nt_dims, preferred_element_type=F32)
            s_pad = s[:, :FRONT] + bias_ref[...]
            s_ref[j, :, :FRONT] = s_pad
            s_ref[j, :, FRONT:] = s[:, FRONT:]
            m_ref[j] = jnp.maximum(jnp.max(s_pad, axis=-1, keepdims=True),
                                   jnp.max(s[:, FRONT:], axis=-1, keepdims=True))

    def stage_b(s_ref, m_ref, e_ref):
        for j in range(2):
            e_ref[j] = jnp.exp2(s_ref[j] - m_ref[j]).astype(BF16)

    def stage_c(blk, e_ref):
        r0 = jnp.dot(e_ref[0], va[...], preferred_element_type=F32)
        r1 = jnp.dot(e_ref[1], va[...], preferred_element_type=F32)
        o = r0[:, :LANE] / r0[:, LANE:LANE + 1] - lam * (r1[:, :LANE] / r1[:, LANE:LANE + 1])
        o = o * lax.rsqrt(jnp.mean(o * o, axis=-1, keepdims=True) + RMS_EPS) * g_ref[...]
        o_ref[rows_of(blk), :] = (o * (1.0 - LAM_INIT)).astype(BF16)

    stage_a(0, s0, m0)
    stage_a(min(1, nblk - 1), s1, m1)
    stage_b(s0, m0, e0)

    def step(t, carry):
        nxt = jnp.minimum(t + 1, nblk - 1)
        prv = t - 1

        @pl.when(t % 2 == 0)
        def _even():
            stage_a(nxt, s1, m1)
            stage_b(s0, m0, e0)
            stage_c(prv, e1)

        @pl.when(t % 2 == 1)
        def _odd():
            stage_a(nxt, s0, m0)
            stage_b(s1, m1, e1)
            stage_c(prv, e0)

        return carry

    lax.fori_loop(1, nblk, step, 0)
    stage_c(nblk - 1, e0 if nblk % 2 == 1 else e1)


def _attention(lamv, qk, v, bias, g_out, nb, lp):
    tq = _attn_q_rows(lp)
    blk = lambda col: pl.BlockSpec((lp, LANE), col)
    return pl.pallas_call(
        _attn_body,
        grid=(nb, DA_HEADS),
        in_specs=[
            pl.BlockSpec((4, DA_QK_DIM), lambda b, h: (0, 0)),
            blk(lambda b, h: (b, h)),
            blk(lambda b, h: (b, DA_HEADS + h)),
            blk(lambda b, h: (b, h)),
            pl.BlockSpec((1, FRONT), lambda b, h: (0, 0)),
            pl.BlockSpec((1, LANE), lambda b, h: (0, 0)),
        ],
        out_specs=blk(lambda b, h: (b, h)),
        out_shape=jax.ShapeDtypeStruct((nb * lp, DA_WIDTH), BF16),
        scratch_shapes=[
            pltpu.VMEM((lp, 2 * LANE), BF16),
            pltpu.VMEM((2, tq, lp), F32), pltpu.VMEM((2, tq, lp), F32),
            pltpu.VMEM((2, tq, 1), F32), pltpu.VMEM((2, tq, 1), F32),
            pltpu.VMEM((2, tq, lp), BF16), pltpu.VMEM((2, tq, lp), BF16),
        ],
        compiler_params=_cparams(2),
        name="diff_attention",
    )(lamv, qk, qk, v, bias, g_out)


def _log_sigmoid(x):
    return jnp.minimum(x, 0.0) - jnp.log(1.0 + jnp.exp(-jnp.abs(x)))


def _lane_scan(x, forward, op, identity):
    lane = lax.broadcasted_iota(I32, x.shape, 1)
    sh = 1
    while sh < LANE:
        if forward:
            x = op(x, jnp.where(lane >= sh, pltpu.roll(x, sh, 1), identity))
        else:
            x = op(x, jnp.where(lane < LANE - sh, pltpu.roll(x, LANE - sh, 1), identity))
        sh *= 2
    return x


def _mlstm_body(q_ref, k_ref, v_ref, og_ref, gif_ref, gff_ref, gib_ref, gfb_ref,
                cwq_ref, cwk_ref, gml_ref, o_ref,
                qs, ks, va, hf, hb, rowv0, rowv1, uvec0, uvec1, scal0, scal1, colv, r1, kvs):
    gates = (gif_ref, gff_ref, gib_ref, gfb_ref)
    rowv, uvec, scal = (rowv0, rowv1), (uvec0, uvec1), (scal0, scal1)
    lp = q_ref.shape[0]
    nc = lp // MLSTM_CHUNK
    c = MLSTM_CHUNK
    row = lax.broadcasted_iota(I32, (lp, 1), 0)

    row_c = lax.broadcasted_iota(I32, (c, 1), 0)

    def conv_chunk(ci, carry):
        start = pl.multiple_of(ci * c, c)
        before = pl.ds(pl.multiple_of(jnp.maximum(start - SUBLANE, 0), SUBLANE), SUBLANE)
        after = pl.ds(pl.multiple_of(jnp.minimum(start + c, lp - SUBLANE), SUBLANE), SUBLANE)
        for src, w_ref, dst, scale in ((q_ref, cwq_ref, qs, ML_HEAD_DIM ** -0.5),
                                       (k_ref, cwk_ref, ks, 1.0)):
            x = src[pl.ds(start, c), :]
            w = w_ref[...]
            last_before = jnp.where(ci > 0, src[before, :][SUBLANE - 1:SUBLANE, :], 0.0)
            first_after = jnp.where(ci < nc - 1, src[after, :][0:1, :], 0.0)
            x_prev = jnp.where(row_c == 0, last_before, pltpu.roll(x, 1, 0))
            x_next = jnp.where(row_c == c - 1, first_after, pltpu.roll(x, c - 1, 0))
            y = x_prev * w[0:1] + x * w[1:2] + x_next * w[2:3]
            dst[pl.ds(start, c), :] = (y * jax.nn.sigmoid(y) * scale).astype(BF16)
        return carry

    lax.fori_loop(0, nc, conv_chunk, 0)
    va[:, :LANE] = v_ref[...]
    va[:, LANE:] = jnp.ones((lp, LANE), BF16)

    r_i = lax.broadcasted_iota(I32, (c, c), 0)
    c_i = lax.broadcasted_iota(I32, (c, c), 1)
    ncp = gates[0].shape[2]
    lane_c = lax.broadcasted_iota(I32, (ncp, c), 1)
    pos = lax.broadcasted_iota(I32, (ncp, c), 0) * c + lane_c
    valid = pos >= PAD_ROWS

    for d, forward in enumerate((True, False)):
        li = jnp.where(valid, gates[2 * d][0, 0], NEG_BIG)
        lf = jnp.where(valid, _log_sigmoid(gates[2 * d + 1][0, 0]), 0.0)
        cum = _lane_scan(lf, forward, jnp.add, 0.0)
        u = li - cum
        cm = _lane_scan(u, forward, jnp.maximum, -jnp.inf)
        b_last = cum[:, c - 1:c] if forward else cum[:, 0:1]
        a = b_last + u
        a_max = jnp.max(a, axis=-1, keepdims=True)
        rowv[d][...] = jnp.zeros_like(rowv[d])
        for r, vec in enumerate((cm, jnp.exp(a - a_max), cum)):
            rowv[d][pl.ds(r, ncp, stride=SUBLANE), :] = vec
        uvec[d][...] = u
        scal[d][0] = jnp.broadcast_to(b_last, (ncp, c))
        scal[d][1] = jnp.broadcast_to(a_max, (ncp, c))

    def rows_of(ci):
        return pl.ds(pl.multiple_of(ci * c, c), c)

    def intra(ci, d, slot, qc, kc, k_t, vc):
        forward = d == 0
        tile = rowv[d][pl.ds(pl.multiple_of(ci * SUBLANE, SUBLANE), SUBLANE), :]

        def column(r):
            return jnp.broadcast_to(tile[r:r + 1, :], (c, c)).T

        cm_c = column(0)
        cum_c = column(2)
        colv[d, 0, slot] = cm_c
        colv[d, 1, slot] = cum_c
        mask = (c_i <= r_i) if forward else (c_i >= r_i)
        w = jnp.exp(jnp.where(mask, uvec[d][pl.ds(ci, 1), :] - cm_c, -jnp.inf))
        s = lax.dot_general(qc, kc, (((1,), (1,)), ((), ())), preferred_element_type=F32) * w
        r1[d, slot] = jnp.dot(s.astype(BF16), vc, preferred_element_type=F32)
        kw_t = k_t * tile[1:2, :]
        kvs[d, slot] = jnp.dot(kw_t.astype(BF16), vc, preferred_element_type=F32)

    def inter_step(ci, d, slot, state, m_prev):
        cm_c = colv[d, 0, slot]
        cum_c = colv[d, 1, slot]
        m_loc = cum_c + cm_c
        inter = cum_c + m_prev
        m_t = jnp.maximum(m_loc, inter)
        r2 = jnp.dot(qs[rows_of(ci), :], state.astype(BF16), preferred_element_type=F32)
        w_intra = jnp.exp(m_loc - m_t)
        w_inter = jnp.exp(inter - m_t)
        r1v = r1[d, slot]
        num = w_intra * r1v[:, :LANE] + w_inter * r2[:, :LANE]
        den = w_intra * r1v[:, LANE:] + w_inter * r2[:, LANE:]
        h_out = num / jnp.maximum(jnp.abs(den), jnp.exp(-m_t))
        b_last = scal[d][0, pl.ds(ci, 1), :]
        a_max = scal[d][1, pl.ds(ci, 1), :]
        m_new = jnp.maximum(b_last + m_prev, a_max)
        keep = jnp.exp(b_last + m_prev - m_new)
        add = jnp.exp(a_max - m_new)
        state_new = (jnp.concatenate([keep, keep], axis=1) * state
                     + jnp.concatenate([add, add], axis=1) * kvs[d, slot])
        return h_out, state_new, m_new

    def intra_both(ci, carry):
        rows = rows_of(ci)
        qc = qs[rows, :]
        kc = ks[rows, :]
        vc = va[rows, :]
        k_t = kc.astype(F32).T
        intra(ci, 0, ci, qc, kc, k_t, vc)
        intra(ci, 1, ci, qc, kc, k_t, vc)
        return carry

    lax.fori_loop(0, nc, intra_both, 0, unroll=MLSTM_UNROLL)

    def body(i, carry):
        sf, mf, sb, mb = carry
        h_f, sf, mf = inter_step(i, 0, i, sf, mf)
        hf[rows_of(i), :] = h_f
        j = nc - 1 - i
        h_b, sb, mb = inter_step(j, 1, j, sb, mb)
        hb[rows_of(j), :] = h_b
        return sf, mf, sb, mb

    z_state = jnp.zeros((ML_HEAD_DIM, 2 * LANE), F32)
    z_m = jnp.zeros((1, LANE), F32)
    lax.fori_loop(0, nc, body, (z_state, z_m, z_state, z_m))

    hs = hf[...] + hb[...]
    y = hs * lax.rsqrt(jnp.mean(hs * hs, axis=-1, keepdims=True) + RMS_EPS) * gml_ref[...]
    o_ref[...] = (y * jax.nn.sigmoid(og_ref[...])).astype(BF16)


def _mlstm(mqk, mv, mo, gt, conv_w, g_ml, nb, lp):
    blk = lambda col: pl.BlockSpec((lp, LANE), col)
    c = MLSTM_CHUNK
    nc = lp // c
    ncp = _round_up(nc, SUBLANE)
    gt4 = jnp.pad(gt.reshape(nb, N_GATES, nc, c), ((0, 0), (0, 0), (0, ncp - nc), (0, 0)))
    gate = lambda g: pl.BlockSpec((1, 1, ncp, c), lambda b, h: (b, g * ML_HEADS + h, 0, 0))
    vec = lambda rows, cols=LANE: pltpu.VMEM((rows, cols), F32)
    return pl.pallas_call(
        _mlstm_body,
        grid=(nb, ML_HEADS),
        in_specs=[
            blk(lambda b, h: (b, h)),
            blk(lambda b, h: (b, ML_HEADS + h)),
            blk(lambda b, h: (b, h)),
            pl.BlockSpec((lp, LANE), lambda b, h: (b, h), pipeline_mode=pl.Buffered(1)),
            gate(0), gate(1), gate(2), gate(3),
            pl.BlockSpec((3, LANE), lambda b, h: (0, h)),
            pl.BlockSpec((3, LANE), lambda b, h: (0, ML_HEADS + h)),
            pl.BlockSpec((1, LANE), lambda b, h: (0, 0)),
        ],
        out_specs=blk(lambda b, h: (b, h)),
        out_shape=jax.ShapeDtypeStruct((nb * lp, ML_WIDTH), BF16),
        scratch_shapes=[
            pltpu.VMEM((lp, LANE), BF16),
            pltpu.VMEM((lp, LANE), BF16),
            pltpu.VMEM((lp, 2 * LANE), BF16),
            vec(lp), vec(lp),
            vec(ncp * SUBLANE), vec(ncp * SUBLANE),
            vec(ncp), vec(ncp),
            pltpu.VMEM((2, ncp, c), F32), pltpu.VMEM((2, ncp, c), F32),
            pltpu.VMEM((2, 2, nc, c, LANE), F32),
            pltpu.VMEM((2, nc, c, 2 * LANE), F32),
            pltpu.VMEM((2, nc, ML_HEAD_DIM, 2 * LANE), F32),
        ],
        compiler_params=_cparams(2),
        name="bidir_mlstm",
    )(mqk, mqk, mv, mo, gt4, gt4, gt4, gt4, conv_w, conv_w, g_ml)


def _outproj_body(x_ref, oa_ref, hm_ref, wa_ref, wm_ref, gffn_ref, wr_ref, valid_ref,
                  x2_ref, xn_ref, affc_ref, afft_ref):
    x2 = (x_ref[...]
          + jnp.dot(oa_ref[...], wa_ref[...], preferred_element_type=F32)
          + jnp.dot(hm_ref[...], wm_ref[...], preferred_element_type=F32))
    x2_ref[...] = x2
    xn = x2 * lax.rsqrt(jnp.mean(x2 * x2, axis=-1, keepdims=True) + RMS_EPS) * gffn_ref[...]
    xn_ref[...] = xn.astype(BF16)
    xh = xn.astype(BF16)
    xl = (xn - xh.astype(F32)).astype(BF16)
    r_h = jnp.dot(xh, wr_ref[...], preferred_element_type=F32)
    logits = (r_h[:, :LANE] + r_h[:, LANE:]
              + jnp.dot(xl, wr_ref[:, :LANE], preferred_element_type=F32))
    lane = lax.broadcasted_iota(I32, logits.shape, 1)
    real = lane < N_EXPERTS
    logits = jnp.where(real, logits, -jnp.inf)
    e = jnp.exp(logits - jnp.max(logits, axis=-1, keepdims=True))
    aff = e / jnp.sum(e, axis=-1, keepdims=True)
    aff = jnp.where(real & (valid_ref[...] > 0.0), aff, -1.0)
    affc_ref[...] = aff
    afft_ref[...] = aff.T[:N_EXPERTS, :]


def _outproj(xp, oa, hm, w_a, w_m, g_ffn, w_r, valid, nb, lp):
    tm = _row_tile(lp)
    nt = lp // tm
    t_all = nb * lp
    row = lambda i: (i, 0)
    const = lambda i: (0, 0)
    return pl.pallas_call(
        _outproj_body,
        grid=(nb * nt,),
        in_specs=[
            pl.BlockSpec((tm, D_MODEL), row),
            pl.BlockSpec((tm, DA_WIDTH), row),
            pl.BlockSpec((tm, ML_WIDTH), row),
            pl.BlockSpec((DA_WIDTH, D_MODEL), const),
            pl.BlockSpec((ML_WIDTH, D_MODEL), const),
            pl.BlockSpec((1, D_MODEL), const),
            pl.BlockSpec((D_MODEL, 2 * LANE), const),
            pl.BlockSpec((tm, 1), lambda i: (i % nt, 0)),
        ],
        out_specs=[
            pl.BlockSpec((tm, D_MODEL), row),
            pl.BlockSpec((tm, D_MODEL), row),
            pl.BlockSpec((tm, LANE), row),
            pl.BlockSpec((N_EXPERTS, tm), lambda i: (0, i)),
        ],
        out_shape=[
            jax.ShapeDtypeStruct((t_all, D_MODEL), F32),
            jax.ShapeDtypeStruct((t_all, D_MODEL), BF16),
            jax.ShapeDtypeStruct((t_all, LANE), F32),
            jax.ShapeDtypeStruct((N_EXPERTS, t_all), F32),
        ],
        compiler_params=_cparams(1),
        name="outproj_router",
    )(xp, oa, hm, w_a, w_m, g_ffn, w_r, valid)


def _rope_tables(lp):
    pos = jnp.arange(lp, dtype=F32) - float(PAD_ROWS)
    inv = ROPE_THETA ** (-jnp.arange(0, ROPE_DIM, 2, dtype=F32) / ROPE_DIM)
    ang = pos[:, None] * inv[None, :]
    cos8, sin8 = jnp.cos(ang), jnp.sin(ang)
    half = ROPE_DIM // 2
    one_block = jnp.ones((lp, DA_QK_DIM - ROPE_DIM), F32)
    zero8 = jnp.zeros((lp, half), F32)
    zero_block = jnp.zeros((lp, DA_QK_DIM - ROPE_DIM), F32)
    cos_c = jnp.concatenate([cos8, cos8, one_block], axis=1)
    sa_c = jnp.concatenate([zero8, sin8, zero_block], axis=1)
    sb_c = jnp.concatenate([-sin8, zero8, zero_block], axis=1)
    tile2 = lambda a: jnp.concatenate([a, a], axis=1)
    return tile2(cos_c), tile2(sa_c), tile2(sb_c)


def _mixer(xs, meta, lp, g_mix, w_in, b_gates, conv_w, g_q, g_k, lamv, g_da_out, g_ml_out, w_out,
           g_ffn, w_router):
    nb = sum(x.shape[0] for x in xs)
    w_in_p = jnp.pad(w_in.astype(BF16), ((0, 0), (0, D_IN_PAD - D_IN)))
    bg = jnp.pad(b_gates, (0, LANE - N_GATES)).reshape(1, LANE)
    cos, sa, sb = _rope_tables(lp)
    gq = jnp.tile(g_q, 2).reshape(1, LANE)
    gk = jnp.tile(g_k, 2).reshape(1, LANE)
    front = jnp.pad(meta.astype(F32), ((PAD_ROWS, 0), (0, 0)))
    qk, v, mqk, mv, mo, gt, xp = _inproj(xs, front, w_in_p, g_mix.reshape(1, D_MODEL), bg, cos,
                                         sa, sb, gq, gk, lp)
    bias = jnp.where(jnp.arange(FRONT) < PAD_ROWS, NEG_BIG, 0.0).astype(F32).reshape(1, FRONT)
    oa = _attention(lamv, qk, v, bias, g_da_out.reshape(1, LANE), nb, lp)
    hm = _mlstm(mqk, mv, mo, gt, conv_w, g_ml_out.reshape(1, LANE), nb, lp)
    w_o = w_out.astype(BF16)
    w_r = jnp.pad(w_router, ((0, 0), (0, LANE - N_EXPERTS)))
    w_r_hi = w_r.astype(BF16)
    w_r = jnp.concatenate([w_r_hi, (w_r - w_r_hi.astype(F32)).astype(BF16)], axis=1)
    valid = (jnp.arange(lp) >= PAD_ROWS).astype(F32).reshape(lp, 1)
    return _outproj(xp, oa, hm, w_o[:DA_WIDTH], w_o[DA_WIDTH:], g_ffn.reshape(1, D_MODEL), w_r,
                    valid, nb, lp)


TOK_TILE = LANE
WIN = 32
SUB = SUBLANE
ROW_ALIGN = 16
assert WIN % ROW_ALIGN == 0
CWIN = WIN + ROW_ALIGN
FFN_ROWS = 512
FF_CHUNK = 256


def _route_body(aff_ref, thr_ref, cnt_ref, eqa_ref, seg_ref, *, cap, nt):
    a = aff_ref[...]
    capf = float(cap)

    def search(i, v):
        trial = v | lax.shift_left(jnp.int32(1), 30 - i)
        c = jnp.sum((a >= pltpu.bitcast(trial, F32)).astype(F32), axis=-1, keepdims=True)
        return jnp.where(c >= capf, trial, v)

    thr_bits = lax.fori_loop(0, 31, search, jnp.zeros((N_EXPERTS, 1), I32))
    thr = pltpu.bitcast(thr_bits, F32)
    gt = a > thr
    eq = a == thr
    need = capf - jnp.sum(gt.astype(F32), axis=-1, keepdims=True)
    thr_ref[...] = jnp.broadcast_to(thr, (N_EXPERTS, LANE))

    slab = LANE * TOK_TILE
    ind = (lax.broadcasted_iota(I32, (slab, LANE), 0) // TOK_TILE
           == lax.broadcasted_iota(I32, (slab, LANE), 1)).astype(BF16)
    upper = (lax.broadcasted_iota(I32, (LANE, LANE), 0)
             < lax.broadcasted_iota(I32, (LANE, LANE), 1)).astype(BF16)
    eq_base = jnp.zeros((N_EXPERTS, 1), F32)
    seg_base = jnp.zeros((N_EXPERTS, 1), F32)
    for s0 in range(0, nt, LANE):
        n_t = min(LANE, nt - s0)
        cols = slice(s0 * TOK_TILE, (s0 + n_t) * TOK_TILE)
        gt_cnt = jnp.dot(gt[:, cols].astype(BF16), ind[:n_t * TOK_TILE], preferred_element_type=F32)
        eq_cnt = jnp.dot(eq[:, cols].astype(BF16), ind[:n_t * TOK_TILE], preferred_element_type=F32)
        eq_before = eq_base + jnp.dot(eq_cnt.astype(BF16), upper, preferred_element_type=F32)
        eq_allow = jnp.clip(need - eq_before, 0.0, eq_cnt)
        cnt = gt_cnt + eq_allow
        seg = seg_base + jnp.dot(cnt.astype(BF16), upper, preferred_element_type=F32)
        eq_base = eq_base + jnp.sum(eq_cnt, axis=-1, keepdims=True)
        seg_base = seg_base + jnp.sum(cnt, axis=-1, keepdims=True)
        cnt_ref[:, s0:s0 + LANE] = cnt.astype(I32)
        eqa_ref[:, s0:s0 + LANE] = eq_allow.astype(I32)
        seg_ref[:, s0:s0 + LANE] = seg.astype(I32)


def _route(aff_t, cap):
    tg = aff_t.shape[1]
    nt = tg // TOK_TILE
    ntp = -(-nt // LANE) * LANE
    tab = jax.ShapeDtypeStruct((N_EXPERTS, ntp), I32)
    return pl.pallas_call(
        functools.partial(_route_body, cap=cap, nt=nt),
        out_shape=[jax.ShapeDtypeStruct((N_EXPERTS, LANE), F32), tab, tab, tab],
        compiler_params=pltpu.CompilerParams(vmem_limit_bytes=VMEM_LIMIT),
        name="route",
    )(aff_t)


def _window_rows(cnt, p):
    return jnp.clip(cnt - p * WIN, 0, WIN)


def _dispatch_body(seg_sm, cnt_sm, xn_ref, aff_ref, thr_ref, eqa_ref, *rest,
                   nt, ntp, base, stride, tail_start):
    xg_ref, pbuf, stage, stage_x, zbuf, sem = rest[-6:]
    j = pl.program_id(0)

    if tail_start is not None:
        @pl.when(j == 0)
        def _zero_tail():
            zbuf[...] = jnp.zeros_like(zbuf)
            copies = [pltpu.make_async_copy(
                zbuf, xg_ref.at[pl.ds((e * stride + tail_start) * SUB, zbuf.shape[0])],
                sem.at[2 * N_EXPERTS + e])
                for e in range(N_EXPERTS)]
            for c in copies:
                c.start()
            for c in copies:
                c.wait()

    a = aff_ref[...]
    thr = thr_ref[...]
    lane = lax.broadcasted_iota(I32, (N_EXPERTS, LANE), 1)
    eqa = jnp.sum(jnp.where(lane == j % LANE, eqa_ref[...], 0).astype(F32), axis=-1, keepdims=True)
    upper = (lax.broadcasted_iota(I32, (LANE, LANE), 0)
             < lax.broadcasted_iota(I32, (LANE, LANE), 1)).astype(BF16)
    gt = a > thr
    eq = a == thr
    eq_rank = jnp.dot(eq.astype(BF16), upper, preferred_element_type=F32)
    sel = gt | (eq & (eq_rank < eqa))
    rank = jnp.dot(sel.astype(BF16), upper, preferred_element_type=F32)
    rank = jnp.where(sel, rank, -1.0)

    n_pass = (functools.reduce(jnp.maximum, [cnt_sm[e * ntp + j] for e in range(N_EXPERTS)])
              + WIN - 1) // WIN
    win_row = lax.broadcasted_iota(I32, (WIN, LANE), 0).astype(F32)

    def gather_rows(p, dst):
        off = (p * WIN).astype(F32) if not isinstance(p, int) else float(p * WIN)
        for e in range(N_EXPERTS):
            r_e = jnp.broadcast_to(rank[e:e + 1, :], (WIN, LANE))
            pbuf[e * WIN:(e + 1) * WIN, :] = (r_e == win_row + off).astype(BF16)
        g = jnp.dot(pbuf[...], xn_ref[...], preferred_element_type=F32)
        for s in range(SUB):
            dst[pl.ds(s, N_EXPERTS * WIN, stride=SUB), :] = g[:, s * LANE:(s + 1) * LANE]

    def copy(t, e, p, src, sem_set):
        n = _window_rows(cnt_sm[e * ntp + t], p) * SUB
        row = pl.multiple_of((e * stride + base + seg_sm[e * ntp + t] + p * WIN) * SUB, SUB)
        return pltpu.make_async_copy(src.at[pl.ds(e * WIN * SUB, n)], xg_ref.at[pl.ds(row, n)],
                                     sem.at[sem_set * N_EXPERTS + e])

    def for_experts(t, p, fn):
        for e in range(N_EXPERTS):
            @pl.when(cnt_sm[e * ntp + t] > p * WIN)
            def _():
                fn(e)

    slot = j % 2
    mine = stage.at[slot]

    @pl.when(j >= 2)
    def _reuse():
        for_experts(j - 2, 0, lambda e: copy(j - 2, e, 0, mine, slot).wait())

    gather_rows(0, mine)
    for_experts(j, 0, lambda e: copy(j, e, 0, mine, slot).start())

    def extra_pass(p, carry):
        gather_rows(p, stage_x)
        for_experts(j, p, lambda e: copy(j, e, p, stage_x, 2).start())
        for_experts(j, p, lambda e: copy(j, e, p, stage_x, 2).wait())
        return carry

    lax.fori_loop(1, n_pass, extra_pass, 0)

    @pl.when(j == nt - 1)
    def _drain():
        if nt >= 2:
            for_experts(j - 1, 0, lambda e: copy(j - 1, e, 0, stage.at[1 - slot], 1 - slot).wait())
        for_experts(j, 0, lambda e: copy(j, e, 0, mine, slot).wait())


def _dispatch(seg, cnt, xn, aff_t, thr, eqa, xg, *, tile0, nt, ntp, base, stride, tail_start):
    first = xg is None
    zero_tail = first and stride > tail_start
    tail_rows = (stride - tail_start) * SUB if zero_tail else SUB
    kern = functools.partial(_dispatch_body, nt=nt, ntp=ntp, base=base, stride=stride,
                             tail_start=tail_start if zero_tail else None)
    in_specs = [
        pl.BlockSpec((TOK_TILE, D_MODEL), lambda j, s, c: (tile0 + j, 0)),
        pl.BlockSpec((N_EXPERTS, TOK_TILE), lambda j, s, c: (0, tile0 + j)),
        pl.BlockSpec((N_EXPERTS, LANE), lambda j, s, c: (0, 0)),
        pl.BlockSpec((N_EXPERTS, LANE), lambda j, s, c: (0, j // LANE)),
    ]
    args = [seg, cnt, xn, aff_t, thr, eqa]
    aliases = {}
    if not first:
        in_specs.append(pl.BlockSpec(memory_space=pl.ANY))
        args.append(xg)
        aliases = {len(args) - 1: 0}
    return pl.pallas_call(
        kern,
        grid_spec=pltpu.PrefetchScalarGridSpec(
            num_scalar_prefetch=2,
            grid=(nt,),
            in_specs=in_specs,
            out_specs=pl.BlockSpec(memory_space=pl.ANY),
            scratch_shapes=[
                pltpu.VMEM((N_EXPERTS * WIN, TOK_TILE), BF16),
                pltpu.VMEM((2, N_EXPERTS * WIN * SUB, LANE), F32),
                pltpu.VMEM((N_EXPERTS * WIN * SUB, LANE), F32),
                pltpu.VMEM((tail_rows, LANE), F32),
                pltpu.SemaphoreType.DMA((3 * N_EXPERTS,)),
            ]),
        out_shape=jax.ShapeDtypeStruct((N_EXPERTS * stride * SUB, LANE), F32),
        input_output_aliases=aliases,
        compiler_params=pltpu.CompilerParams(dimension_semantics=("arbitrary",),
                                             vmem_limit_bytes=VMEM_LIMIT, has_side_effects=True),
        name="dispatch",
    )(*args)


def _ffn_body(xg_ref, wg_ref, wu_ref, wd_ref, wr_ref, o_ref, xb):
    e = pl.program_id(0)
    tm = xb.shape[0]
    for s in range(SUB):
        xb[:, s * LANE:(s + 1) * LANE] = xg_ref[pl.ds(s, tm, stride=SUB), :].astype(BF16)
    x = xb[...]
    logits = jnp.dot(x, wr_ref[...], preferred_element_type=F32)
    lane = lax.broadcasted_iota(I32, logits.shape, 1)
    logits = jnp.where(lane < N_EXPERTS, logits, -jnp.inf)
    ex = jnp.exp(logits - jnp.max(logits, axis=-1, keepdims=True))
    aff = ex / jnp.sum(ex, axis=-1, keepdims=True)
    gate = jnp.sum(jnp.where(lane == e, aff, 0.0), axis=-1, keepdims=True)
    acc = jnp.zeros((tm, D_MODEL), F32)
    for f in range(0, D_FF, FF_CHUNK):
        g = jnp.dot(x, wg_ref[0, :, f:f + FF_CHUNK], preferred_element_type=F32)
        u = jnp.dot(x, wu_ref[0, :, f:f + FF_CHUNK], preferred_element_type=F32)
        h = (g * jax.nn.sigmoid(g) * u).astype(BF16)
        acc = acc + jnp.dot(h, wd_ref[0, f:f + FF_CHUNK, :], preferred_element_type=F32)
    o_ref[...] = (acc * gate).astype(BF16)


def _ffn(xg, wg, wu, wd, w_r, stride):
    nr = stride // FFN_ROWS
    blk = pl.BlockSpec((FFN_ROWS * SUB, LANE), lambda e, i: (e * nr + i, 0))
    return pl.pallas_call(
        _ffn_body,
        grid=(N_EXPERTS, nr),
        in_specs=[
            blk,
            pl.BlockSpec((1, D_MODEL, D_FF), lambda e, i: (e, 0, 0)),
            pl.BlockSpec((1, D_MODEL, D_FF), lambda e, i: (e, 0, 0)),
            pl.BlockSpec((1, D_FF, D_MODEL), lambda e, i: (e, 0, 0)),
            pl.BlockSpec((D_MODEL, LANE), lambda e, i: (0, 0)),
        ],
        out_specs=pl.BlockSpec((FFN_ROWS, D_MODEL), lambda e, i: (e * nr + i, 0)),
        out_shape=jax.ShapeDtypeStruct((N_EXPERTS * stride, D_MODEL), BF16),
        scratch_shapes=[pltpu.VMEM((FFN_ROWS, D_MODEL), BF16)],
        compiler_params=_cparams(2),
        name="expert_ffn",
    )(xg, wg, wu, wd, w_r)


def _combine_body(seg_sm, cnt_sm, x2_ref, aff_ref, thr_ref, eqa_ref, ye_ref, o_ref,
                  ybuf, ybuf_x, rep, sem, *, nt, ntp, ntl, base, stride):
    j = pl.program_id(0)
    ncol = N_EXPERTS * CWIN

    def first_row(t, e):
        return e * stride + base + seg_sm[e * ntp + t]

    def copy(t, e, p, dst, sem_set):
        r0 = first_row(t, e) + p * WIN
        off = r0 & (ROW_ALIGN - 1)
        n = pl.multiple_of(
            (off + _window_rows(cnt_sm[e * ntp + t], p) + ROW_ALIGN - 1) & ~(ROW_ALIGN - 1), ROW_ALIGN)
        return pltpu.make_async_copy(ye_ref.at[pl.ds(pl.multiple_of(r0 - off, ROW_ALIGN), n)],
                                     dst.at[pl.ds(e * CWIN, n)], sem.at[sem_set * N_EXPERTS + e])

    def for_experts(t, p, fn):
        for e in range(N_EXPERTS):
            @pl.when(cnt_sm[e * ntp + t] > p * WIN)
            def _():
                fn(e)

    @pl.when(j == 0)
    def _init():
        ybuf[...] = jnp.zeros_like(ybuf)
        ybuf_x[...] = jnp.zeros_like(ybuf_x)
        col_of = lax.broadcasted_iota(I32, (LANE, ncol), 1)
        blk_lo = lax.broadcasted_iota(I32, (LANE, ncol), 0) * CWIN
        rep[...] = ((col_of >= blk_lo) & (col_of < blk_lo + CWIN)).astype(BF16)

    nxt = jnp.minimum(j + 1, nt - 1)

    @pl.when((j + 1 < nt) & (nxt % ntl > 0))
    def _prefetch():
        for_experts(nxt, 0, lambda e: copy(nxt, e, 0, ybuf.at[nxt % 2], nxt % 2).start())

    @pl.when(j % ntl > 0)
    def _tile():
        a = aff_ref[...]
        thr = thr_ref[...]
        eqa = eqa_ref[0]
        lower = (lax.broadcasted_iota(I32, (TOK_TILE, TOK_TILE), 1)
                 < lax.broadcasted_iota(I32, (TOK_TILE, TOK_TILE), 0)).astype(BF16)
        gt = a > thr
        eq = a == thr
        eq_rank = jnp.dot(lower, eq.astype(BF16), preferred_element_type=F32)
        sel = gt | (eq & (eq_rank < eqa))
        rank = jnp.dot(lower, sel.astype(BF16), preferred_element_type=F32)
        rank = jnp.where(sel, rank, -1.0).astype(BF16)
        col = lax.broadcasted_iota(I32, (1, ncol), 1)
        rank_rep = jnp.dot(rank, rep[...], preferred_element_type=F32)
        col_slot = col
        for e in range(N_EXPERTS):
            col_slot = jnp.where(col >= e * CWIN,
                                 col - (e * CWIN + (first_row(j, e) & (ROW_ALIGN - 1))), col_slot)
        col_slot = col_slot.astype(F32)
        n_pass = (functools.reduce(jnp.maximum, [cnt_sm[e * ntp + j] for e in range(N_EXPERTS)])
                  + WIN - 1) // WIN

        def onehot(p):
            lo = p * WIN if isinstance(p, int) else (p * WIN).astype(F32)
            in_pass = (rank_rep >= lo) & (rank_rep < lo + WIN)
            return (in_pass & (rank_rep - lo == col_slot)).astype(BF16)

        slot = j % 2
        for_experts(j, 0, lambda e: copy(j, e, 0, ybuf.at[slot], slot).wait())
        acc = x2_ref[...] + jnp.dot(onehot(0), ybuf[slot], preferred_element_type=F32)

        def extra_pass(p, acc):
            for_experts(j, p, lambda e: copy(j, e, p, ybuf_x, 2).start())
            hot = onehot(p)
            for_experts(j, p, lambda e: copy(j, e, p, ybuf_x, 2).wait())
            return acc + jnp.dot(hot, ybuf_x[...], preferred_element_type=F32)

        o_ref[...] = lax.fori_loop(1, n_pass, extra_pass, acc)


def _combine(seg, cnt, x2, aff_c, thr_row, eqa_rows, ye, *, tile0, nt, ntp, ntl, base, stride):
    ncol = N_EXPERTS * CWIN
    n_out = nt // ntl * (ntl - 1)
    out_map = lambda j, s, c: (j // ntl * (ntl - 1) + jnp.maximum(j % ntl - 1, 0), 0)
    return pl.pallas_call(
        functools.partial(_combine_body, nt=nt, ntp=ntp, ntl=ntl, base=base, stride=stride),
        grid_spec=pltpu.PrefetchScalarGridSpec(
            num_scalar_prefetch=2,
            grid=(nt,),
            in_specs=[
                pl.BlockSpec((TOK_TILE, D_MODEL), lambda j, s, c: (tile0 + j, 0)),
                pl.BlockSpec((TOK_TILE, LANE), lambda j, s, c: (tile0 + j, 0)),
                pl.BlockSpec((1, LANE), lambda j, s, c: (0, 0)),
                pl.BlockSpec((1, 1, LANE), lambda j, s, c: (j, 0, 0)),
                pl.BlockSpec(memory_space=pl.ANY),
            ],
            out_specs=pl.BlockSpec((TOK_TILE, D_MODEL), out_map),
            scratch_shapes=[
                pltpu.VMEM((2, ncol, D_MODEL), BF16),
                pltpu.VMEM((ncol, D_MODEL), BF16),
                pltpu.VMEM((LANE, ncol), BF16),
                pltpu.SemaphoreType.DMA((3 * N_EXPERTS,)),
            ]),
        out_shape=jax.ShapeDtypeStruct((n_out * TOK_TILE, D_MODEL), F32),
        compiler_params=pltpu.CompilerParams(dimension_semantics=("arbitrary",),
                                             vmem_limit_bytes=VMEM_LIMIT),
        name="combine",
    )(seg, cnt, x2, aff_c, thr_row, eqa_rows, ye)


def _round_up(a, b):
    return -(-a // b) * b


def kernel(x_prompt, x_sample, meta, g_mix, w_in, b_gates, conv_w, g_q, g_k, lam_q1, lam_k1, lam_q2,
           lam_k2, g_da_out, g_ml_out, w_out, g_ffn, w_router, w_gate, w_up, w_down):
    assert x_prompt.shape[1:] == x_sample.shape[1:]
    groups = (x_prompt.shape[0], x_sample.shape[0])
    s = x_prompt.shape[1]
    lp = FRONT + s
    assert s % LANE == 0
    ntl = lp // TOK_TILE
    lamv = jnp.stack([lam_q1[0], lam_k1[0], lam_q2[0], lam_k2[0]])
    x2, xn, aff_c, aff_t = _mixer((x_prompt, x_sample), meta, lp, g_mix[0], w_in[0], b_gates[0],
                                  conv_w[0], g_q[0], g_k[0], lamv, g_da_out[0], g_ml_out[0],
                                  w_out[0], g_ffn[0], w_router[0])

    caps = [max(1, CAP_FACTOR * b * (N_META + s) // N_EXPERTS) for b in groups]
    bases = [sum(caps[:i]) for i in range(len(caps))]
    tail_start = sum(caps)
    stride = _round_up(tail_start, FFN_ROWS)
    infos = []
    tile0 = 0
    for b, cap in zip(groups, caps):
        nt = b * ntl
        thr, cnt, eqa, seg = _route(aff_t[:, tile0 * TOK_TILE:(tile0 + nt) * TOK_TILE], cap)
        ntp = cnt.shape[1]
        thr_row = jnp.pad(thr[:, 0], (0, LANE - N_EXPERTS), constant_values=2.0).reshape(1, LANE)
        eqa_rows = jnp.pad(eqa.T.astype(F32), ((0, 0), (0, LANE - N_EXPERTS))).reshape(ntp, 1, LANE)
        infos.append(dict(tile0=tile0, nt=nt, ntp=ntp, cap=cap, thr=thr, thr_row=thr_row, eqa=eqa,
                          eqa_rows=eqa_rows, seg=seg.reshape(-1), cnt=cnt.reshape(-1)))
        tile0 += nt

    xg = None
    for gi, info in enumerate(infos):
        xg = _dispatch(info["seg"], info["cnt"], xn, aff_t, info["thr"], info["eqa"], xg,
                       tile0=info["tile0"], nt=info["nt"], ntp=info["ntp"], base=bases[gi],
                       stride=stride, tail_start=tail_start)
    w_r = jnp.pad(w_router[0].astype(BF16), ((0, 0), (0, LANE - N_EXPERTS)))
    ye = _ffn(xg, w_gate[0].astype(BF16), w_up[0].astype(BF16), w_down[0].astype(BF16), w_r, stride)

    outs = []
    for gi, (b, info) in enumerate(zip(groups, infos)):
        y = _combine(info["seg"], info["cnt"], x2, aff_c, info["thr_row"], info["eqa_rows"], ye,
                     tile0=info["tile0"], nt=info["nt"], ntp=info["ntp"], ntl=ntl, base=bases[gi],
                     stride=stride)
        outs.append(y.reshape(b, s, D_MODEL))
    return tuple(outs)
```

```python
import functools
import math

import jax
import jax.numpy as jnp
from jax import lax
from jax.experimental import pallas as pl
from jax.experimental.pallas import tpu as pltpu

F32 = jnp.float32
BF16 = jnp.bfloat16
I32 = jnp.int32

D_MODEL = 1024
N_META = 16
LANE = 128
SUBLANE = 8
FRONT = LANE
PAD_ROWS = FRONT - N_META
RMS_EPS = 1e-6
DA_HEADS = 4
DA_WIDTH = 512
DA_QK_DIM = 64
ROPE_DIM = 16
ROPE_THETA = 500000.0
ML_HEADS = 4
ML_WIDTH = 512
ML_HEAD_DIM = 128
NEG_BIG = -1e30
N_GATES = 16
D_IN = 3600
D_IN_PAD = 3712
N_EXPERTS = 16
CAP_FACTOR = 2
D_FF = 2816
LAM_INIT = 0.8 - 0.6 * math.exp(-0.3 * 0)
LOG2E = math.log2(math.e)
MLSTM_CHUNK = LANE
MLSTM_UNROLL = 11
VMEM_LIMIT = 56 * 1024 * 1024


def _cparams(n_axes):
    return pltpu.CompilerParams(dimension_semantics=("arbitrary",) * n_axes,
                                vmem_limit_bytes=VMEM_LIMIT)


def _row_tile(lp):
    for t in (384, 256, 128):
        if lp % t == 0:
            return t
    raise ValueError(lp)


def _inproj_body(*refs, nt, nb_first, k):
    front_ref = refs[0]
    x_first, x_second = refs[1:1 + k], refs[1 + k:1 + 2 * k]
    (w_ref, gmix_ref, bg_ref, cos_ref, sa_ref, sb_ref, gq_ref, gk_ref,
     qk_ref, v_ref, mqk_ref, mv_ref, mo_ref, gt_ref, xp_ref) = refs[1 + 2 * k:]
    i = pl.program_id(0)
    in_first = i // nt < nb_first
    parts = [jnp.where(in_first, a[...], b[...]) for a, b in zip(x_first, x_second)]
    head = jnp.where(i % nt == 0, front_ref[...], parts[0])
    x = jnp.concatenate((head,) + tuple(parts[1:]), axis=0)
    xp_ref[...] = x
    h = x * lax.rsqrt(jnp.mean(x * x, axis=-1, keepdims=True) + RMS_EPS) * gmix_ref[...]
    z = jnp.dot(h.astype(BF16), w_ref[...], preferred_element_type=F32)
    lane = lax.broadcasted_iota(I32, (1, LANE), 1)
    lo = lane < DA_QK_DIM
    cos = cos_ref[...]
    sa = sa_ref[...]
    sb = sb_ref[...]

    def normrope(u, g, scale):
        sq = u * u
        s_lo = jnp.sum(jnp.where(lo, sq, 0.0), axis=-1, keepdims=True)
        s_hi = jnp.sum(jnp.where(lo, 0.0, sq), axis=-1, keepdims=True)
        ms = jnp.where(lo, s_lo, s_hi) * (1.0 / DA_QK_DIM)
        y = u * lax.rsqrt(ms + RMS_EPS) * g
        y = y * cos + pltpu.roll(y, 8, 1) * sa + pltpu.roll(y, LANE - 8, 1) * sb
        return y * scale

    for hh in range(DA_HEADS):
        c0 = hh * LANE
        qk_ref[:, c0:c0 + LANE] = normrope(z[:, c0:c0 + LANE], gq_ref[...],
                                           DA_QK_DIM ** -0.5 * LOG2E).astype(BF16)
        c1 = DA_WIDTH + hh * LANE
        qk_ref[:, c1:c1 + LANE] = normrope(z[:, c1:c1 + LANE], gk_ref[...], 1.0).astype(BF16)
    v_ref[...] = z[:, 1024:1536].astype(BF16)
    mqk_ref[...] = z[:, 1536:2560]
    mv_ref[...] = z[:, 2560:3072].astype(BF16)
    mo_ref[...] = z[:, 3072:3584]
    g = z[:, 3584:3712] + bg_ref[...]
    gt_ref[0] = g.T[:N_GATES, :]


def _inproj(xs, front, w_in_p, g_mix, bg, cos, sa, sb, gq, gk, lp):
    tm = _row_tile(lp)
    nt = lp // tm
    k = tm // LANE
    seq_blocks = (lp - FRONT) // LANE
    counts = [x.shape[0] for x in xs]
    nb = sum(counts)
    t_all = nb * lp
    row = lambda i: (i, 0)
    tab = lambda i: (i % nt, 0)
    const = lambda i: (0, 0)

    def x_block(first_seq, n_seq, r):
        def index(i):
            b = i // nt - first_seq
            blk = jnp.clip(b * seq_blocks + k * (i % nt) - 1 + r, 0, n_seq * seq_blocks - 1)
            return (jnp.where(b < 0, 0, jnp.where(b >= n_seq, n_seq * seq_blocks - 1, blk)), 0)
        return pl.BlockSpec((LANE, D_MODEL), index)

    x_specs = [x_block(0, counts[0], r) for r in range(k)]
    x_specs += [x_block(counts[0], counts[1], r) for r in range(k)]
    x_args = [xs[0].reshape(-1, D_MODEL)] * k + [xs[1].reshape(-1, D_MODEL)] * k
    return pl.pallas_call(
        functools.partial(_inproj_body, nt=nt, nb_first=counts[0], k=k),
        grid=(nb * nt,),
        in_specs=[pl.BlockSpec((LANE, D_MODEL), const)] + x_specs + [
            pl.BlockSpec((D_MODEL, D_IN_PAD), const),
            pl.BlockSpec((1, D_MODEL), const),
            pl.BlockSpec((1, LANE), const),
            pl.BlockSpec((tm, LANE), tab),
            pl.BlockSpec((tm, LANE), tab),
            pl.BlockSpec((tm, LANE), tab),
            pl.BlockSpec((1, LANE), const),
            pl.BlockSpec((1, LANE), const),
        ],
        out_specs=[
            pl.BlockSpec((tm, 1024), row),
            pl.BlockSpec((tm, 512), row),
            pl.BlockSpec((tm, 1024), row),
            pl.BlockSpec((tm, 512), row),
            pl.BlockSpec((tm, 512), row),
            pl.BlockSpec((1, N_GATES, tm), lambda i: (i // nt, 0, i % nt)),
            pl.BlockSpec((tm, D_MODEL), row),
        ],
        out_shape=[
            jax.ShapeDtypeStruct((t_all, 1024), BF16),
            jax.ShapeDtypeStruct((t_all, 512), BF16),
            jax.ShapeDtypeStruct((t_all, 1024), F32),
            jax.ShapeDtypeStruct((t_all, 512), BF16),
            jax.ShapeDtypeStruct((t_all, 512), F32),
            jax.ShapeDtypeStruct((nb, N_GATES, lp), F32),
            jax.ShapeDtypeStruct((t_all, D_MODEL), F32),
        ],
        compiler_params=_cparams(1),
        name="inproj",
    )(front, *x_args, w_in_p, g_mix, bg, cos, sa, sb, gq, gk)


def _attn_q_rows(lp):
    for t in (192, 128):
        if lp % t == 0:
            return t
    raise ValueError(lp)


def _attn_body(lam_ref, q_ref, k_ref, v_ref, bias_ref, g_ref, o_ref,
               va, s0, s1, m0, m1, e0, e1):
    lp = k_ref.shape[0]
    tq = s0.shape[1]
    nblk = lp // tq
    va[:, :LANE] = v_ref[...]
    va[:, LANE:] = jnp.where(lax.broadcasted_iota(I32, (lp, LANE), 1) == 0, 1.0, 0.0).astype(BF16)

    lv = lam_ref[...]
    lam = (jnp.exp(jnp.sum(lv[0:1] * lv[1:2], axis=-1, keepdims=True))
           - jnp.exp(jnp.sum(lv[2:3] * lv[3:4], axis=-1, keepdims=True)) + LAM_INIT)
    first = lax.broadcasted_iota(I32, (tq, LANE), 1) < DA_QK_DIM
    nt_dims = (((1,), (1,)), ((), ()))

    def rows_of(blk):
        return pl.ds(pl.multiple_of(blk * tq, tq), tq)

    def stage_a(blk, s_ref, m_ref):
        q = q_ref[rows_of(blk), :]
        zero = jnp.zeros_like(q)
        for j, qj in enumerate((jnp.where(first, q, zero), jnp.where(first, zero, q))):
            s = lax.dot_general(qj, k_ref[...], nt_dims, preferred_element_type=F32)
            s_pad = s[:, :FRONT] + bias_ref[...]
            s_ref[j, :, :FRONT] = s_pad
            s_ref[j, :, FRONT:] = s[:, FRONT:]
            m_ref[j] = jnp.maximum(jnp.max(s_pad, axis=-1, keepdims=True),
                                   jnp.max(s[:, FRONT:], axis=-1, keepdims=True))

    def stage_b(s_ref, m_ref, e_ref):
        for j in range(2):
            e_ref[j] = jnp.exp2(s_ref[j] - m_ref[j]).astype(BF16)

    def stage_c(blk, e_ref):
        r0 = jnp.dot(e_ref[0], va[...], preferred_element_type=F32)
        r1 = jnp.dot(e_ref[1], va[...], preferred_element_type=F32)
        o = r0[:, :LANE] / r0[:, LANE:LANE + 1] - lam * (r1[:, :LANE] / r1[:, LANE:LANE + 1])
        o = o * lax.rsqrt(jnp.mean(o * o, axis=-1, keepdims=True) + RMS_EPS) * g_ref[...]
        o_ref[rows_of(blk), :] = (o * (1.0 - LAM_INIT)).astype(BF16)

    stage_a(0, s0, m0)
    stage_a(min(1, nblk - 1), s1, m1)
    stage_b(s0, m0, e0)

    def step(t, carry):
        nxt = jnp.minimum(t + 1, nblk - 1)
        prv = t - 1

        @pl.when(t % 2 == 0)
        def _even():
            stage_a(nxt, s1, m1)
            stage_b(s0, m0, e0)
            stage_c(prv, e1)

        @pl.when(t % 2 == 1)
        def _odd():
            stage_a(nxt, s0, m0)
            stage_b(s1, m1, e1)
            stage_c(prv, e0)

        return carry

    lax.fori_loop(1, nblk, step, 0)
    stage_c(nblk - 1, e0 if nblk % 2 == 1 else e1)


def _attention(lamv, qk, v, bias, g_out, nb, lp):
    tq = _attn_q_rows(lp)
    blk = lambda col: pl.BlockSpec((lp, LANE), col)
    return pl.pallas_call(
        _attn_body,
        grid=(nb, DA_HEADS),
        in_specs=[
            pl.BlockSpec((4, DA_QK_DIM), lambda b, h: (0, 0)),
            blk(lambda b, h: (b, h)),
            blk(lambda b, h: (b, DA_HEADS + h)),
            blk(lambda b, h: (b, h)),
            pl.BlockSpec((1, FRONT), lambda b, h: (0, 0)),
            pl.BlockSpec((1, LANE), lambda b, h: (0, 0)),
        ],
        out_specs=blk(lambda b, h: (b, h)),
        out_shape=jax.ShapeDtypeStruct((nb * lp, DA_WIDTH), BF16),
        scratch_shapes=[
            pltpu.VMEM((lp, 2 * LANE), BF16),
            pltpu.VMEM((2, tq, lp), F32), pltpu.VMEM((2, tq, lp), F32),
            pltpu.VMEM((2, tq, 1), F32), pltpu.VMEM((2, tq, 1), F32),
            pltpu.VMEM((2, tq, lp), BF16), pltpu.VMEM((2, tq, lp), BF16),
        ],
        compiler_params=_cparams(2),
        name="diff_attention",
    )(lamv, qk, qk, v, bias, g_out)


def _log_sigmoid(x):
    return jnp.minimum(x, 0.0) - jnp.log(1.0 + jnp.exp(-jnp.abs(x)))


def _lane_scan(x, forward, op, identity):
    lane = lax.broadcasted_iota(I32, x.shape, 1)
    sh = 1
    while sh < LANE:
        if forward:
            x = op(x, jnp.where(lane >= sh, pltpu.roll(x, sh, 1), identity))
        else:
            x = op(x, jnp.where(lane < LANE - sh, pltpu.roll(x, LANE - sh, 1), identity))
        sh *= 2
    return x


def _mlstm_body(q_ref, k_ref, v_ref, og_ref, gif_ref, gff_ref, gib_ref, gfb_ref,
                cwq_ref, cwk_ref, gml_ref, o_ref,
                qs, ks, va, hf, hb, rowv0, rowv1, uvec0, uvec1, scal0, scal1, colv, r1, kvs):
    gates = (gif_ref, gff_ref, gib_ref, gfb_ref)
    rowv, uvec, scal = (rowv0, rowv1), (uvec0, uvec1), (scal0, scal1)
    lp = q_ref.shape[0]
    nc = lp // MLSTM_CHUNK
    c = MLSTM_CHUNK
    row = lax.broadcasted_iota(I32, (lp, 1), 0)

    row_c = lax.broadcasted_iota(I32, (c, 1), 0)

    def conv_chunk(ci, carry):
        start = pl.multiple_of(ci * c, c)
        before = pl.ds(pl.multiple_of(jnp.maximum(start - SUBLANE, 0), SUBLANE), SUBLANE)
        after = pl.ds(pl.multiple_of(jnp.minimum(start + c, lp - SUBLANE), SUBLANE), SUBLANE)
        for src, w_ref, dst, scale in ((q_ref, cwq_ref, qs, ML_HEAD_DIM ** -0.5),
                                       (k_ref, cwk_ref, ks, 1.0)):
            x = src[pl.ds(start, c), :]
            w = w_ref[...]
            last_before = jnp.where(ci > 0, src[before, :][SUBLANE - 1:SUBLANE, :], 0.0)
            first_after = jnp.where(ci < nc - 1, src[after, :][0:1, :], 0.0)
            x_prev = jnp.where(row_c == 0, last_before, pltpu.roll(x, 1, 0))
            x_next = jnp.where(row_c == c - 1, first_after, pltpu.roll(x, c - 1, 0))
            y = x_prev * w[0:1] + x * w[1:2] + x_next * w[2:3]
            dst[pl.ds(start, c), :] = (y * jax.nn.sigmoid(y) * scale).astype(BF16)
        return carry

    lax.fori_loop(0, nc, conv_chunk, 0)
    va[:, :LANE] = v_ref[...]
    va[:, LANE:] = jnp.ones((lp, LANE), BF16)

    r_i = lax.broadcasted_iota(I32, (c, c), 0)
    c_i = lax.broadcasted_iota(I32, (c, c), 1)
    ncp = gates[0].shape[2]
    lane_c = lax.broadcasted_iota(I32, (ncp, c), 1)
    pos = lax.broadcasted_iota(I32, (ncp, c), 0) * c + lane_c
    valid = pos >= PAD_ROWS

    for d, forward in enumerate((True, False)):
        li = jnp.where(valid, gates[2 * d][0, 0], NEG_BIG)
        lf = jnp.where(valid, _log_sigmoid(gates[2 * d + 1][0, 0]), 0.0)
        cum = _lane_scan(lf, forward, jnp.add, 0.0)
        u = li - cum
        cm = _lane_scan(u, forward, jnp.maximum, -jnp.inf)
        b_last = cum[:, c - 1:c] if forward else cum[:, 0:1]
        a = b_last + u
        a_max = jnp.max(a, axis=-1, keepdims=True)
        rowv[d][...] = jnp.zeros_like(rowv[d])
        for r, vec in enumerate((cm, jnp.exp(a - a_max), cum)):
            rowv[d][pl.ds(r, ncp, stride=SUBLANE), :] = vec
        uvec[d][...] = u
        scal[d][0] = jnp.broadcast_to(b_last, (ncp, c))
        scal[d][1] = jnp.broadcast_to(a_max, (ncp, c))

    def rows_of(ci):
        return pl.ds(pl.multiple_of(ci * c, c), c)

    def intra(ci, d, slot, qc, kc, k_t, vc):
        forward = d == 0
        tile = rowv[d][pl.ds(pl.multiple_of(ci * SUBLANE, SUBLANE), SUBLANE), :]

        def column(r):
            return jnp.broadcast_to(tile[r:r + 1, :], (c, c)).T

        cm_c = column(0)
        cum_c = column(2)
        colv[d, 0, slot] = cm_c
        colv[d, 1, slot] = cum_c
        mask = (c_i <= r_i) if forward else (c_i >= r_i)
        w = jnp.exp(jnp.where(mask, uvec[d][pl.ds(ci, 1), :] - cm_c, -jnp.inf))
        s = lax.dot_general(qc, kc, (((1,), (1,)), ((), ())), preferred_element_type=F32) * w
        r1[d, slot] = jnp.dot(s.astype(BF16), vc, preferred_element_type=F32)
        kw_t = k_t * tile[1:2, :]
        kvs[d, slot] = jnp.dot(kw_t.astype(BF16), vc, preferred_element_type=F32)

    def inter_step(ci, d, slot, state, m_prev):
        cm_c = colv[d, 0, slot]
        cum_c = colv[d, 1, slot]
        m_loc = cum_c + cm_c
        inter = cum_c + m_prev
        m_t = jnp.maximum(m_loc, inter)
        r2 = jnp.dot(qs[rows_of(ci), :], state.astype(BF16), preferred_element_type=F32)
        w_intra = jnp.exp(m_loc - m_t)
        w_inter = jnp.exp(inter - m_t)
        r1v = r1[d, slot]
        num = w_intra * r1v[:, :LANE] + w_inter * r2[:, :LANE]
        den = w_intra * r1v[:, LANE:] + w_inter * r2[:, LANE:]
        h_out = num / jnp.maximum(jnp.abs(den), jnp.exp(-m_t))
        b_last = scal[d][0, pl.ds(ci, 1), :]
        a_max = scal[d][1, pl.ds(ci, 1), :]
        m_new = jnp.maximum(b_last + m_prev, a_max)
        keep = jnp.exp(b_last + m_prev - m_new)
        add = jnp.exp(a_max - m_new)
        state_new = (jnp.concatenate([keep, keep], axis=1) * state
                     + jnp.concatenate([add, add], axis=1) * kvs[d, slot])
        return h_out, state_new, m_new

    def intra_both(ci, carry):
        rows = rows_of(ci)
        qc = qs[rows, :]
        kc = ks[rows, :]
        vc = va[rows, :]
        k_t = kc.astype(F32).T
        intra(ci, 0, ci, qc, kc, k_t, vc)
        intra(ci, 1, ci, qc, kc, k_t, vc)
        return carry

    lax.fori_loop(0, nc, intra_both, 0, unroll=MLSTM_UNROLL)

    def body(i, carry):
        sf, mf, sb, mb = carry
        h_f, sf, mf = inter_step(i, 0, i, sf, mf)
        hf[rows_of(i), :] = h_f
        j = nc - 1 - i
        h_b, sb, mb = inter_step(j, 1, j, sb, mb)
        hb[rows_of(j), :] = h_b
        return sf, mf, sb, mb

    z_state = jnp.zeros((ML_HEAD_DIM, 2 * LANE), F32)
    z_m = jnp.zeros((1, LANE), F32)
    lax.fori_loop(0, nc, body, (z_state, z_m, z_state, z_m))

    hs = hf[...] + hb[...]
    y = hs * lax.rsqrt(jnp.mean(hs * hs, axis=-1, keepdims=True) + RMS_EPS) * gml_ref[...]
    o_ref[...] = (y * jax.nn.sigmoid(og_ref[...])).astype(BF16)


def _mlstm(mqk, mv, mo, gt, conv_w, g_ml, nb, lp):
    blk = lambda col: pl.BlockSpec((lp, LANE), col)
    c = MLSTM_CHUNK
    nc = lp // c
    ncp = _round_up(nc, SUBLANE)
    gt4 = jnp.pad(gt.reshape(nb, N_GATES, nc, c), ((0, 0), (0, 0), (0, ncp - nc), (0, 0)))
    gate = lambda g: pl.BlockSpec((1, 1, ncp, c), lambda b, h: (b, g * ML_HEADS + h, 0, 0))
    vec = lambda rows, cols=LANE: pltpu.VMEM((rows, cols), F32)
    return pl.pallas_call(
        _mlstm_body,
        grid=(nb, ML_HEADS),
        in_specs=[
            blk(lambda b, h: (b, h)),
            blk(lambda b, h: (b, ML_HEADS + h)),
            blk(lambda b, h: (b, h)),
            pl.BlockSpec((lp, LANE), lambda b, h: (b, h), pipeline_mode=pl.Buffered(1)),
            gate(0), gate(1), gate(2), gate(3),
            pl.BlockSpec((3, LANE), lambda b, h: (0, h)),
            pl.BlockSpec((3, LANE), lambda b, h: (0, ML_HEADS + h)),
            pl.BlockSpec((1, LANE), lambda b, h: (0, 0)),
        ],
        out_specs=blk(lambda b, h: (b, h)),
        out_shape=jax.ShapeDtypeStruct((nb * lp, ML_WIDTH), BF16),
        scratch_shapes=[
            pltpu.VMEM((lp, LANE), BF16),
            pltpu.VMEM((lp, LANE), BF16),
            pltpu.VMEM((lp, 2 * LANE), BF16),
            vec(lp), vec(lp),
            vec(ncp * SUBLANE), vec(ncp * SUBLANE),
            vec(ncp), vec(ncp),
            pltpu.VMEM((2, ncp, c), F32), pltpu.VMEM((2, ncp, c), F32),
            pltpu.VMEM((2, 2, nc, c, LANE), F32),
            pltpu.VMEM((2, nc, c, 2 * LANE), F32),
            pltpu.VMEM((2, nc, ML_HEAD_DIM, 2 * LANE), F32),
        ],
        compiler_params=_cparams(2),
        name="bidir_mlstm",
    )(mqk, mqk, mv, mo, gt4, gt4, gt4, gt4, conv_w, conv_w, g_ml)


def _outproj_body(x_ref, oa_ref, hm_ref, wa_ref, wm_ref, gffn_ref, wr_ref, valid_ref,
                  x2_ref, xn_ref, affc_ref, afft_ref):
    x2 = (x_ref[...]
          + jnp.dot(oa_ref[...], wa_ref[...], preferred_element_type=F32)
          + jnp.dot(hm_ref[...], wm_ref[...], preferred_element_type=F32))
    x2_ref[...] = x2
    xn = x2 * lax.rsqrt(jnp.mean(x2 * x2, axis=-1, keepdims=True) + RMS_EPS) * gffn_ref[...]
    xn_ref[...] = xn.astype(BF16)
    xh = xn.astype(BF16)
    xl = (xn - xh.astype(F32)).astype(BF16)
    r_h = jnp.dot(xh, wr_ref[...], preferred_element_type=F32)
    logits = (r_h[:, :LANE] + r_h[:, LANE:]
              + jnp.dot(xl, wr_ref[:, :LANE], preferred_element_type=F32))
    lane = lax.broadcasted_iota(I32, logits.shape, 1)
    real = lane < N_EXPERTS
    logits = jnp.where(real, logits, -jnp.inf)
    e = jnp.exp(logits - jnp.max(logits, axis=-1, keepdims=True))
    aff = e / jnp.sum(e, axis=-1, keepdims=True)
    aff = jnp.where(real & (valid_ref[...] > 0.0), aff, -1.0)
    affc_ref[...] = aff
    afft_ref[...] = aff.T[:N_EXPERTS, :]


def _outproj(xp, oa, hm, w_a, w_m, g_ffn, w_r, valid, nb, lp):
    tm = _row_tile(lp)
    nt = lp // tm
    t_all = nb * lp
    row = lambda i: (i, 0)
    const = lambda i: (0, 0)
    return pl.pallas_call(
        _outproj_body,
        grid=(nb * nt,),
        in_specs=[
            pl.BlockSpec((tm, D_MODEL), row),
            pl.BlockSpec((tm, DA_WIDTH), row),
            pl.BlockSpec((tm, ML_WIDTH), row),
            pl.BlockSpec((DA_WIDTH, D_MODEL), const),
            pl.BlockSpec((ML_WIDTH, D_MODEL), const),
            pl.BlockSpec((1, D_MODEL), const),
            pl.BlockSpec((D_MODEL, 2 * LANE), const),
            pl.BlockSpec((tm, 1), lambda i: (i % nt, 0)),
        ],
        out_specs=[
            pl.BlockSpec((tm, D_MODEL), row),
            pl.BlockSpec((tm, D_MODEL), row),
            pl.BlockSpec((tm, LANE), row),
            pl.BlockSpec((N_EXPERTS, tm), lambda i: (0, i)),
        ],
        out_shape=[
            jax.ShapeDtypeStruct((t_all, D_MODEL), F32),
            jax.ShapeDtypeStruct((t_all, D_MODEL), BF16),
            jax.ShapeDtypeStruct((t_all, LANE), F32),
            jax.ShapeDtypeStruct((N_EXPERTS, t_all), F32),
        ],
        compiler_params=_cparams(1),
        name="outproj_router",
    )(xp, oa, hm, w_a, w_m, g_ffn, w_r, valid)


def _rope_tables(lp):
    pos = jnp.arange(lp, dtype=F32) - float(PAD_ROWS)
    inv = ROPE_THETA ** (-jnp.arange(0, ROPE_DIM, 2, dtype=F32) / ROPE_DIM)
    ang = pos[:, None] * inv[None, :]
    cos8, sin8 = jnp.cos(ang), jnp.sin(ang)
    half = ROPE_DIM // 2
    one_block = jnp.ones((lp, DA_QK_DIM - ROPE_DIM), F32)
    zero8 = jnp.zeros((lp, half), F32)
    zero_block = jnp.zeros((lp, DA_QK_DIM - ROPE_DIM), F32)
    cos_c = jnp.concatenate([cos8, cos8, one_block], axis=1)
    sa_c = jnp.concatenate([zero8, sin8, zero_block], axis=1)
    sb_c = jnp.concatenate([-sin8, zero8, zero_block], axis=1)
    tile2 = lambda a: jnp.concatenate([a, a], axis=1)
    return tile2(cos_c), tile2(sa_c), tile2(sb_c)


def _mixer(xs, meta, lp, g_mix, w_in, b_gates, conv_w, g_q, g_k, lamv, g_da_out, g_ml_out, w_out,
           g_ffn, w_router):
    nb = sum(x.shape[0] for x in xs)
    w_in_p = jnp.pad(w_in.astype(BF16), ((0, 0), (0, D_IN_PAD - D_IN)))
    bg = jnp.pad(b_gates, (0, LANE - N_GATES)).reshape(1, LANE)
    cos, sa, sb = _rope_tables(lp)
    gq = jnp.tile(g_q, 2).reshape(1, LANE)
    gk = jnp.tile(g_k, 2).reshape(1, LANE)
    front = jnp.pad(meta.astype(F32), ((PAD_ROWS, 0), (0, 0)))
    qk, v, mqk, mv, mo, gt, xp = _inproj(xs, front, w_in_p, g_mix.reshape(1, D_MODEL), bg, cos,
                                         sa, sb, gq, gk, lp)
    bias = jnp.where(jnp.arange(FRONT) < PAD_ROWS, NEG_BIG, 0.0).astype(F32).reshape(1, FRONT)
    oa = _attention(lamv, qk, v, bias, g_da_out.reshape(1, LANE), nb, lp)
    hm = _mlstm(mqk, mv, mo, gt, conv_w, g_ml_out.reshape(1, LANE), nb, lp)
    w_o = w_out.astype(BF16)
    w_r = jnp.pad(w_router, ((0, 0), (0, LANE - N_EXPERTS)))
    w_r_hi = w_r.astype(BF16)
    w_r = jnp.concatenate([w_r_hi, (w_r - w_r_hi.astype(F32)).astype(BF16)], axis=1)
    valid = (jnp.arange(lp) >= PAD_ROWS).astype(F32).reshape(lp, 1)
    return _outproj(xp, oa, hm, w_o[:DA_WIDTH], w_o[DA_WIDTH:], g_ffn.reshape(1, D_MODEL), w_r,
                    valid, nb, lp)


TOK_TILE = LANE
WIN = 32
SUB = SUBLANE
ROW_ALIGN = 16
assert WIN % ROW_ALIGN == 0
CWIN = WIN + ROW_ALIGN
FFN_ROWS = 512
FF_CHUNK = 256


def _route_body(aff_ref, thr_ref, cnt_ref, eqa_ref, seg_ref, *, cap, nt):
    a = aff_ref[...]
    capf = float(cap)

    def search(i, v):
        trial = v | lax.shift_left(jnp.int32(1), 30 - i)
        c = jnp.sum((a >= pltpu.bitcast(trial, F32)).astype(F32), axis=-1, keepdims=True)
        return jnp.where(c >= capf, trial, v)

    thr_bits = lax.fori_loop(0, 31, search, jnp.zeros((N_EXPERTS, 1), I32))
    thr = pltpu.bitcast(thr_bits, F32)
    gt = a > thr
    eq = a == thr
    need = capf - jnp.sum(gt.astype(F32), axis=-1, keepdims=True)
    thr_ref[...] = jnp.broadcast_to(thr, (N_EXPERTS, LANE))

    slab = LANE * TOK_TILE
    ind = (lax.broadcasted_iota(I32, (slab, LANE), 0) // TOK_TILE
           == lax.broadcasted_iota(I32, (slab, LANE), 1)).astype(BF16)
    upper = (lax.broadcasted_iota(I32, (LANE, LANE), 0)
             < lax.broadcasted_iota(I32, (LANE, LANE), 1)).astype(BF16)
    eq_base = jnp.zeros((N_EXPERTS, 1), F32)
    seg_base = jnp.zeros((N_EXPERTS, 1), F32)
    for s0 in range(0, nt, LANE):
        n_t = min(LANE, nt - s0)
        cols = slice(s0 * TOK_TILE, (s0 + n_t) * TOK_TILE)
        gt_cnt = jnp.dot(gt[:, cols].astype(BF16), ind[:n_t * TOK_TILE], preferred_element_type=F32)
        eq_cnt = jnp.dot(eq[:, cols].astype(BF16), ind[:n_t * TOK_TILE], preferred_element_type=F32)
        eq_before = eq_base + jnp.dot(eq_cnt.astype(BF16), upper, preferred_element_type=F32)
        eq_allow = jnp.clip(need - eq_before, 0.0, eq_cnt)
        cnt = gt_cnt + eq_allow
        seg = seg_base + jnp.dot(cnt.astype(BF16), upper, preferred_element_type=F32)
        eq_base = eq_base + jnp.sum(eq_cnt, axis=-1, keepdims=True)
        seg_base = seg_base + jnp.sum(cnt, axis=-1, keepdims=True)
        cnt_ref[:, s0:s0 + LANE] = cnt.astype(I32)
        eqa_ref[:, s0:s0 + LANE] = eq_allow.astype(I32)
        seg_ref[:, s0:s0 + LANE] = seg.astype(I32)


def _route(aff_t, cap):
    tg = aff_t.shape[1]
    nt = tg // TOK_TILE
    ntp = -(-nt // LANE) * LANE
    tab = jax.ShapeDtypeStruct((N_EXPERTS, ntp), I32)
    return pl.pallas_call(
        functools.partial(_route_body, cap=cap, nt=nt),
        out_shape=[jax.ShapeDtypeStruct((N_EXPERTS, LANE), F32), tab, tab, tab],
        compiler_params=pltpu.CompilerParams(vmem_limit_bytes=VMEM_LIMIT),
        name="route",
    )(aff_t)


def _window_rows(cnt, p):
    return jnp.clip(cnt - p * WIN, 0, WIN)


def _dispatch_body(seg_sm, cnt_sm, xn_ref, aff_ref, thr_ref, eqa_ref, *rest,
                   nt, ntp, base, stride, tail_start):
    xg_ref, pbuf, stage, stage_x, zbuf, sem = rest[-6:]
    j = pl.program_id(0)

    if tail_start is not None:
        @pl.when(j == 0)
        def _zero_tail():
            zbuf[...] = jnp.zeros_like(zbuf)
            copies = [pltpu.make_async_copy(
                zbuf, xg_ref.at[pl.ds((e * stride + tail_start) * SUB, zbuf.shape[0])],
                sem.at[2 * N_EXPERTS + e])
                for e in range(N_EXPERTS)]
            for c in copies:
                c.start()
            for c in copies:
                c.wait()

    a = aff_ref[...]
    thr = thr_ref[...]
    lane = lax.broadcasted_iota(I32, (N_EXPERTS, LANE), 1)
    eqa = jnp.sum(jnp.where(lane == j % LANE, eqa_ref[...], 0).astype(F32), axis=-1, keepdims=True)
    upper = (lax.broadcasted_iota(I32, (LANE, LANE), 0)
             < lax.broadcasted_iota(I32, (LANE, LANE), 1)).astype(BF16)
    gt = a > thr
    eq = a == thr
    eq_rank = jnp.dot(eq.astype(BF16), upper, preferred_element_type=F32)
    sel = gt | (eq & (eq_rank < eqa))
    rank = jnp.dot(sel.astype(BF16), upper, preferred_element_type=F32)
    rank = jnp.where(sel, rank, -1.0)

    n_pass = (functools.reduce(jnp.maximum, [cnt_sm[e * ntp + j] for e in range(N_EXPERTS)])
              + WIN - 1) // WIN
    win_row = lax.broadcasted_iota(I32, (WIN, LANE), 0).astype(F32)

    def gather_rows(p, dst):
        off = (p * WIN).astype(F32) if not isinstance(p, int) else float(p * WIN)
        for e in range(N_EXPERTS):
            r_e = jnp.broadcast_to(rank[e:e + 1, :], (WIN, LANE))
            pbuf[e * WIN:(e + 1) * WIN, :] = (r_e == win_row + off).astype(BF16)
        g = jnp.dot(pbuf[...], xn_ref[...], preferred_element_type=F32)
        for s in range(SUB):
            dst[pl.ds(s, N_EXPERTS * WIN, stride=SUB), :] = g[:, s * LANE:(s + 1) * LANE]

    def copy(t, e, p, src, sem_set):
        n = _window_rows(cnt_sm[e * ntp + t], p) * SUB
        row = pl.multiple_of((e * stride + base + seg_sm[e * ntp + t] + p * WIN) * SUB, SUB)
        return pltpu.make_async_copy(src.at[pl.ds(e * WIN * SUB, n)], xg_ref.at[pl.ds(row, n)],
                                     sem.at[sem_set * N_EXPERTS + e])

    def for_experts(t, p, fn):
        for e in range(N_EXPERTS):
            @pl.when(cnt_sm[e * ntp + t] > p * WIN)
            def _():
                fn(e)

    slot = j % 2
    mine = stage.at[slot]

    @pl.when(j >= 2)
    def _reuse():
        for_experts(j - 2, 0, lambda e: copy(j - 2, e, 0, mine, slot).wait())

    gather_rows(0, mine)
    for_experts(j, 0, lambda e: copy(j, e, 0, mine, slot).start(priority=e % 2))

    def extra_pass(p, carry):
        gather_rows(p, stage_x)
        for_experts(j, p, lambda e: copy(j, e, p, stage_x, 2).start())
        for_experts(j, p, lambda e: copy(j, e, p, stage_x, 2).wait())
        return carry

    lax.fori_loop(1, n_pass, extra_pass, 0)

    @pl.when(j == nt - 1)
    def _drain():
        if nt >= 2:
            for_experts(j - 1, 0, lambda e: copy(j - 1, e, 0, stage.at[1 - slot], 1 - slot).wait())
        for_experts(j, 0, lambda e: copy(j, e, 0, mine, slot).wait())


def _dispatch(seg, cnt, xn, aff_t, thr, eqa, xg, *, tile0, nt, ntp, base, stride, tail_start):
    first = xg is None
    zero_tail = first and stride > tail_start
    tail_rows = (stride - tail_start) * SUB if zero_tail else SUB
    kern = functools.partial(_dispatch_body, nt=nt, ntp=ntp, base=base, stride=stride,
                             tail_start=tail_start if zero_tail else None)
    in_specs = [
        pl.BlockSpec((TOK_TILE, D_MODEL), lambda j, s, c: (tile0 + j, 0)),
        pl.BlockSpec((N_EXPERTS, TOK_TILE), lambda j, s, c: (0, tile0 + j)),
        pl.BlockSpec((N_EXPERTS, LANE), lambda j, s, c: (0, 0)),
        pl.BlockSpec((N_EXPERTS, LANE), lambda j, s, c: (0, j // LANE)),
    ]
    args = [seg, cnt, xn, aff_t, thr, eqa]
    aliases = {}
    if not first:
        in_specs.append(pl.BlockSpec(memory_space=pl.ANY))
        args.append(xg)
        aliases = {len(args) - 1: 0}
    return pl.pallas_call(
        kern,
        grid_spec=pltpu.PrefetchScalarGridSpec(
            num_scalar_prefetch=2,
            grid=(nt,),
            in_specs=in_specs,
            out_specs=pl.BlockSpec(memory_space=pl.ANY),
            scratch_shapes=[
                pltpu.VMEM((N_EXPERTS * WIN, TOK_TILE), BF16),
                pltpu.VMEM((2, N_EXPERTS * WIN * SUB, LANE), F32),
                pltpu.VMEM((N_EXPERTS * WIN * SUB, LANE), F32),
                pltpu.VMEM((tail_rows, LANE), F32),
                pltpu.SemaphoreType.DMA((3 * N_EXPERTS,)),
            ]),
        out_shape=jax.ShapeDtypeStruct((N_EXPERTS * stride * SUB, LANE), F32),
        input_output_aliases=aliases,
        compiler_params=pltpu.CompilerParams(dimension_semantics=("arbitrary",),
                                             vmem_limit_bytes=VMEM_LIMIT, has_side_effects=True),
        name="dispatch",
    )(*args)


def _ffn_body(xg_ref, wg_ref, wu_ref, wd_ref, wr_ref, o_ref, xb):
    e = pl.program_id(0)
    tm = xb.shape[0]
    for s in range(SUB):
        xb[:, s * LANE:(s + 1) * LANE] = xg_ref[pl.ds(s, tm, stride=SUB), :].astype(BF16)
    x = xb[...]
    logits = jnp.dot(x, wr_ref[...], preferred_element_type=F32)
    lane = lax.broadcasted_iota(I32, logits.shape, 1)
    logits = jnp.where(lane < N_EXPERTS, logits, -jnp.inf)
    ex = jnp.exp(logits - jnp.max(logits, axis=-1, keepdims=True))
    aff = ex / jnp.sum(ex, axis=-1, keepdims=True)
    gate = jnp.sum(jnp.where(lane == e, aff, 0.0), axis=-1, keepdims=True)
    acc = jnp.zeros((tm, D_MODEL), F32)
    for f in range(0, D_FF, FF_CHUNK):
        g = jnp.dot(x, wg_ref[0, :, f:f + FF_CHUNK], preferred_element_type=F32)
        u = jnp.dot(x, wu_ref[0, :, f:f + FF_CHUNK], preferred_element_type=F32)
        h = (g * jax.nn.sigmoid(g) * u).astype(BF16)
        acc = acc + jnp.dot(h, wd_ref[0, f:f + FF_CHUNK, :], preferred_element_type=F32)
    o_ref[...] = (acc * gate).astype(BF16)


def _ffn(xg, wg, wu, wd, w_r, stride):
    nr = stride // FFN_ROWS
    blk = pl.BlockSpec((FFN_ROWS * SUB, LANE), lambda e, i: (e * nr + i, 0))
    return pl.pallas_call(
        _ffn_body,
        grid=(N_EXPERTS, nr),
        in_specs=[
            blk,
            pl.BlockSpec((1, D_MODEL, D_FF), lambda e, i: (e, 0, 0)),
            pl.BlockSpec((1, D_MODEL, D_FF), lambda e, i: (e, 0, 0)),
            pl.BlockSpec((1, D_FF, D_MODEL), lambda e, i: (e, 0, 0)),
            pl.BlockSpec((D_MODEL, LANE), lambda e, i: (0, 0)),
        ],
        out_specs=pl.BlockSpec((FFN_ROWS, D_MODEL), lambda e, i: (e * nr + i, 0)),
        out_shape=jax.ShapeDtypeStruct((N_EXPERTS * stride, D_MODEL), BF16),
        scratch_shapes=[pltpu.VMEM((FFN_ROWS, D_MODEL), BF16)],
        compiler_params=_cparams(2),
        name="expert_ffn",
    )(xg, wg, wu, wd, w_r)


def _combine_body(seg_sm, cnt_sm, x2_ref, aff_ref, thr_ref, eqa_ref, ye_ref, o_ref,
                  ybuf, ybuf_x, rep, sem, *, nt, ntp, ntl, base, stride):
    j = pl.program_id(0)
    ncol = N_EXPERTS * CWIN

    def first_row(t, e):
        return e * stride + base + seg_sm[e * ntp + t]

    def copy(t, e, p, dst, sem_set):
        r0 = first_row(t, e) + p * WIN
        off = r0 & (ROW_ALIGN - 1)
        n = pl.multiple_of(
            (off + _window_rows(cnt_sm[e * ntp + t], p) + ROW_ALIGN - 1) & ~(ROW_ALIGN - 1), ROW_ALIGN)
        return pltpu.make_async_copy(ye_ref.at[pl.ds(pl.multiple_of(r0 - off, ROW_ALIGN), n)],
                                     dst.at[pl.ds(e * CWIN, n)], sem.at[sem_set * N_EXPERTS + e])

    def for_experts(t, p, fn):
        for e in range(N_EXPERTS):
            @pl.when(cnt_sm[e * ntp + t] > p * WIN)
            def _():
                fn(e)

    @pl.when(j == 0)
    def _init():
        ybuf[...] = jnp.zeros_like(ybuf)
        ybuf_x[...] = jnp.zeros_like(ybuf_x)
        col_of = lax.broadcasted_iota(I32, (LANE, ncol), 1)
        blk_lo = lax.broadcasted_iota(I32, (LANE, ncol), 0) * CWIN
        rep[...] = ((col_of >= blk_lo) & (col_of < blk_lo + CWIN)).astype(BF16)

    nxt = jnp.minimum(j + 1, nt - 1)

    @pl.when((j + 1 < nt) & (nxt % ntl > 0))
    def _prefetch():
        for_experts(nxt, 0, lambda e: copy(nxt, e, 0, ybuf.at[nxt % 2], nxt % 2).start(priority=e % 2))

    @pl.when(j % ntl > 0)
    def _tile():
        a = aff_ref[...]
        thr = thr_ref[...]
        eqa = eqa_ref[0]
        lower = (lax.broadcasted_iota(I32, (TOK_TILE, TOK_TILE), 1)
                 < lax.broadcasted_iota(I32, (TOK_TILE, TOK_TILE), 0)).astype(BF16)
        gt = a > thr
        eq = a == thr
        eq_rank = jnp.dot(lower, eq.astype(BF16), preferred_element_type=F32)
        sel = gt | (eq & (eq_rank < eqa))
        rank = jnp.dot(lower, sel.astype(BF16), preferred_element_type=F32)
        rank = jnp.where(sel, rank, -1.0).astype(BF16)
        col = lax.broadcasted_iota(I32, (1, ncol), 1)
        rank_rep = jnp.dot(rank, rep[...], preferred_element_type=F32)
        col_slot = col
        for e in range(N_EXPERTS):
            col_slot = jnp.where(col >= e * CWIN,
                                 col - (e * CWIN + (first_row(j, e) & (ROW_ALIGN - 1))), col_slot)
        col_slot = col_slot.astype(F32)
        n_pass = (functools.reduce(jnp.maximum, [cnt_sm[e * ntp + j] for e in range(N_EXPERTS)])
                  + WIN - 1) // WIN

        def onehot(p):
            lo = p * WIN if isinstance(p, int) else (p * WIN).astype(F32)
            in_pass = (rank_rep >= lo) & (rank_rep < lo + WIN)
            return (in_pass & (rank_rep - lo == col_slot)).astype(BF16)

        slot = j % 2
        for_experts(j, 0, lambda e: copy(j, e, 0, ybuf.at[slot], slot).wait())
        acc = x2_ref[...] + jnp.dot(onehot(0), ybuf[slot], preferred_element_type=F32)

        def extra_pass(p, acc):
            for_experts(j, p, lambda e: copy(j, e, p, ybuf_x, 2).start())
            hot = onehot(p)
            for_experts(j, p, lambda e: copy(j, e, p, ybuf_x, 2).wait())
            return acc + jnp.dot(hot, ybuf_x[...], preferred_element_type=F32)

        o_ref[...] = lax.fori_loop(1, n_pass, extra_pass, acc)


def _combine(seg, cnt, x2, aff_c, thr_row, eqa_rows, ye, *, tile0, nt, ntp, ntl, base, stride):
    ncol = N_EXPERTS * CWIN
    n_out = nt // ntl * (ntl - 1)
    out_map = lambda j, s, c: (j // ntl * (ntl - 1) + jnp.maximum(j % ntl - 1, 0), 0)
    return pl.pallas_call(
        functools.partial(_combine_body, nt=nt, ntp=ntp, ntl=ntl, base=base, stride=stride),
        grid_spec=pltpu.PrefetchScalarGridSpec(
            num_scalar_prefetch=2,
            grid=(nt,),
            in_specs=[
                pl.BlockSpec((TOK_TILE, D_MODEL), lambda j, s, c: (tile0 + j, 0)),
                pl.BlockSpec((TOK_TILE, LANE), lambda j, s, c: (tile0 + j, 0)),
                pl.BlockSpec((1, LANE), lambda j, s, c: (0, 0)),
                pl.BlockSpec((1, 1, LANE), lambda j, s, c: (j, 0, 0)),
                pl.BlockSpec(memory_space=pl.ANY),
            ],
            out_specs=pl.BlockSpec((TOK_TILE, D_MODEL), out_map),
            scratch_shapes=[
                pltpu.VMEM((2, ncol, D_MODEL), BF16),
                pltpu.VMEM((ncol, D_MODEL), BF16),
                pltpu.VMEM((LANE, ncol), BF16),
                pltpu.SemaphoreType.DMA((3 * N_EXPERTS,)),
            ]),
        out_shape=jax.ShapeDtypeStruct((n_out * TOK_TILE, D_MODEL), F32),
        compiler_params=pltpu.CompilerParams(dimension_semantics=("arbitrary",),
                                             vmem_limit_bytes=VMEM_LIMIT),
        name="combine",
    )(seg, cnt, x2, aff_c, thr_row, eqa_rows, ye)


def _round_up(a, b):
    return -(-a // b) * b


def kernel(x_prompt, x_sample, meta, g_mix, w_in, b_gates, conv_w, g_q, g_k, lam_q1, lam_k1, lam_q2,
           lam_k2, g_da_out, g_ml_out, w_out, g_ffn, w_router, w_gate, w_up, w_down):
    assert x_prompt.shape[1:] == x_sample.shape[1:]
    groups = (x_prompt.shape[0], x_sample.shape[0])
    s = x_prompt.shape[1]
    lp = FRONT + s
    assert s % LANE == 0
    ntl = lp // TOK_TILE
    lamv = jnp.stack([lam_q1[0], lam_k1[0], lam_q2[0], lam_k2[0]])
    x2, xn, aff_c, aff_t = _mixer((x_prompt, x_sample), meta, lp, g_mix[0], w_in[0], b_gates[0],
                                  conv_w[0], g_q[0], g_k[0], lamv, g_da_out[0], g_ml_out[0],
                                  w_out[0], g_ffn[0], w_router[0])

    caps = [max(1, CAP_FACTOR * b * (N_META + s) // N_EXPERTS) for b in groups]
    bases = [sum(caps[:i]) for i in range(len(caps))]
    tail_start = sum(caps)
    stride = _round_up(tail_start, FFN_ROWS)
    infos = []
    tile0 = 0
    for b, cap in zip(groups, caps):
        nt = b * ntl
        thr, cnt, eqa, seg = _route(aff_t[:, tile0 * TOK_TILE:(tile0 + nt) * TOK_TILE], cap)
        ntp = cnt.shape[1]
        thr_row = jnp.pad(thr[:, 0], (0, LANE - N_EXPERTS), constant_values=2.0).reshape(1, LANE)
        eqa_rows = jnp.pad(eqa.T.astype(F32), ((0, 0), (0, LANE - N_EXPERTS))).reshape(ntp, 1, LANE)
        infos.append(dict(tile0=tile0, nt=nt, ntp=ntp, cap=cap, thr=thr, thr_row=thr_row, eqa=eqa,
                          eqa_rows=eqa_rows, seg=seg.reshape(-1), cnt=cnt.reshape(-1)))
        tile0 += nt

    xg = None
    for gi, info in enumerate(infos):
        xg = _dispatch(info["seg"], info["cnt"], xn, aff_t, info["thr"], info["eqa"], xg,
                       tile0=info["tile0"], nt=info["nt"], ntp=info["ntp"], base=bases[gi],
                       stride=stride, tail_start=tail_start)
    w_r = jnp.pad(w_router[0].astype(BF16), ((0, 0), (0, LANE - N_EXPERTS)))
    ye = _ffn(xg, w_gate[0].astype(BF16), w_up[0].astype(BF16), w_down[0].astype(BF16), w_r, stride)

    outs = []
    for gi, (b, info) in enumerate(zip(groups, infos)):
        y = _combine(info["seg"], info["cnt"], x2, aff_c, info["thr_row"], info["eqa_rows"], ye,
                     tile0=info["tile0"], nt=info["nt"], ntp=info["ntp"], ntl=ntl, base=bases[gi],
                     stride=stride)
        outs.append(y.reshape(b, s, D_MODEL))
    return tuple(outs)
```
